```python
import math
import jax, jax.numpy as jnp
from jax import lax
import numpy as np

D_MODEL = 2048
BATCH = 1
SEQ = 16384
DEPTH = 1

DN_HEADS = 8
DN_HEAD_DIM = 128
DN_CHUNK = 64
CONV_WIDTH = 5
SWA_HEADS = 8
SWA_KV_HEADS = 2
SWA_HEAD_DIM = 128
WINDOW = 128
NUM_BUCKETS = 32
MAX_DISTANCE = 128
MEM_TOKENS = 256
MEM_HEADS = 4
MEM_HEAD_DIM = 128
D_FF = -(-8 * D_MODEL // (3 * 256)) * 256
RMS_EPS = 1e-6

DN_QK = DN_HEADS * DN_HEAD_DIM
DN_V = DN_HEADS * DN_HEAD_DIM
SWA_Q = SWA_HEADS * SWA_HEAD_DIM
SWA_KV = SWA_KV_HEADS * SWA_HEAD_DIM
IN_WIDTH = 2 * DN_QK + 2 * DN_V + 4 * DN_HEADS + SWA_Q + 2 * SWA_KV
MIX_WIDTH = DN_V + SWA_Q

kernel_name = "hybrid_deltanet_swa_memory_encoder"


def rmsnorm(x, g):
    xf = x.astype(jnp.float32)
    y = xf * lax.rsqrt(jnp.mean(xf * xf, axis=-1, keepdims=True) + RMS_EPS)
    return (y * g.astype(jnp.float32)).astype(x.dtype)


def l2norm(x):
    return x * lax.rsqrt(jnp.sum(x * x, axis=-1, keepdims=True) + 1e-6)


def split_proj(proj):
    sizes = [DN_QK + DN_QK + DN_V, DN_V, DN_HEADS, DN_HEADS, DN_HEADS, DN_HEADS, SWA_Q, SWA_KV, SWA_KV]
    out, off = [], 0
    for s in sizes:
        out.append(proj[..., off:off + s])
        off += s
    return out


def short_conv(u, w):
    C = u.shape[-1]
    pad = (CONV_WIDTH - 1) // 2
    out = lax.conv_general_dilated(
        u, w[:, None, :].astype(u.dtype), window_strides=(1,), padding=[(pad, pad)],
        dimension_numbers=('NWC', 'WIO', 'NWC'), feature_group_count=C)
    return jax.nn.silu(out)


def chunk_gated_delta(q, k, v, beta, g):
    B, H, S, dk = q.shape
    dv = v.shape[-1]
    C = DN_CHUNK
    N = S // C
    q = q.reshape(B, H, N, C, dk)
    k = k.reshape(B, H, N, C, dk)
    v = v.reshape(B, H, N, C, dv)
    beta = beta.reshape(B, H, N, C)
    g = jnp.cumsum(g.reshape(B, H, N, C), axis=-1)
    tril = jnp.tril(jnp.ones((C, C), dtype=bool))
    tril_strict = jnp.tril(jnp.ones((C, C), dtype=bool), -1)
    decay = jnp.exp(jnp.where(tril, g[..., :, None] - g[..., None, :], -jnp.inf))
    k_beta = k * beta[..., None]
    v_beta = v * beta[..., None]
    m = jnp.where(tril_strict, jnp.einsum('bhncd,bhnsd->bhncs', k_beta, k) * decay, 0.0)
    eye = jnp.eye(C, dtype=q.dtype)
    t_mat = lax.linalg.triangular_solve(eye + m, jnp.broadcast_to(eye, m.shape),
                                        left_side=True, lower=True, unit_diagonal=True)
    u = jnp.einsum('bhncs,bhnsd->bhncd', t_mat, v_beta)
    w = jnp.einsum('bhncs,bhnsd->bhncd', t_mat, k_beta * jnp.exp(g)[..., None])
    a_intra = jnp.einsum('bhncd,bhnsd->bhncs', q, k) * decay
    q_dec = q * jnp.exp(g)[..., None]
    k_dec = k * jnp.exp(g[..., -1:] - g)[..., None]
    g_last = jnp.exp(g[..., -1])

    def step(state, inp):
        q_c, k_c, u_c, w_c, a_c, gl_c = inp
        v_new = u_c - jnp.einsum('bhck,bhkv->bhcv', w_c, state)
        o_c = jnp.einsum('bhck,bhkv->bhcv', q_c, state) + jnp.einsum('bhcs,bhsv->bhcv', a_c, v_new)
        state = state * gl_c[..., None, None] + jnp.einsum('bhck,bhcv->bhkv', k_c, v_new)
        return state, o_c

    xs = tuple(jnp.moveaxis(t, 2, 0) for t in (q_dec, k_dec, u, w, a_intra, g_last))
    s0 = jnp.zeros((B, H, dk, dv), dtype=q.dtype)
    _, o = lax.scan(step, s0, xs)
    return jnp.moveaxis(o, 0, 2).reshape(B, H, S, dv)


def bidir_gated_deltanet(qkv, z, b_f, b_b, a_f, a_b, a_log_f, a_log_b, dt_bias_f, dt_bias_b, norm_g):
    B, S, _ = z.shape
    f32 = jnp.float32
    H, d = DN_HEADS, DN_HEAD_DIM
    qkv = qkv.astype(f32)
    q = l2norm(qkv[..., :DN_QK].reshape(B, S, H, d)) * (d ** -0.5)
    k = l2norm(qkv[..., DN_QK:2 * DN_QK].reshape(B, S, H, d))
    v = qkv[..., 2 * DN_QK:].reshape(B, S, H, d)

    def log_decay(a, a_log, dt_bias):
        return -jnp.exp(a_log.astype(f32)) * jax.nn.softplus(a.astype(f32) + dt_bias.astype(f32))

    to_bhs = lambda t: jnp.moveaxis(t, 2, 1)
    rev = lambda t: jnp.flip(t, axis=2)
    qh, kh, vh = to_bhs(q), to_bhs(k), to_bhs(v)
    beta_f = to_bhs(jax.nn.sigmoid(b_f.astype(f32)))
    beta_b = to_bhs(jax.nn.sigmoid(b_b.astype(f32)))
    g_f = to_bhs(log_decay(a_f, a_log_f, dt_bias_f))
    g_b = to_bhs(log_decay(a_b, a_log_b, dt_bias_b))
    o_fwd = chunk_gated_delta(qh, kh, vh, beta_f, g_f)
    o_bwd = rev(chunk_gated_delta(rev(qh), rev(kh), rev(vh), rev(beta_b), rev(g_b)))
    o = jnp.moveaxis(o_fwd + o_bwd, 1, 2)
    o = rmsnorm(o, norm_g) * jax.nn.silu(z.astype(f32).reshape(B, S, H, d))
    return o.reshape(B, S, DN_V).astype(z.dtype)


def t5_bucket(rel):
    nb = NUM_BUCKETS // 2
    max_exact = nb // 2
    n = jnp.abs(rel)
    large = max_exact + (jnp.log(jnp.maximum(n, max_exact).astype(jnp.float32) / max_exact)
                         / math.log(MAX_DISTANCE / max_exact) * (nb - max_exact)).astype(jnp.int32)
    large = jnp.minimum(large, nb - 1)
    return jnp.where(rel > 0, nb, 0) + jnp.where(n < max_exact, n, large)


def window_rel_bias(rel_bias):
    rel = (jnp.arange(3 * WINDOW)[None, :] - WINDOW) - jnp.arange(WINDOW)[:, None]
    bias = rel_bias.astype(jnp.float32)[t5_bucket(rel)]
    bias = jnp.transpose(bias, (2, 0, 1))
    return bias.reshape(SWA_KV_HEADS, SWA_HEADS // SWA_KV_HEADS, WINDOW, 3 * WINDOW)


def window_gqa(q, k, v, sink, bias):
    B, S, _ = q.shape
    W, d, Hkv = WINDOW, SWA_HEAD_DIM, SWA_KV_HEADS
    G = SWA_HEADS // Hkv
    nb = S // W
    qb = q.reshape(B, nb, W, Hkv, G, d)

    def band(t):
        tp = jnp.pad(t.reshape(B, S, Hkv, d), ((0, 0), (W, W), (0, 0), (0, 0)))
        tp = tp.reshape(B, nb + 2, W, Hkv, d)
        return jnp.concatenate([tp[:, :-2], tp[:, 1:-1], tp[:, 2:]], axis=2)

    kb, vb = band(k), band(v)
    s = jnp.einsum('bnqkgd,bnjkd->bnkgqj', qb, kb).astype(jnp.float32) * (d ** -0.5) + bias
    rel = (jnp.arange(3 * W)[None, :] - W) - jnp.arange(W)[:, None]
    key_pos = jnp.arange(nb)[:, None, None] * W + jnp.arange(3 * W)[None, None, :] - W
    valid = (jnp.abs(rel) <= W)[None] & (key_pos >= 0) & (key_pos < S)
    s = jnp.where(valid[None, :, None, None], s, -jnp.inf)
    sk = sink.astype(jnp.float32).reshape(Hkv, G)[None, None, :, :, None, None]
    mx = jnp.maximum(jnp.max(s, axis=-1, keepdims=True), sk)
    p = jnp.exp(s - mx)
    p = p / (jnp.sum(p, axis=-1, keepdims=True) + jnp.exp(sk - mx))
    o = jnp.einsum('bnkgqj,bnjkd->bnqkgd', p.astype(vb.dtype), vb)
    return o.reshape(B, S, SWA_Q)


def memory_cross_attn(hn, memn, w_q, w_kv, w_o):
    B, S, _ = hn.shape
    M = memn.shape[1]
    q = (hn @ w_q).reshape(B, S, MEM_HEADS, MEM_HEAD_DIM)
    kv = memn @ w_kv
    k = kv[..., :MEM_HEADS * MEM_HEAD_DIM].reshape(B, M, MEM_HEADS, MEM_HEAD_DIM)
    v = kv[..., MEM_HEADS * MEM_HEAD_DIM:].reshape(B, M, MEM_HEADS, MEM_HEAD_DIM)
    s = jnp.einsum('bshd,bmhd->bhsm', q, k).astype(jnp.float32) * (MEM_HEAD_DIM ** -0.5)
    p = jax.nn.softmax(s, axis=-1).astype(v.dtype)
    o = jnp.einsum('bhsm,bmhd->bshd', p, v).reshape(B, S, MEM_HEADS * MEM_HEAD_DIM)
    return o @ w_o


def setup_inputs(seed: int = 0) -> dict:
    key = jax.random.key(seed)
    ks = jax.random.split(key, 24)
    L = DEPTH

    def dense(k, shape, fan_in):
        return jax.random.normal(k, shape, jnp.float32) * (fan_in ** -0.5)

    def gain(k, shape):
        return 1.0 + 0.02 * jax.random.normal(k, shape, jnp.float32)

    def dt_bias(k):
        u = jax.random.uniform(k, (L, DN_HEADS), jnp.float32)
        dt = jnp.exp(u * (math.log(0.1) - math.log(0.001)) + math.log(0.001))
        return dt + jnp.log(-jnp.expm1(-dt))

    def a_log(k):
        return jnp.log(jax.random.uniform(k, (L, DN_HEADS), jnp.float32, minval=1.0, maxval=16.0))

    return {
        "x": jax.random.normal(ks[0], (BATCH, SEQ, D_MODEL), jnp.float32),
        "mem": jax.random.normal(ks[1], (BATCH, MEM_TOKENS, D_MODEL), jnp.float32),
        "norm_mix_g": gain(ks[2], (L, D_MODEL)),
        "w_in": dense(ks[3], (L, D_MODEL, IN_WIDTH), D_MODEL),
        "conv_w": dense(ks[4], (L, CONV_WIDTH, 3 * DN_QK), CONV_WIDTH),
        "a_log_f": a_log(ks[5]),
        "a_log_b": a_log(ks[6]),
        "dt_bias_f": dt_bias(ks[7]),
        "dt_bias_b": dt_bias(ks[8]),
        "dn_norm_g": gain(ks[9], (L, DN_HEAD_DIM)),
        "attn_sink": 0.5 * jax.random.normal(ks[10], (L, SWA_HEADS), jnp.float32),
        "rel_bias": 0.2 * jax.random.normal(ks[11], (NUM_BUCKETS, SWA_HEADS), jnp.float32),
        "w_out": dense(ks[12], (L, MIX_WIDTH, D_MODEL), MIX_WIDTH),
        "norm_x_g": gain(ks[13], (L, D_MODEL)),
        "norm_mem_g": gain(ks[14], (L, D_MODEL)),
        "w_q_mem": dense(ks[15], (L, D_MODEL, MEM_HEADS * MEM_HEAD_DIM), D_MODEL),
        "w_kv_mem": dense(ks[16], (L, D_MODEL, 2 * MEM_HEADS * MEM_HEAD_DIM), D_MODEL),
        "w_o_mem": dense(ks[17], (L, MEM_HEADS * MEM_HEAD_DIM, D_MODEL), MEM_HEADS * MEM_HEAD_DIM),
        "norm_ffn_g": gain(ks[18], (L, D_MODEL)),
        "w_gate": dense(ks[19], (L, D_MODEL, D_FF), D_MODEL),
        "w_up": dense(ks[20], (L, D_MODEL, D_FF), D_MODEL),
        "w_down": dense(ks[21], (L, D_FF, D_MODEL), D_FF),
        "norm_final_g": gain(ks[22], (D_MODEL,)),
    }


def reference(x, mem, norm_mix_g, w_in, conv_w, a_log_f, a_log_b, dt_bias_f, dt_bias_b,
              dn_norm_g, attn_sink, rel_bias, w_out, norm_x_g, norm_mem_g, w_q_mem,
              w_kv_mem, w_o_mem, norm_ffn_g, w_gate, w_up, w_down, norm_final_g):
    bias_win = window_rel_bias(rel_bias)
    h = x
    for l in range(DEPTH):
        n = rmsnorm(h, norm_mix_g[l])
        proj = n @ w_in[l]
        qkv_dn, z_dn, b_f, b_b, a_f, a_b, q_sw, k_sw, v_sw = split_proj(proj)
        qkv_dn = short_conv(qkv_dn, conv_w[l])
        y_dn = bidir_gated_deltanet(qkv_dn, z_dn, b_f, b_b, a_f, a_b, a_log_f[l], a_log_b[l],
                                    dt_bias_f[l], dt_bias_b[l], dn_norm_g[l])
        y_sw = window_gqa(q_sw, k_sw, v_sw, attn_sink[l], bias_win).astype(y_dn.dtype)
        h = h + jnp.concatenate([y_dn, y_sw], axis=-1) @ w_out[l]
        h = h + memory_cross_attn(rmsnorm(h, norm_x_g[l]), rmsnorm(mem, norm_mem_g[l]),
                                  w_q_mem[l], w_kv_mem[l], w_o_mem[l])
        f = rmsnorm(h, norm_ffn_g[l])
        h = h + (jax.nn.silu(f @ w_gate[l]) * (f @ w_up[l])) @ w_down[l]
    return rmsnorm(h, norm_final_g)
```

```python
import functools
import math

import jax
import jax.numpy as jnp
from jax import lax
from jax.experimental import pallas as pl
from jax.experimental.pallas import tpu as pltpu

F32 = jnp.float32
BF16 = jnp.bfloat16

RMS_EPS = 1e-6
L2_EPS = 1e-6
HEAD_DIM = 128
DN_HEADS = 8
DN_CHUNK = 64
DN_SUB = 16
CONV_WIDTH = 5
CONV_HALO = 8
SWA_HEADS = 8
SWA_KV_HEADS = 2
WINDOW = 128
NUM_BUCKETS = 32
MAX_DISTANCE = 128
MEM_HEADS = 4
LANES = 128

DN_QKV = 3 * DN_HEADS * HEAD_DIM
Z_OFF = DN_QKV
QSW_OFF = Z_OFF + DN_HEADS * HEAD_DIM
KSW_OFF = QSW_OFF + SWA_HEADS * HEAD_DIM
VSW_OFF = KSW_OFF + SWA_KV_HEADS * HEAD_DIM
GATE_OFF = VSW_OFF + SWA_KV_HEADS * HEAD_DIM
PROJ_WIDTH = GATE_OFF + LANES
GATE_BETA_F, GATE_BETA_B, GATE_G_F, GATE_G_B = 0, DN_HEADS, 2 * DN_HEADS, 3 * DN_HEADS
GATE_ROWS = 4 * DN_HEADS

VMEM_LIMIT_V7X = 56 * 1024 * 1024


def _params(*sem):
    return pltpu.CompilerParams(dimension_semantics=sem, vmem_limit_bytes=VMEM_LIMIT_V7X)


def _dot(a, b):
    return jnp.dot(a, b, preferred_element_type=F32)


def _dot_nt(a, b):
    return lax.dot_general(a, b, (((1,), (1,)), ((), ())), preferred_element_type=F32)


def _dot_tn(a, b):
    return lax.dot_general(a, b, (((0,), (0,)), ((), ())), preferred_element_type=F32)


def _rms(x, g):
    return x * lax.rsqrt(jnp.mean(x * x, axis=-1, keepdims=True) + RMS_EPS) * g


def _rms_matmul_kernel(x_ref, g_ref, w_ref, o_ref, n_ref):
    @pl.when(pl.program_id(1) == 0)
    def _():
        n_ref[...] = _rms(x_ref[...], g_ref[...]).astype(n_ref.dtype)

    o_ref[...] = _dot(n_ref[...], w_ref[...]).astype(o_ref.dtype)


def _rms_matmul(x, g, w, *, tm, tn, name):
    m, k = x.shape
    n = w.shape[1]
    return pl.pallas_call(
        _rms_matmul_kernel,
        grid=(m // tm, n // tn),
        in_specs=[pl.BlockSpec((tm, k), lambda i, j: (i, 0)),
                  pl.BlockSpec((1, k), lambda i, j: (0, 0)),
                  pl.BlockSpec((k, tn), lambda i, j: (0, j))],
        out_specs=pl.BlockSpec((tm, tn), lambda i, j: (i, j)),
        out_shape=jax.ShapeDtypeStruct((m, n), F32),
        scratch_shapes=[pltpu.VMEM((tm, k), BF16)],
        compiler_params=_params("parallel", "arbitrary"),
        name=name,
    )(x, g.reshape(1, k), w)


def _split3(x):
    hi = x.astype(BF16)
    r = x - hi.astype(F32)
    mid = r.astype(BF16)
    lo = (r - mid.astype(F32)).astype(BF16)
    return hi, mid, lo


def _prep_kernel(main_ref, prev_ref, next_ref, gate_ref, cw_ref, alog_ref, dt_ref,
                 q_ref, k_ref, v_ref, go_ref, gt_ref, ext_ref):
    i = pl.program_id(0)
    rows = main_ref.shape[0]
    halo = CONV_HALO
    pad = (CONV_WIDTH - 1) // 2

    ext_ref[0:halo, :] = jnp.where(i > 0, prev_ref[...], 0.0)
    ext_ref[halo:halo + rows, :] = main_ref[...]
    ext_ref[halo + rows:, :] = jnp.where(i < pl.num_programs(0) - 1, next_ref[...], 0.0)

    for s in range(3 * DN_HEADS):
        cols = slice(s * HEAD_DIM, (s + 1) * HEAD_DIM)
        acc = cw_ref[0:1, cols] * ext_ref[halo - pad:halo - pad + rows, cols]
        for j in range(1, CONV_WIDTH):
            acc = acc + cw_ref[j:j + 1, cols] * ext_ref[halo - pad + j:halo - pad + j + rows, cols]
        y = acc * jax.nn.sigmoid(acc)
        if s < 2 * DN_HEADS:
            y = y * lax.rsqrt(jnp.sum(y * y, axis=-1, keepdims=True) + L2_EPS)
        if s < DN_HEADS:
            q_ref[:, cols] = y * (HEAD_DIM ** -0.5)
        elif s < 2 * DN_HEADS:
            k_ref[:, slice((s - DN_HEADS) * HEAD_DIM, (s - DN_HEADS + 1) * HEAD_DIM)] = y
        else:
            v_ref[:, slice((s - 2 * DN_HEADS) * HEAD_DIM, (s - 2 * DN_HEADS + 1) * HEAD_DIM)] = y

    t = gate_ref[...]
    beta = jax.nn.sigmoid(t)
    a = t + dt_ref[...]
    softplus = jnp.maximum(a, 0.0) + jnp.log1p(jnp.exp(-jnp.abs(a)))
    g = -jnp.exp(alog_ref[...]) * softplus

    ri = lax.broadcasted_iota(jnp.int32, (rows, rows), 0)
    ci = lax.broadcasted_iota(jnp.int32, (rows, rows), 1)
    shift = DN_CHUNK.bit_length() - 1
    same_chunk = (ri >> shift) == (ci >> shift)
    lower = jnp.where(same_chunk & (ci <= ri), 1.0, 0.0).astype(BF16)
    upper = jnp.where(same_chunk & (ci >= ri), 1.0, 0.0).astype(BF16)
    parts = _split3(g)
    gc_f = _dot(lower, parts[0]) + _dot(lower, parts[1]) + _dot(lower, parts[2])
    gc_b = _dot(upper, parts[0]) + _dot(upper, parts[1]) + _dot(upper, parts[2])

    col = lax.broadcasted_iota(jnp.int32, t.shape, 1)
    out = jnp.where(col < GATE_G_F, beta, jnp.where(col < GATE_G_B, gc_f, gc_b))
    go_ref[...] = out
    gt_ref[...] = out.T[0:GATE_ROWS, :]


def _dn_prep(proj, conv_w, alog_row, dt_row, *, rows):
    s = proj.shape[0]
    nblk = s // rows
    hb = rows // CONV_HALO
    last_halo = s // CONV_HALO - 1
    head_cols = DN_HEADS * HEAD_DIM
    return pl.pallas_call(
        _prep_kernel,
        grid=(nblk,),
        in_specs=[pl.BlockSpec((rows, DN_QKV), lambda i: (i, 0)),
                  pl.BlockSpec((CONV_HALO, DN_QKV), lambda i: (jnp.maximum(i * hb - 1, 0), 0)),
                  pl.BlockSpec((CONV_HALO, DN_QKV), lambda i: (jnp.minimum((i + 1) * hb, last_halo), 0)),
                  pl.BlockSpec((rows, LANES), lambda i: (i, GATE_OFF // LANES)),
                  pl.BlockSpec((CONV_WIDTH, DN_QKV), lambda i: (0, 0)),
                  pl.BlockSpec((1, LANES), lambda i: (0, 0)),
                  pl.BlockSpec((1, LANES), lambda i: (0, 0))],
        out_specs=[pl.BlockSpec((rows, head_cols), lambda i: (i, 0)),
                   pl.BlockSpec((rows, head_cols), lambda i: (i, 0)),
                   pl.BlockSpec((rows, head_cols), lambda i: (i, 0)),
                   pl.BlockSpec((rows, LANES), lambda i: (i, 0)),
                   pl.BlockSpec((GATE_ROWS, rows), lambda i: (0, i))],
        out_shape=[jax.ShapeDtypeStruct((s, head_cols), F32),
                   jax.ShapeDtypeStruct((s, head_cols), F32),
                   jax.ShapeDtypeStruct((s, head_cols), F32),
                   jax.ShapeDtypeStruct((s, LANES), F32),
                   jax.ShapeDtypeStruct((GATE_ROWS, s), F32)],
        scratch_shapes=[pltpu.VMEM((rows + 2 * CONV_HALO, DN_QKV), F32)],
        compiler_params=_params("parallel"),
        name="dn_prep",
    )(proj, proj, proj, proj, conv_w, alog_row, dt_row)


def _mm(a, b):
    return _dot(a.astype(BF16), b.astype(BF16))


def _unit_tri_inverse(m, eye, same_sub):
    md = jnp.where(same_sub, m, 0.0)
    e = m - md
    m2 = _mm(md, md)
    m4 = _mm(m2, m2)
    m8 = _mm(m4, m4)
    x = eye - md
    x = x + _mm(x, m2)
    x = x + _mm(x, m4)
    x = x + _mm(x, m8)
    n = _mm(x, e)
    n2 = _mm(n, n)
    y = (n2 - n) - _mm(n, n2)
    return x + _mm(y, x)


def _dn_direction(h, q_ref, k_ref, v_ref, g_ref, gt_ref, o_ref, s_ref, consts, reverse):
    eye, same_sub, ri, ci, lane = consts
    c_len = DN_CHUNK
    nchunk = q_ref.shape[0] // c_len
    incl = (ci >= ri) if reverse else (ci <= ri)
    strict = (ci > ri) if reverse else (ci < ri)
    beta_col = h + (GATE_BETA_B if reverse else GATE_BETA_F)
    g_col = h + (GATE_G_B if reverse else GATE_G_F)
    order = range(nchunk - 1, -1, -1) if reverse else range(nchunk)
    g_rows = gt_ref[pl.ds(g_col, 1), :]

    pre = {}
    for c in order:
        rows = slice(c * c_len, (c + 1) * c_len)
        qc, kc, vc, gates = q_ref[rows, :], k_ref[rows, :], v_ref[rows, :], g_ref[rows, :]
        beta = jnp.sum(jnp.where(lane == beta_col, gates, 0.0), axis=1, keepdims=True)
        gcol = jnp.sum(jnp.where(lane == g_col, gates, 0.0), axis=1, keepdims=True)
        grow = g_rows[:, rows]
        glast = grow[:, 0:1] if reverse else grow[:, c_len - 1:c_len]
        eg = jnp.exp(gcol)
        kbeta = kc * beta
        vbeta = vc * beta
        kc_b = kc.astype(BF16)
        prod = _dot_nt(jnp.concatenate([kbeta, qc], axis=0).astype(BF16), kc_b)
        decay = jnp.exp(jnp.where(incl, gcol - grow, -jnp.inf))
        m = jnp.where(strict, prod[:c_len] * decay, 0.0)
        a = prod[c_len:] * decay
        t = _unit_tri_inverse(m, eye, same_sub)
        uw = _mm(t, jnp.concatenate([vbeta, kbeta * eg], axis=1))
        u, w = uw[:, :HEAD_DIM], uw[:, HEAD_DIM:]
        wq = jnp.concatenate([w, qc * eg], axis=0).astype(BF16)
        kdec = (kc * jnp.exp(glast - gcol)).astype(BF16)
        pre[c] = (u, wq, a.astype(BF16), kdec, jnp.exp(glast))

    state = s_ref[...]
    for c in order:
        u, wq, a, kdec, gl = pre[c]
        ws = _dot(wq, state.astype(BF16))
        v_new = (u - ws[:c_len]).astype(BF16)
        o_ref[c * c_len:(c + 1) * c_len, :] = ws[c_len:] + _dot(a, v_new)
        state = state * gl + _dot_tn(kdec, v_new)
    s_ref[...] = state


def _deltanet_kernel(qf, kf, vf, gf, gtf, qb, kb, vb, gb, gtb, of_ref, ob_ref, sf_ref, sb_ref):
    h = pl.program_id(0)

    @pl.when(pl.program_id(1) == 0)
    def _():
        sf_ref[...] = jnp.zeros_like(sf_ref)
        sb_ref[...] = jnp.zeros_like(sb_ref)

    c_len = DN_CHUNK
    ri = lax.broadcasted_iota(jnp.int32, (c_len, c_len), 0)
    ci = lax.broadcasted_iota(jnp.int32, (c_len, c_len), 1)
    eye = jnp.where(ri == ci, 1.0, 0.0)
    sub_shift = DN_SUB.bit_length() - 1
    same_sub = (ri >> sub_shift) == (ci >> sub_shift)
    lane = lax.broadcasted_iota(jnp.int32, (c_len, LANES), 1)
    consts = (eye, same_sub, ri, ci, lane)
    _dn_direction(h, qf, kf, vf, gf, gtf, of_ref, sf_ref, consts, reverse=False)
    _dn_direction(h, qb, kb, vb, gb, gtb, ob_ref, sb_ref, consts, reverse=True)


def _deltanet(q, k, v, gates, gates_t, *, rows):
    s = q.shape[0]
    nb = s // rows
    fwd = lambda h, b: (b, h)
    bwd = lambda h, b: (nb - 1 - b, h)
    head = lambda im: pl.BlockSpec((rows, HEAD_DIM), im)
    gate = lambda im: pl.BlockSpec((rows, LANES), lambda h, b: (im(h, b)[0], 0))
    gate_t = lambda im: pl.BlockSpec((GATE_ROWS, rows), lambda h, b: (0, im(h, b)[0]))
    return pl.pallas_call(
        _deltanet_kernel,
        grid=(DN_HEADS, nb),
        in_specs=[head(fwd), head(fwd), head(fwd), gate(fwd), gate_t(fwd),
                  head(bwd), head(bwd), head(bwd), gate(bwd), gate_t(bwd)],
        out_specs=[head(fwd), head(bwd)],
        out_shape=[jax.ShapeDtypeStruct(q.shape, F32), jax.ShapeDtypeStruct(q.shape, F32)],
        scratch_shapes=[pltpu.VMEM((HEAD_DIM, HEAD_DIM), F32), pltpu.VMEM((HEAD_DIM, HEAD_DIM), F32)],
        compiler_params=_params("parallel", "arbitrary"),
        name="deltanet",
    )(q, k, v, gates, gates_t, q, k, v, gates, gates_t)


def _t5_bucket(rel):
    nb = NUM_BUCKETS // 2
    max_exact = nb // 2
    n = jnp.abs(rel)
    large = max_exact + (jnp.log(jnp.maximum(n, max_exact).astype(F32) / max_exact)
                         / math.log(MAX_DISTANCE / max_exact) * (nb - max_exact)).astype(jnp.int32)
    large = jnp.minimum(large, nb - 1)
    return jnp.where(rel > 0, nb, 0) + jnp.where(n < max_exact, n, large)


def _swa_kernel(q_ref, kp_ref, kc_ref, kn_ref, vp_ref, vc_ref, vn_ref, bucket_ref, rb_ref, sink_ref,
                o_ref, bias_ref, *, seq):
    kvh = pl.program_id(0)
    n = pl.program_id(1)
    w = WINDOW
    group = SWA_HEADS // SWA_KV_HEADS

    @pl.when(n == 0)
    def _():
        bucket = bucket_ref[...]
        for g in range(group):
            acc = jnp.zeros((w, 3 * w), F32)
            for b in range(NUM_BUCKETS):
                acc = jnp.where(bucket == b, rb_ref[b, kvh * group + g], acc)
            bias_ref[g] = acc

    kcat = jnp.concatenate([kp_ref[...], kc_ref[...], kn_ref[...]], axis=0).astype(BF16)
    vcat = jnp.concatenate([vp_ref[...], vc_ref[...], vn_ref[...]], axis=0).astype(BF16)
    ri = lax.broadcasted_iota(jnp.int32, (w, 3 * w), 0)
    ci = lax.broadcasted_iota(jnp.int32, (w, 3 * w), 1)
    key_pos = n * w + ci - w
    valid = (jnp.abs(ci - w - ri) <= w) & (key_pos >= 0) & (key_pos < seq)
    for g in range(group):
        cols = slice(g * HEAD_DIM, (g + 1) * HEAD_DIM)
        s = _dot_nt(q_ref[:, cols].astype(BF16), kcat) * (HEAD_DIM ** -0.5) + bias_ref[g]
        s = jnp.where(valid, s, -jnp.inf)
        sink = sink_ref[kvh * group + g]
        mx = jnp.maximum(jnp.max(s, axis=1, keepdims=True), sink)
        p = jnp.exp(s - mx)
        den = jnp.sum(p, axis=1, keepdims=True) + jnp.exp(sink - mx)
        o_ref[:, cols] = (_dot(p.astype(BF16), vcat) / den).astype(o_ref.dtype)


def _swa(proj, bucket, rel_bias, sink):
    s = proj.shape[0]
    w = WINDOW
    nb = s // w
    group = SWA_HEADS // SWA_KV_HEADS
    qw = group * HEAD_DIM
    kblk = lambda off, shift: pl.BlockSpec(
        (w, HEAD_DIM), lambda kvh, n: (jnp.clip(n + shift, 0, nb - 1), off // HEAD_DIM + kvh))
    return pl.pallas_call(
        functools.partial(_swa_kernel, seq=s),
        grid=(SWA_KV_HEADS, nb),
        in_specs=[pl.BlockSpec((w, qw), lambda kvh, n: (n, QSW_OFF // qw + kvh)),
                  kblk(KSW_OFF, -1), kblk(KSW_OFF, 0), kblk(KSW_OFF, 1),
                  kblk(VSW_OFF, -1), kblk(VSW_OFF, 0), kblk(VSW_OFF, 1),
                  pl.BlockSpec((w, 3 * w), lambda kvh, n: (0, 0)),
                  pl.BlockSpec(memory_space=pltpu.SMEM),
                  pl.BlockSpec(memory_space=pltpu.SMEM)],
        out_specs=pl.BlockSpec((w, qw), lambda kvh, n: (n, kvh)),
        out_shape=jax.ShapeDtypeStruct((s, SWA_HEADS * HEAD_DIM), BF16),
        scratch_shapes=[pltpu.VMEM((group, w, 3 * w), F32)],
        compiler_params=_params("parallel", "arbitrary"),
        name="swa",
    )(proj, proj, proj, proj, proj, proj, proj, bucket, rel_bias, sink)


def _out_proj_kernel(of_ref, ob_ref, z_ref, sw_ref, g_ref, w_ref, x_ref, o_ref, lhs_ref):
    @pl.when(pl.program_id(1) == 0)
    def _():
        for hd in range(DN_HEADS):
            cols = slice(hd * HEAD_DIM, (hd + 1) * HEAD_DIM)
            z = z_ref[:, cols]
            y = _rms(of_ref[:, cols] + ob_ref[:, cols], g_ref[...]) * (z * jax.nn.sigmoid(z))
            lhs_ref[:, cols] = y.astype(lhs_ref.dtype)
        lhs_ref[:, DN_HEADS * HEAD_DIM:] = sw_ref[...]

    o_ref[...] = x_ref[...] + _dot(lhs_ref[...], w_ref[...])


def _out_proj(o_f, o_b, proj, y_sw, dn_g, w_out, x, *, tm, tn):
    s, d = x.shape
    dn = DN_HEADS * HEAD_DIM
    mix = w_out.shape[0]
    return pl.pallas_call(
        _out_proj_kernel,
        grid=(s // tm, d // tn),
        in_specs=[pl.BlockSpec((tm, dn), lambda i, j: (i, 0)),
                  pl.BlockSpec((tm, dn), lambda i, j: (i, 0)),
                  pl.BlockSpec((tm, dn), lambda i, j: (i, Z_OFF // dn)),
                  pl.BlockSpec((tm, mix - dn), lambda i, j: (i, 0)),
                  pl.BlockSpec((1, HEAD_DIM), lambda i, j: (0, 0)),
                  pl.BlockSpec((mix, tn), lambda i, j: (0, j)),
                  pl.BlockSpec((tm, tn), lambda i, j: (i, j))],
        out_specs=pl.BlockSpec((tm, tn), lambda i, j: (i, j)),
        out_shape=jax.ShapeDtypeStruct((s, d), F32),
        scratch_shapes=[pltpu.VMEM((tm, mix), BF16)],
        compiler_params=_params("parallel", "arbitrary"),
        name="out_proj",
    )(o_f, o_b, proj, y_sw, dn_g.reshape(1, HEAD_DIM), w_out, x)


def _mem_attn_kernel(h_ref, gx_ref, wq_ref, k_ref, v_ref, wo_ref, gf_ref, h2_ref, f_ref):
    h = h_ref[...]
    q = _dot(_rms(h, gx_ref[...]).astype(BF16), wq_ref[...])
    heads = []
    for hd in range(MEM_HEADS):
        cols = slice(hd * HEAD_DIM, (hd + 1) * HEAD_DIM)
        s = _dot_nt(q[:, cols].astype(BF16), k_ref[:, cols]) * (HEAD_DIM ** -0.5)
        p = jnp.exp(s - jnp.max(s, axis=1, keepdims=True))
        den = jnp.sum(p, axis=1, keepdims=True)
        heads.append((_dot(p.astype(BF16), v_ref[:, cols]) / den).astype(BF16))
    h2 = h + _dot(jnp.concatenate(heads, axis=1), wo_ref[...])
    h2_ref[...] = h2
    f_ref[...] = _rms(h2, gf_ref[...]).astype(f_ref.dtype)


def _mem_attn(h, gx, wq, k, v, wo, gf, *, tm):
    s, d = h.shape
    full = lambda a: pl.BlockSpec(a.shape, lambda i: (0, 0))
    gx, gf = gx.reshape(1, d), gf.reshape(1, d)
    return pl.pallas_call(
        _mem_attn_kernel,
        grid=(s // tm,),
        in_specs=[pl.BlockSpec((tm, d), lambda i: (i, 0)), full(gx), full(wq), full(k), full(v), full(wo),
                  full(gf)],
        out_specs=[pl.BlockSpec((tm, d), lambda i: (i, 0)), pl.BlockSpec((tm, d), lambda i: (i, 0))],
        out_shape=[jax.ShapeDtypeStruct((s, d), F32), jax.ShapeDtypeStruct((s, d), BF16)],
        compiler_params=_params("parallel"),
        name="mem_attn",
    )(h, gx, wq, k, v, wo, gf)


def _glu_kernel(f_ref, wg_ref, wu_ref, o_ref):
    f = f_ref[...]
    a = _dot(f, wg_ref[...])
    o_ref[...] = (a * jax.nn.sigmoid(a) * _dot(f, wu_ref[...])).astype(o_ref.dtype)


def _ffn_glu(f, wg, wu, *, tm, tn):
    s, d = f.shape
    dff = wg.shape[1]
    return pl.pallas_call(
        _glu_kernel,
        grid=(s // tm, dff // tn),
        in_specs=[pl.BlockSpec((tm, d), lambda i, j: (i, 0)),
                  pl.BlockSpec((d, tn), lambda i, j: (0, j)),
                  pl.BlockSpec((d, tn), lambda i, j: (0, j))],
        out_specs=pl.BlockSpec((tm, tn), lambda i, j: (i, j)),
        out_shape=jax.ShapeDtypeStruct((s, dff), BF16),
        compiler_params=_params("parallel", "parallel"),
        name="ffn_glu",
    )(f, wg, wu)


def _down_kernel(a_ref, w_ref, h_ref, g_ref, o_ref, acc_ref):
    k = pl.program_id(1)

    @pl.when(k == 0)
    def _():
        acc_ref[...] = h_ref[...]

    acc_ref[...] += _dot(a_ref[...], w_ref[...])

    @pl.when(k == pl.num_programs(1) - 1)
    def _():
        o_ref[...] = _rms(acc_ref[...], g_ref[...])


def _ffn_down(act, wd, h, g, *, tm, tk):
    s, dff = act.shape
    d = wd.shape[1]
    return pl.pallas_call(
        _down_kernel,
        grid=(s // tm, dff // tk),
        in_specs=[pl.BlockSpec((tm, tk), lambda i, k: (i, k)),
                  pl.BlockSpec((tk, d), lambda i, k: (k, 0)),
                  pl.BlockSpec((tm, d), lambda i, k: (i, 0)),
                  pl.BlockSpec((1, d), lambda i, k: (0, 0))],
        out_specs=pl.BlockSpec((tm, d), lambda i, k: (i, 0)),
        out_shape=jax.ShapeDtypeStruct((s, d), F32),
        scratch_shapes=[pltpu.VMEM((tm, d), F32)],
        compiler_params=_params("parallel", "arbitrary"),
        name="ffn_down",
    )(act, wd, h, g.reshape(1, d))


def _gate_row(fwd, bwd):
    row = jnp.zeros((1, LANES), F32)
    row = row.at[0, GATE_G_F:GATE_G_F + DN_HEADS].set(fwd.astype(F32))
    return row.at[0, GATE_G_B:GATE_G_B + DN_HEADS].set(bwd.astype(F32))


def _pick(n, *cands):
    for c in cands:
        if n % c == 0:
            return c
    return n


def kernel(x, mem, norm_mix_g, w_in, conv_w, a_log_f, a_log_b, dt_bias_f, dt_bias_b, dn_norm_g, attn_sink, rel_bias, w_out, norm_x_g, norm_mem_g, w_q_mem, w_kv_mem, w_o_mem, norm_ffn_g, w_gate, w_up, w_down, norm_final_g):
    batch, s, d = x.shape
    assert batch == 1 and mem.shape[0] == 1
    depth = w_in.shape[0]
    w = WINDOW
    rel = (jnp.arange(3 * w)[None, :] - w) - jnp.arange(w)[:, None]
    bucket = _t5_bucket(rel).astype(jnp.int32)
    gate_lo = DN_QKV + DN_HEADS * HEAD_DIM
    mem_dim = MEM_HEADS * HEAD_DIM

    h = x.reshape(s, d)
    mem2 = mem.reshape(mem.shape[1], d)
    tm_big = _pick(s, 1024, 512, 256, 128)
    tm_mid = _pick(s, 512, 256, 128)
    out = None
    for l in range(depth):
        wl = w_in[l]
        w_r = jnp.concatenate([wl[:, :gate_lo], wl[:, gate_lo + GATE_ROWS:], wl[:, gate_lo:gate_lo + GATE_ROWS],
                               jnp.zeros((d, LANES - GATE_ROWS), wl.dtype)], axis=1).astype(BF16)
        proj = _rms_matmul(h, norm_mix_g[l], w_r, tm=tm_big, tn=_pick(PROJ_WIDTH, 1152, 640, 128), name="in_proj")

        q, k, v, gates, gates_t = _dn_prep(proj, conv_w[l], _gate_row(a_log_f[l], a_log_b[l]),
                                           _gate_row(dt_bias_f[l], dt_bias_b[l]), rows=_pick(s, 256, 128))
        o_f, o_b = _deltanet(q, k, v, gates, gates_t, rows=_pick(s, 256, 128))
        y_sw = _swa(proj, bucket, rel_bias.astype(F32), attn_sink[l].astype(F32))
        h = _out_proj(o_f, o_b, proj, y_sw, dn_norm_g[l], w_out[l].astype(BF16), h, tm=tm_mid, tn=1024)

        kv = _rms_matmul(mem2, norm_mem_g[l], w_kv_mem[l].astype(BF16), tm=mem2.shape[0], tn=2 * mem_dim,
                         name="mem_kv").astype(BF16)
        h, f = _mem_attn(h, norm_x_g[l], w_q_mem[l].astype(BF16), kv[:, :mem_dim], kv[:, mem_dim:],
                         w_o_mem[l].astype(BF16), norm_ffn_g[l], tm=tm_mid)

        act = _ffn_glu(f, w_gate[l].astype(BF16), w_up[l].astype(BF16), tm=tm_big, tn=512)
        last = l == depth - 1
        g_last = norm_final_g if last else jnp.ones((d,), F32)
        out = _ffn_down(act, w_down[l].astype(BF16), h, g_last, tm=tm_mid, tk=1408)
        h = out
    return out.reshape(batch, s, d)
```

```python
import functools
import math

import jax
import jax.numpy as jnp
from jax import lax
from jax.experimental import pallas as pl
from jax.experimental.pallas import tpu as pltpu

F32 = jnp.float32
BF16 = jnp.bfloat16

RMS_EPS = 1e-6
L2_EPS = 1e-6
HEAD_DIM = 128
DN_HEADS = 8
DN_CHUNK = 64
DN_SUB = 16
CONV_WIDTH = 5
CONV_HALO = 8
SWA_HEADS = 8
SWA_KV_HEADS = 2
WINDOW = 128
NUM_BUCKETS = 32
MAX_DISTANCE = 128
MEM_HEADS = 4
LANES = 128

DN_QKV = 3 * DN_HEADS * HEAD_DIM
Z_OFF = DN_QKV
QSW_OFF = Z_OFF + DN_HEADS * HEAD_DIM
KSW_OFF = QSW_OFF + SWA_HEADS * HEAD_DIM
VSW_OFF = KSW_OFF + SWA_KV_HEADS * HEAD_DIM
GATE_OFF = VSW_OFF + SWA_KV_HEADS * HEAD_DIM
PROJ_WIDTH = GATE_OFF + LANES
GATE_BETA_F, GATE_BETA_B, GATE_G_F, GATE_G_B = 0, DN_HEADS, 2 * DN_HEADS, 3 * DN_HEADS
GATE_ROWS = 4 * DN_HEADS

VMEM_LIMIT_V7X = 56 * 1024 * 1024


def _params(*sem):
    return pltpu.CompilerParams(dimension_semantics=sem, vmem_limit_bytes=VMEM_LIMIT_V7X)


def _dot(a, b):
    return jnp.dot(a, b, preferred_element_type=F32)


def _dot_nt(a, b):
    return lax.dot_general(a, b, (((1,), (1,)), ((), ())), preferred_element_type=F32)


def _dot_tn(a, b):
    return lax.dot_general(a, b, (((0,), (0,)), ((), ())), preferred_element_type=F32)


def _rms(x, g):
    return x * lax.rsqrt(jnp.mean(x * x, axis=-1, keepdims=True) + RMS_EPS) * g


def _rms_matmul_kernel(x_ref, g_ref, w_ref, o_ref, n_ref):
    @pl.when(pl.program_id(1) == 0)
    def _():
        n_ref[...] = _rms(x_ref[...], g_ref[...]).astype(n_ref.dtype)

    o_ref[...] = _dot(n_ref[...], w_ref[...]).astype(o_ref.dtype)


def _rms_matmul(x, g, w, *, tm, tn, name):
    m, k = x.shape
    n = w.shape[1]
    return pl.pallas_call(
        _rms_matmul_kernel,
        grid=(m // tm, n // tn),
        in_specs=[pl.BlockSpec((tm, k), lambda i, j: (i, 0)),
                  pl.BlockSpec((1, k), lambda i, j: (0, 0)),
                  pl.BlockSpec((k, tn), lambda i, j: (0, j))],
        out_specs=pl.BlockSpec((tm, tn), lambda i, j: (i, j)),
        out_shape=jax.ShapeDtypeStruct((m, n), F32),
        scratch_shapes=[pltpu.VMEM((tm, k), BF16)],
        compiler_params=_params("parallel", "arbitrary"),
        name=name,
    )(x, g.reshape(1, k), w)


def _split3(x):
    hi = x.astype(BF16)
    r = x - hi.astype(F32)
    mid = r.astype(BF16)
    lo = (r - mid.astype(F32)).astype(BF16)
    return hi, mid, lo


def _prep_kernel(main_ref, prev_ref, next_ref, gate_ref, cw_ref, alog_ref, dt_ref,
                 q_ref, k_ref, v_ref, go_ref, gt_ref, ext_ref):
    i = pl.program_id(0)
    rows = main_ref.shape[0]
    halo = CONV_HALO
    pad = (CONV_WIDTH - 1) // 2

    ext_ref[0:halo, :] = jnp.where(i > 0, prev_ref[...], 0.0)
    ext_ref[halo:halo + rows, :] = main_ref[...]
    ext_ref[halo + rows:, :] = jnp.where(i < pl.num_programs(0) - 1, next_ref[...], 0.0)

    for s in range(3 * DN_HEADS):
        cols = slice(s * HEAD_DIM, (s + 1) * HEAD_DIM)
        acc = cw_ref[0:1, cols] * ext_ref[halo - pad:halo - pad + rows, cols]
        for j in range(1, CONV_WIDTH):
            acc = acc + cw_ref[j:j + 1, cols] * ext_ref[halo - pad + j:halo - pad + j + rows, cols]
        y = acc * jax.nn.sigmoid(acc)
        if s < 2 * DN_HEADS:
            y = y * lax.rsqrt(jnp.sum(y * y, axis=-1, keepdims=True) + L2_EPS)
        if s < DN_HEADS:
            q_ref[:, cols] = y * (HEAD_DIM ** -0.5)
        elif s < 2 * DN_HEADS:
            k_ref[:, slice((s - DN_HEADS) * HEAD_DIM, (s - DN_HEADS + 1) * HEAD_DIM)] = y
        else:
            v_ref[:, slice((s - 2 * DN_HEADS) * HEAD_DIM, (s - 2 * DN_HEADS + 1) * HEAD_DIM)] = y

    t = gate_ref[...]
    beta = jax.nn.sigmoid(t)
    a = t + dt_ref[...]
    softplus = jnp.maximum(a, 0.0) + jnp.log1p(jnp.exp(-jnp.abs(a)))
    g = -jnp.exp(alog_ref[...]) * softplus

    ri = lax.broadcasted_iota(jnp.int32, (rows, rows), 0)
    ci = lax.broadcasted_iota(jnp.int32, (rows, rows), 1)
    shift = DN_CHUNK.bit_length() - 1
    same_chunk = (ri >> shift) == (ci >> shift)
    lower = jnp.where(same_chunk & (ci <= ri), 1.0, 0.0).astype(BF16)
    upper = jnp.where(same_chunk & (ci >= ri), 1.0, 0.0).astype(BF16)
    parts = _split3(g)
    gc_f = _dot(lower, parts[0]) + _dot(lower, parts[1]) + _dot(lower, parts[2])
    gc_b = _dot(upper, parts[0]) + _dot(upper, parts[1]) + _dot(upper, parts[2])

    col = lax.broadcasted_iota(jnp.int32, t.shape, 1)
    out = jnp.where(col < GATE_G_F, beta, jnp.where(col < GATE_G_B, gc_f, gc_b))
    go_ref[...] = out
    gt_ref[...] = out.T[0:GATE_ROWS, :]


def _dn_prep(proj, conv_w, alog_row, dt_row, *, rows):
    s = proj.shape[0]
    nblk = s // rows
    hb = rows // CONV_HALO
    last_halo = s // CONV_HALO - 1
    head_cols = DN_HEADS * HEAD_DIM
    return pl.pallas_call(
        _prep_kernel,
        grid=(nblk,),
        in_specs=[pl.BlockSpec((rows, DN_QKV), lambda i: (i, 0)),
                  pl.BlockSpec((CONV_HALO, DN_QKV), lambda i: (jnp.maximum(i * hb - 1, 0), 0)),
                  pl.BlockSpec((CONV_HALO, DN_QKV), lambda i: (jnp.minimum((i + 1) * hb, last_halo), 0)),
                  pl.BlockSpec((rows, LANES), lambda i: (i, GATE_OFF // LANES)),
                  pl.BlockSpec((CONV_WIDTH, DN_QKV), lambda i: (0, 0)),
                  pl.BlockSpec((1, LANES), lambda i: (0, 0)),
                  pl.BlockSpec((1, LANES), lambda i: (0, 0))],
        out_specs=[pl.BlockSpec((rows, head_cols), lambda i: (i, 0)),
                   pl.BlockSpec((rows, head_cols), lambda i: (i, 0)),
                   pl.BlockSpec((rows, head_cols), lambda i: (i, 0)),
                   pl.BlockSpec((rows, LANES), lambda i: (i, 0)),
                   pl.BlockSpec((GATE_ROWS, rows), lambda i: (0, i))],
        out_shape=[jax.ShapeDtypeStruct((s, head_cols), F32),
                   jax.ShapeDtypeStruct((s, head_cols), F32),
                   jax.ShapeDtypeStruct((s, head_cols), F32),
                   jax.ShapeDtypeStruct((s, LANES), F32),
                   jax.ShapeDtypeStruct((GATE_ROWS, s), F32)],
        scratch_shapes=[pltpu.VMEM((rows + 2 * CONV_HALO, DN_QKV), F32)],
        compiler_params=_params("parallel"),
        name="dn_prep",
    )(proj, proj, proj, proj, conv_w, alog_row, dt_row)


def _mm(a, b):
    return _dot(a.astype(BF16), b.astype(BF16))


def _unit_tri_inverse(ms, eye, same_sub):
    mds = [jnp.where(same_sub, m, 0.0) for m in ms]
    es = [m - md for m, md in zip(ms, mds)]
    m2 = [_mm(md, md) for md in mds]
    xs = [eye - md for md in mds]
    m4 = [_mm(a, a) for a in m2]
    xs = [x + _mm(x, a) for x, a in zip(xs, m2)]
    m8 = [_mm(a, a) for a in m4]
    xs = [x + _mm(x, a) for x, a in zip(xs, m4)]
    xs = [x + _mm(x, a) for x, a in zip(xs, m8)]
    ns = [_mm(x, e) for x, e in zip(xs, es)]
    n2 = [_mm(n, n) for n in ns]
    ys = [(b - n) - _mm(n, b) for n, b in zip(ns, n2)]
    return [x + _mm(y, x) for x, y in zip(xs, ys)]


def _deltanet_kernel(qf, kf, vf, gf, gtf, qb, kb, vb, gb, gtb, of_ref, ob_ref, state_ref):
    c_len = DN_CHUNK
    nchunk = qf.shape[0] // c_len
    heads = qf.shape[1] // HEAD_DIM
    head0 = pl.program_id(0) * heads

    @pl.when(pl.program_id(1) == 0)
    def _():
        state_ref[...] = jnp.zeros_like(state_ref)

    ri = lax.broadcasted_iota(jnp.int32, (c_len, c_len), 0)
    ci = lax.broadcasted_iota(jnp.int32, (c_len, c_len), 1)
    eye = jnp.where(ri == ci, 1.0, 0.0)
    sub_shift = DN_SUB.bit_length() - 1
    same_sub = (ri >> sub_shift) == (ci >> sub_shift)
    lane = lax.broadcasted_iota(jnp.int32, (c_len, LANES), 1)
    masks = {False: (ci <= ri, ci < ri), True: (ci >= ri, ci > ri)}
    refs = {False: (qf, kf, vf, gf, gtf, of_ref), True: (qb, kb, vb, gb, gtb, ob_ref)}

    chains = [(rev, hd) for rev in (False, True) for hd in range(heads)]
    order = {False: list(range(nchunk)), True: list(range(nchunk - 1, -1, -1))}
    inst = [(rev, hd, c) for rev, hd in chains for c in order[rev]]

    g_rows = {}
    for rev, hd in chains:
        g_col = head0 + hd + (GATE_G_B if rev else GATE_G_F)
        g_rows[rev, hd] = refs[rev][4][pl.ds(g_col, 1), :]

    def load(rev, hd, c):
        q_ref, k_ref, v_ref, g_ref = refs[rev][:4]
        rows = slice(c * c_len, (c + 1) * c_len)
        cols = slice(hd * HEAD_DIM, (hd + 1) * HEAD_DIM)
        gates = g_ref[rows, :]
        beta_col = head0 + hd + (GATE_BETA_B if rev else GATE_BETA_F)
        g_col = head0 + hd + (GATE_G_B if rev else GATE_G_F)
        beta = jnp.sum(jnp.where(lane == beta_col, gates, 0.0), axis=1, keepdims=True)
        gcol = jnp.sum(jnp.where(lane == g_col, gates, 0.0), axis=1, keepdims=True)
        grow = g_rows[rev, hd][:, rows]
        glast = grow[:, 0:1] if rev else grow[:, c_len - 1:c_len]
        return q_ref[rows, cols], k_ref[rows, cols], v_ref[rows, cols], beta, gcol, grow, glast

    data = [load(*i) for i in inst]
    kbeta = [k * beta for (_, k, _, beta, _, _, _) in data]
    prod = [_dot_nt(jnp.concatenate([kb_, q], axis=0).astype(BF16), k.astype(BF16))
            for kb_, (q, k, _, _, _, _, _) in zip(kbeta, data)]
    decay = [jnp.exp(jnp.where(masks[rev][0], gcol - grow, -jnp.inf))
             for (rev, _, _), (_, _, _, _, gcol, grow, _) in zip(inst, data)]
    ms = [jnp.where(masks[rev][1], p[:c_len] * dec, 0.0) for (rev, _, _), p, dec in zip(inst, prod, decay)]
    a_mat = [(p[c_len:] * dec).astype(BF16) for p, dec in zip(prod, decay)]
    ts = _unit_tri_inverse(ms, eye, same_sub)
    egs = [jnp.exp(gcol) for (_, _, _, _, gcol, _, _) in data]
    uw = [_mm(t, jnp.concatenate([v * beta, kb_ * eg], axis=1))
          for t, kb_, eg, (_, _, v, beta, _, _, _) in zip(ts, kbeta, egs, data)]
    wq = [jnp.concatenate([x[:, HEAD_DIM:], q * eg], axis=0).astype(BF16)
          for x, eg, (q, _, _, _, _, _, _) in zip(uw, egs, data)]
    kdec = [(k * jnp.exp(glast - gcol)).astype(BF16) for (_, k, _, _, gcol, _, glast) in data]
    gl = [jnp.exp(glast) for (_, _, _, _, _, _, glast) in data]
    pre = {i: (x[:, :HEAD_DIM], w_, a_, kd, g_) for i, x, w_, a_, kd, g_ in zip(inst, uw, wq, a_mat, kdec, gl)}

    states = [state_ref[n] for n in range(len(chains))]
    for t in range(nchunk):
        cur = [pre[rev, hd, order[rev][t]] for rev, hd in chains]
        ws = [_dot(w_, s_.astype(BF16)) for (_, w_, _, _, _), s_ in zip(cur, states)]
        v_new = [(u - x[:c_len]).astype(BF16) for (u, _, _, _, _), x in zip(cur, ws)]
        outs = [x[c_len:] + _dot(a_, vn) for (_, _, a_, _, _), x, vn in zip(cur, ws, v_new)]
        states = [s_ * g_ + _dot_tn(kd, vn) for (_, _, _, kd, g_), s_, vn in zip(cur, states, v_new)]
        for (rev, hd), o in zip(chains, outs):
            c = order[rev][t]
            refs[rev][5][c * c_len:(c + 1) * c_len, hd * HEAD_DIM:(hd + 1) * HEAD_DIM] = o
    for n, s_ in enumerate(states):
        state_ref[n] = s_


def _deltanet(q, k, v, gates, gates_t, *, rows, heads):
    s = q.shape[0]
    nb = s // rows
    fwd = lambda h, b: (b, h)
    bwd = lambda h, b: (nb - 1 - b, h)
    head = lambda im: pl.BlockSpec((rows, heads * HEAD_DIM), im)
    gate = lambda im: pl.BlockSpec((rows, LANES), lambda h, b: (im(h, b)[0], 0))
    gate_t = lambda im: pl.BlockSpec((GATE_ROWS, rows), lambda h, b: (0, im(h, b)[0]))
    return pl.pallas_call(
        _deltanet_kernel,
        grid=(DN_HEADS // heads, nb),
        in_specs=[head(fwd), head(fwd), head(fwd), gate(fwd), gate_t(fwd),
                  head(bwd), head(bwd), head(bwd), gate(bwd), gate_t(bwd)],
        out_specs=[head(fwd), head(bwd)],
        out_shape=[jax.ShapeDtypeStruct(q.shape, F32), jax.ShapeDtypeStruct(q.shape, F32)],
        scratch_shapes=[pltpu.VMEM((2 * heads, HEAD_DIM, HEAD_DIM), F32)],
        compiler_params=_params("parallel", "arbitrary"),
        name="deltanet",
    )(q, k, v, gates, gates_t, q, k, v, gates, gates_t)


def _t5_bucket(rel):
    nb = NUM_BUCKETS // 2
    max_exact = nb // 2
    n = jnp.abs(rel)
    large = max_exact + (jnp.log(jnp.maximum(n, max_exact).astype(F32) / max_exact)
                         / math.log(MAX_DISTANCE / max_exact) * (nb - max_exact)).astype(jnp.int32)
    large = jnp.minimum(large, nb - 1)
    return jnp.where(rel > 0, nb, 0) + jnp.where(n < max_exact, n, large)


def _swa_kernel(q_ref, kp_ref, kc_ref, kn_ref, vp_ref, vc_ref, vn_ref, bucket_ref, rb_ref, sink_ref,
                o_ref, bias_ref, *, seq):
    kvh = pl.program_id(0)
    n = pl.program_id(1)
    w = WINDOW
    group = SWA_HEADS // SWA_KV_HEADS

    @pl.when(n == 0)
    def _():
        bucket = bucket_ref[...]
        for g in range(group):
            acc = jnp.zeros((w, 3 * w), F32)
            for b in range(NUM_BUCKETS):
                acc = jnp.where(bucket == b, rb_ref[b, kvh * group + g], acc)
            bias_ref[g] = acc

    kcat = jnp.concatenate([kp_ref[...], kc_ref[...], kn_ref[...]], axis=0).astype(BF16)
    vcat = jnp.concatenate([vp_ref[...], vc_ref[...], vn_ref[...]], axis=0).astype(BF16)
    ri = lax.broadcasted_iota(jnp.int32, (w, 3 * w), 0)
    ci = lax.broadcasted_iota(jnp.int32, (w, 3 * w), 1)
    key_pos = n * w + ci - w
    valid = (jnp.abs(ci - w - ri) <= w) & (key_pos >= 0) & (key_pos < seq)
    for g in range(group):
        cols = slice(g * HEAD_DIM, (g + 1) * HEAD_DIM)
        s = _dot_nt(q_ref[:, cols].astype(BF16), kcat) * (HEAD_DIM ** -0.5) + bias_ref[g]
        s = jnp.where(valid, s, -jnp.inf)
        sink = sink_ref[kvh * group + g]
        mx = jnp.maximum(jnp.max(s, axis=1, keepdims=True), sink)
        p = jnp.exp(s - mx)
        den = jnp.sum(p, axis=1, keepdims=True) + jnp.exp(sink - mx)
        o_ref[:, cols] = (_dot(p.astype(BF16), vcat) / den).astype(o_ref.dtype)


def _swa(proj, bucket, rel_bias, sink):
    s = proj.shape[0]
    w = WINDOW
    nb = s // w
    group = SWA_HEADS // SWA_KV_HEADS
    qw = group * HEAD_DIM
    kblk = lambda off, shift: pl.BlockSpec(
        (w, HEAD_DIM), lambda kvh, n: (jnp.clip(n + shift, 0, nb - 1), off // HEAD_DIM + kvh))
    return pl.pallas_call(
        functools.partial(_swa_kernel, seq=s),
        grid=(SWA_KV_HEADS, nb),
        in_specs=[pl.BlockSpec((w, qw), lambda kvh, n: (n, QSW_OFF // qw + kvh)),
                  kblk(KSW_OFF, -1), kblk(KSW_OFF, 0), kblk(KSW_OFF, 1),
                  kblk(VSW_OFF, -1), kblk(VSW_OFF, 0), kblk(VSW_OFF, 1),
                  pl.BlockSpec((w, 3 * w), lambda kvh, n: (0, 0)),
                  pl.BlockSpec(memory_space=pltpu.SMEM),
                  pl.BlockSpec(memory_space=pltpu.SMEM)],
        out_specs=pl.BlockSpec((w, qw), lambda kvh, n: (n, kvh)),
        out_shape=jax.ShapeDtypeStruct((s, SWA_HEADS * HEAD_DIM), BF16),
        scratch_shapes=[pltpu.VMEM((group, w, 3 * w), F32)],
        compiler_params=_params("parallel", "arbitrary"),
        name="swa",
    )(proj, proj, proj, proj, proj, proj, proj, bucket, rel_bias, sink)


def _out_proj_kernel(of_ref, ob_ref, z_ref, sw_ref, g_ref, w_ref, x_ref, o_ref, lhs_ref):
    @pl.when(pl.program_id(1) == 0)
    def _():
        for hd in range(DN_HEADS):
            cols = slice(hd * HEAD_DIM, (hd + 1) * HEAD_DIM)
            z = z_ref[:, cols]
            y = _rms(of_ref[:, cols] + ob_ref[:, cols], g_ref[...]) * (z * jax.nn.sigmoid(z))
            lhs_ref[:, cols] = y.astype(lhs_ref.dtype)
        lhs_ref[:, DN_HEADS * HEAD_DIM:] = sw_ref[...]

    o_ref[...] = x_ref[...] + _dot(lhs_ref[...], w_ref[...])


def _out_proj(o_f, o_b, proj, y_sw, dn_g, w_out, x, *, tm, tn):
    s, d = x.shape
    dn = DN_HEADS * HEAD_DIM
    mix = w_out.shape[0]
    return pl.pallas_call(
        _out_proj_kernel,
        grid=(s // tm, d // tn),
        in_specs=[pl.BlockSpec((tm, dn), lambda i, j: (i, 0)),
                  pl.BlockSpec((tm, dn), lambda i, j: (i, 0)),
                  pl.BlockSpec((tm, dn), lambda i, j: (i, Z_OFF // dn)),
                  pl.BlockSpec((tm, mix - dn), lambda i, j: (i, 0)),
                  pl.BlockSpec((1, HEAD_DIM), lambda i, j: (0, 0)),
                  pl.BlockSpec((mix, tn), lambda i, j: (0, j)),
                  pl.BlockSpec((tm, tn), lambda i, j: (i, j))],
        out_specs=pl.BlockSpec((tm, tn), lambda i, j: (i, j)),
        out_shape=jax.ShapeDtypeStruct((s, d), F32),
        scratch_shapes=[pltpu.VMEM((tm, mix), BF16)],
        compiler_params=_params("parallel", "arbitrary"),
        name="out_proj",
    )(o_f, o_b, proj, y_sw, dn_g.reshape(1, HEAD_DIM), w_out, x)


def _mem_attn_kernel(h_ref, gx_ref, wq_ref, k_ref, v_ref, wo_ref, gf_ref, h2_ref, f_ref):
    h = h_ref[...]
    q = _dot(_rms(h, gx_ref[...]).astype(BF16), wq_ref[...])
    heads = []
    for hd in range(MEM_HEADS):
        cols = slice(hd * HEAD_DIM, (hd + 1) * HEAD_DIM)
        s = _dot_nt(q[:, cols].astype(BF16), k_ref[:, cols]) * (HEAD_DIM ** -0.5)
        p = jnp.exp(s - jnp.max(s, axis=1, keepdims=True))
        den = jnp.sum(p, axis=1, keepdims=True)
        heads.append((_dot(p.astype(BF16), v_ref[:, cols]) / den).astype(BF16))
    h2 = h + _dot(jnp.concatenate(heads, axis=1), wo_ref[...])
    h2_ref[...] = h2
    f_ref[...] = _rms(h2, gf_ref[...]).astype(f_ref.dtype)


def _mem_attn(h, gx, wq, k, v, wo, gf, *, tm):
    s, d = h.shape
    full = lambda a: pl.BlockSpec(a.shape, lambda i: (0, 0))
    gx, gf = gx.reshape(1, d), gf.reshape(1, d)
    return pl.pallas_call(
        _mem_attn_kernel,
        grid=(s // tm,),
        in_specs=[pl.BlockSpec((tm, d), lambda i: (i, 0)), full(gx), full(wq), full(k), full(v), full(wo),
                  full(gf)],
        out_specs=[pl.BlockSpec((tm, d), lambda i: (i, 0)), pl.BlockSpec((tm, d), lambda i: (i, 0))],
        out_shape=[jax.ShapeDtypeStruct((s, d), F32), jax.ShapeDtypeStruct((s, d), BF16)],
        compiler_params=_params("parallel"),
        name="mem_attn",
    )(h, gx, wq, k, v, wo, gf)


def _glu_kernel(f_ref, wg_ref, wu_ref, o_ref):
    f = f_ref[...]
    a = _dot(f, wg_ref[...])
    o_ref[...] = (a * jax.nn.sigmoid(a) * _dot(f, wu_ref[...])).astype(o_ref.dtype)


def _ffn_glu(f, wg, wu, *, tm, tn):
    s, d = f.shape
    dff = wg.shape[1]
    return pl.pallas_call(
        _glu_kernel,
        grid=(s // tm, dff // tn),
        in_specs=[pl.BlockSpec((tm, d), lambda i, j: (i, 0)),
                  pl.BlockSpec((d, tn), lambda i, j: (0, j)),
                  pl.BlockSpec((d, tn), lambda i, j: (0, j))],
        out_specs=pl.BlockSpec((tm, tn), lambda i, j: (i, j)),
        out_shape=jax.ShapeDtypeStruct((s, dff), BF16),
        compiler_params=_params("parallel", "parallel"),
        name="ffn_glu",
    )(f, wg, wu)


def _down_kernel(a_ref, w_ref, h_ref, g_ref, o_ref, acc_ref):
    k = pl.program_id(1)

    @pl.when(k == 0)
    def _():
        acc_ref[...] = h_ref[...]

    acc_ref[...] += _dot(a_ref[...], w_ref[...])

    @pl.when(k == pl.num_programs(1) - 1)
    def _():
        o_ref[...] = _rms(acc_ref[...], g_ref[...])


def _ffn_down(act, wd, h, g, *, tm, tk):
    s, dff = act.shape
    d = wd.shape[1]
    return pl.pallas_call(
        _down_kernel,
        grid=(s // tm, dff // tk),
        in_specs=[pl.BlockSpec((tm, tk), lambda i, k: (i, k)),
                  pl.BlockSpec((tk, d), lambda i, k: (k, 0)),
                  pl.BlockSpec((tm, d), lambda i, k: (i, 0)),
                  pl.BlockSpec((1, d), lambda i, k: (0, 0))],
        out_specs=pl.BlockSpec((tm, d), lambda i, k: (i, 0)),
        out_shape=jax.ShapeDtypeStruct((s, d), F32),
        scratch_shapes=[pltpu.VMEM((tm, d), F32)],
        compiler_params=_params("parallel", "arbitrary"),
        name="ffn_down",
    )(act, wd, h, g.reshape(1, d))


def _gate_row(fwd, bwd):
    row = jnp.zeros((1, LANES), F32)
    row = row.at[0, GATE_G_F:GATE_G_F + DN_HEADS].set(fwd.astype(F32))
    return row.at[0, GATE_G_B:GATE_G_B + DN_HEADS].set(bwd.astype(F32))


def _pick(n, *cands):
    for c in cands:
        if n % c == 0:
            return c
    return n


def kernel(x, mem, norm_mix_g, w_in, conv_w, a_log_f, a_log_b, dt_bias_f, dt_bias_b, dn_norm_g, attn_sink, rel_bias, w_out, norm_x_g, norm_mem_g, w_q_mem, w_kv_mem, w_o_mem, norm_ffn_g, w_gate, w_up, w_down, norm_final_g):
    batch, s, d = x.shape
    assert batch == 1 and mem.shape[0] == 1
    depth = w_in.shape[0]
    w = WINDOW
    rel = (jnp.arange(3 * w)[None, :] - w) - jnp.arange(w)[:, None]
    bucket = _t5_bucket(rel).astype(jnp.int32)
    gate_lo = DN_QKV + DN_HEADS * HEAD_DIM
    mem_dim = MEM_HEADS * HEAD_DIM

    h = x.reshape(s, d)
    mem2 = mem.reshape(mem.shape[1], d)
    tm_big = _pick(s, 1024, 512, 256, 128)
    tm_mid = _pick(s, 512, 256, 128)
    out = None
    for l in range(depth):
        wl = w_in[l]
        w_r = jnp.concatenate([wl[:, :gate_lo], wl[:, gate_lo + GATE_ROWS:], wl[:, gate_lo:gate_lo + GATE_ROWS],
                               jnp.zeros((d, LANES - GATE_ROWS), wl.dtype)], axis=1).astype(BF16)
        proj = _rms_matmul(h, norm_mix_g[l], w_r, tm=tm_big, tn=_pick(PROJ_WIDTH, 1152, 640, 128), name="in_proj")

        q, k, v, gates, gates_t = _dn_prep(proj, conv_w[l], _gate_row(a_log_f[l], a_log_b[l]),
                                           _gate_row(dt_bias_f[l], dt_bias_b[l]), rows=_pick(s, 256, 128))
        o_f, o_b = _deltanet(q, k, v, gates, gates_t, rows=_pick(s, 256, 128), heads=4)
        y_sw = _swa(proj, bucket, rel_bias.astype(F32), attn_sink[l].astype(F32))
        h = _out_proj(o_f, o_b, proj, y_sw, dn_norm_g[l], w_out[l].astype(BF16), h, tm=tm_mid, tn=1024)

        kv = _rms_matmul(mem2, norm_mem_g[l], w_kv_mem[l].astype(BF16), tm=mem2.shape[0], tn=2 * mem_dim,
                         name="mem_kv").astype(BF16)
        h, f = _mem_attn(h, norm_x_g[l], w_q_mem[l].astype(BF16), kv[:, :mem_dim], kv[:, mem_dim:],
                         w_o_mem[l].astype(BF16), norm_ffn_g[l], tm=tm_mid)

        act = _ffn_glu(f, w_gate[l].astype(BF16), w_up[l].astype(BF16), tm=tm_big, tn=512)
        last = l == depth - 1
        g_last = norm_final_g if last else jnp.ones((d,), F32)
        out = _ffn_down(act, w_down[l].astype(BF16), h, g_last, tm=tm_mid, tk=1408)
        h = out
    return out.reshape(batch, s, d)
```

```python
import functools
import math

import jax
import jax.numpy as jnp
from jax import lax
from jax.experimental import pallas as pl
from jax.experimental.pallas import tpu as pltpu

F32 = jnp.float32
BF16 = jnp.bfloat16

RMS_EPS = 1e-6
L2_EPS = 1e-6
HEAD_DIM = 128
DN_HEADS = 8
DN_CHUNK = 64
DN_SUB = 16
CONV_WIDTH = 5
CONV_HALO = 8
SWA_HEADS = 8
SWA_KV_HEADS = 2
WINDOW = 128
NUM_BUCKETS = 32
MAX_DISTANCE = 128
MEM_HEADS = 4
LANES = 128

DN_QKV = 3 * DN_HEADS * HEAD_DIM
Z_OFF = DN_QKV
QSW_OFF = Z_OFF + DN_HEADS * HEAD_DIM
KSW_OFF = QSW_OFF + SWA_HEADS * HEAD_DIM
VSW_OFF = KSW_OFF + SWA_KV_HEADS * HEAD_DIM
GATE_OFF = VSW_OFF + SWA_KV_HEADS * HEAD_DIM
PROJ_WIDTH = GATE_OFF + LANES
GATE_BETA_F, GATE_BETA_B, GATE_G_F, GATE_G_B = 0, DN_HEADS, 2 * DN_HEADS, 3 * DN_HEADS
GATE_ROWS = 4 * DN_HEADS

VMEM_LIMIT_V7X = 56 * 1024 * 1024


def _params(*sem):
    return pltpu.CompilerParams(dimension_semantics=sem, vmem_limit_bytes=VMEM_LIMIT_V7X)


def _dot(a, b):
    return jnp.dot(a, b, preferred_element_type=F32)


def _dot_nt(a, b):
    return lax.dot_general(a, b, (((1,), (1,)), ((), ())), preferred_element_type=F32)


def _dot_tn(a, b):
    return lax.dot_general(a, b, (((0,), (0,)), ((), ())), preferred_element_type=F32)


def _rms(x, g):
    return x * lax.rsqrt(jnp.mean(x * x, axis=-1, keepdims=True) + RMS_EPS) * g


def _rms_matmul_kernel(x_ref, g_ref, w_ref, o_ref, n_ref):
    @pl.when(pl.program_id(1) == 0)
    def _():
        n_ref[...] = _rms(x_ref[...], g_ref[...]).astype(n_ref.dtype)

    o_ref[...] = _dot(n_ref[...], w_ref[...]).astype(o_ref.dtype)


def _rms_matmul(x, g, w, *, tm, tn, name):
    m, k = x.shape
    n = w.shape[1]
    return pl.pallas_call(
        _rms_matmul_kernel,
        grid=(m // tm, n // tn),
        in_specs=[pl.BlockSpec((tm, k), lambda i, j: (i, 0)),
                  pl.BlockSpec((1, k), lambda i, j: (0, 0)),
                  pl.BlockSpec((k, tn), lambda i, j: (0, j))],
        out_specs=pl.BlockSpec((tm, tn), lambda i, j: (i, j)),
        out_shape=jax.ShapeDtypeStruct((m, n), F32),
        scratch_shapes=[pltpu.VMEM((tm, k), BF16)],
        compiler_params=_params("parallel", "arbitrary"),
        name=name,
    )(x, g.reshape(1, k), w)


def _split3(x):
    hi = x.astype(BF16)
    r = x - hi.astype(F32)
    mid = r.astype(BF16)
    lo = (r - mid.astype(F32)).astype(BF16)
    return hi, mid, lo


def _prep_kernel(main_ref, prev_ref, next_ref, gate_ref, cw_ref, alog_ref, dt_ref,
                 q_ref, k_ref, v_ref, go_ref, gt_ref, ext_ref):
    i = pl.program_id(0)
    rows = main_ref.shape[0]
    halo = CONV_HALO
    pad = (CONV_WIDTH - 1) // 2

    ext_ref[0:halo, :] = jnp.where(i > 0, prev_ref[...], 0.0)
    ext_ref[halo:halo + rows, :] = main_ref[...]
    ext_ref[halo + rows:, :] = jnp.where(i < pl.num_programs(0) - 1, next_ref[...], 0.0)

    for s in range(3 * DN_HEADS):
        cols = slice(s * HEAD_DIM, (s + 1) * HEAD_DIM)
        acc = cw_ref[0:1, cols] * ext_ref[halo - pad:halo - pad + rows, cols]
        for j in range(1, CONV_WIDTH):
            acc = acc + cw_ref[j:j + 1, cols] * ext_ref[halo - pad + j:halo - pad + j + rows, cols]
        y = acc * jax.nn.sigmoid(acc)
        if s < 2 * DN_HEADS:
            y = y * lax.rsqrt(jnp.sum(y * y, axis=-1, keepdims=True) + L2_EPS)
        if s < DN_HEADS:
            q_ref[:, cols] = y * (HEAD_DIM ** -0.5)
        elif s < 2 * DN_HEADS:
            k_ref[:, slice((s - DN_HEADS) * HEAD_DIM, (s - DN_HEADS + 1) * HEAD_DIM)] = y
        else:
            v_ref[:, slice((s - 2 * DN_HEADS) * HEAD_DIM, (s - 2 * DN_HEADS + 1) * HEAD_DIM)] = y

    t = gate_ref[...]
    beta = jax.nn.sigmoid(t)
    a = t + dt_ref[...]
    softplus = jnp.maximum(a, 0.0) + jnp.log1p(jnp.exp(-jnp.abs(a)))
    g = -jnp.exp(alog_ref[...]) * softplus

    ri = lax.broadcasted_iota(jnp.int32, (rows, rows), 0)
    ci = lax.broadcasted_iota(jnp.int32, (rows, rows), 1)
    shift = DN_CHUNK.bit_length() - 1
    same_chunk = (ri >> shift) == (ci >> shift)
    lower = jnp.where(same_chunk & (ci <= ri), 1.0, 0.0).astype(BF16)
    upper = jnp.where(same_chunk & (ci >= ri), 1.0, 0.0).astype(BF16)
    parts = _split3(g)
    gc_f = _dot(lower, parts[0]) + _dot(lower, parts[1]) + _dot(lower, parts[2])
    gc_b = _dot(upper, parts[0]) + _dot(upper, parts[1]) + _dot(upper, parts[2])

    col = lax.broadcasted_iota(jnp.int32, t.shape, 1)
    out = jnp.where(col < GATE_G_F, beta, jnp.where(col < GATE_G_B, gc_f, gc_b))
    go_ref[...] = out
    gt_ref[...] = out.T[0:GATE_ROWS, :]


def _dn_prep(proj, conv_w, alog_row, dt_row, *, rows):
    s = proj.shape[0]
    nblk = s // rows
    hb = rows // CONV_HALO
    last_halo = s // CONV_HALO - 1
    head_cols = DN_HEADS * HEAD_DIM
    return pl.pallas_call(
        _prep_kernel,
        grid=(nblk,),
        in_specs=[pl.BlockSpec((rows, DN_QKV), lambda i: (i, 0)),
                  pl.BlockSpec((CONV_HALO, DN_QKV), lambda i: (jnp.maximum(i * hb - 1, 0), 0)),
                  pl.BlockSpec((CONV_HALO, DN_QKV), lambda i: (jnp.minimum((i + 1) * hb, last_halo), 0)),
                  pl.BlockSpec((rows, LANES), lambda i: (i, GATE_OFF // LANES)),
                  pl.BlockSpec((CONV_WIDTH, DN_QKV), lambda i: (0, 0)),
                  pl.BlockSpec((1, LANES), lambda i: (0, 0)),
                  pl.BlockSpec((1, LANES), lambda i: (0, 0))],
        out_specs=[pl.BlockSpec((rows, head_cols), lambda i: (i, 0)),
                   pl.BlockSpec((rows, head_cols), lambda i: (i, 0)),
                   pl.BlockSpec((rows, head_cols), lambda i: (i, 0)),
                   pl.BlockSpec((rows, LANES), lambda i: (i, 0)),
                   pl.BlockSpec((GATE_ROWS, rows), lambda i: (0, i))],
        out_shape=[jax.ShapeDtypeStruct((s, head_cols), F32),
                   jax.ShapeDtypeStruct((s, head_cols), F32),
                   jax.ShapeDtypeStruct((s, head_cols), F32),
                   jax.ShapeDtypeStruct((s, LANES), F32),
                   jax.ShapeDtypeStruct((GATE_ROWS, s), F32)],
        scratch_shapes=[pltpu.VMEM((rows + 2 * CONV_HALO, DN_QKV), F32)],
        compiler_params=_params("parallel"),
        name="dn_prep",
    )(proj, proj, proj, proj, conv_w, alog_row, dt_row)


def _mm(a, b):
    return _dot(a.astype(BF16), b.astype(BF16))


def _unit_tri_inverse(ms, eye, same_sub):
    mds = [jnp.where(same_sub, m, 0.0) for m in ms]
    es = [m - md for m, md in zip(ms, mds)]
    m2 = [_mm(md, md) for md in mds]
    xs = [eye - md for md in mds]
    m4 = [_mm(a, a) for a in m2]
    xs = [x + _mm(x, a) for x, a in zip(xs, m2)]
    m8 = [_mm(a, a) for a in m4]
    xs = [x + _mm(x, a) for x, a in zip(xs, m4)]
    xs = [x + _mm(x, a) for x, a in zip(xs, m8)]
    ns = [_mm(x, e) for x, e in zip(xs, es)]
    n2 = [_mm(n, n) for n in ns]
    ys = [(b - n) - _mm(n, b) for n, b in zip(ns, n2)]
    return [x + _mm(y, x) for x, y in zip(xs, ys)]


def _deltanet_kernel(qf, kf, vf, gf, gtf, qb, kb, vb, gb, gtb, of_ref, ob_ref, state_ref):
    c_len = DN_CHUNK
    nchunk = qf.shape[0] // c_len
    heads = qf.shape[1] // HEAD_DIM
    head0 = pl.program_id(0) * heads

    @pl.when(pl.program_id(1) == 0)
    def _():
        state_ref[...] = jnp.zeros_like(state_ref)

    ri = lax.broadcasted_iota(jnp.int32, (c_len, c_len), 0)
    ci = lax.broadcasted_iota(jnp.int32, (c_len, c_len), 1)
    eye = jnp.where(ri == ci, 1.0, 0.0)
    sub_shift = DN_SUB.bit_length() - 1
    same_sub = (ri >> sub_shift) == (ci >> sub_shift)
    lane = lax.broadcasted_iota(jnp.int32, (c_len, LANES), 1)
    masks = {False: (ci <= ri, ci < ri), True: (ci >= ri, ci > ri)}
    refs = {False: (qf, kf, vf, gf, gtf, of_ref), True: (qb, kb, vb, gb, gtb, ob_ref)}

    chains = [(rev, hd) for rev in (False, True) for hd in range(heads)]
    order = {False: list(range(nchunk)), True: list(range(nchunk - 1, -1, -1))}
    inst = [(rev, hd, c) for rev, hd in chains for c in order[rev]]

    g_rows = {}
    for rev, hd in chains:
        g_col = head0 + hd + (GATE_G_B if rev else GATE_G_F)
        g_rows[rev, hd] = refs[rev][4][pl.ds(g_col, 1), :]

    def load(rev, hd, c):
        q_ref, k_ref, v_ref, g_ref = refs[rev][:4]
        rows = slice(c * c_len, (c + 1) * c_len)
        cols = slice(hd * HEAD_DIM, (hd + 1) * HEAD_DIM)
        gates = g_ref[rows, :]
        beta_col = head0 + hd + (GATE_BETA_B if rev else GATE_BETA_F)
        g_col = head0 + hd + (GATE_G_B if rev else GATE_G_F)
        beta = jnp.sum(jnp.where(lane == beta_col, gates, 0.0), axis=1, keepdims=True)
        gcol = jnp.sum(jnp.where(lane == g_col, gates, 0.0), axis=1, keepdims=True)
        grow = g_rows[rev, hd][:, rows]
        glast = grow[:, 0:1] if rev else grow[:, c_len - 1:c_len]
        return q_ref[rows, cols], k_ref[rows, cols], v_ref[rows, cols], beta, gcol, grow, glast

    data = [load(*i) for i in inst]
    kbeta = [k * beta for (_, k, _, beta, _, _, _) in data]
    prod = [_dot_nt(jnp.concatenate([kb_, q], axis=0).astype(BF16), k.astype(BF16))
            for kb_, (q, k, _, _, _, _, _) in zip(kbeta, data)]
    decay = [jnp.exp(jnp.where(masks[rev][0], gcol - grow, -jnp.inf))
             for (rev, _, _), (_, _, _, _, gcol, grow, _) in zip(inst, data)]
    ms = [jnp.where(masks[rev][1], p[:c_len] * dec, 0.0) for (rev, _, _), p, dec in zip(inst, prod, decay)]
    a_mat = [(p[c_len:] * dec).astype(BF16) for p, dec in zip(prod, decay)]
    ts = _unit_tri_inverse(ms, eye, same_sub)
    egs = [jnp.exp(gcol) for (_, _, _, _, gcol, _, _) in data]
    uw = [_mm(t, jnp.concatenate([v * beta, kb_ * eg], axis=1))
          for t, kb_, eg, (_, _, v, beta, _, _, _) in zip(ts, kbeta, egs, data)]
    wq = [jnp.concatenate([x[:, HEAD_DIM:], q * eg], axis=0).astype(BF16)
          for x, eg, (q, _, _, _, _, _, _) in zip(uw, egs, data)]
    kdec = [(k * jnp.exp(glast - gcol)).astype(BF16) for (_, k, _, _, gcol, _, glast) in data]
    gl = [jnp.exp(glast) for (_, _, _, _, _, _, glast) in data]
    pre = {i: (x[:, :HEAD_DIM], w_, a_, kd, g_) for i, x, w_, a_, kd, g_ in zip(inst, uw, wq, a_mat, kdec, gl)}

    states = [state_ref[n] for n in range(len(chains))]
    for t in range(nchunk):
        cur = [pre[rev, hd, order[rev][t]] for rev, hd in chains]
        ws = [_dot(w_, s_.astype(BF16)) for (_, w_, _, _, _), s_ in zip(cur, states)]
        v_new = [(u - x[:c_len]).astype(BF16) for (u, _, _, _, _), x in zip(cur, ws)]
        outs = [x[c_len:] + _dot(a_, vn) for (_, _, a_, _, _), x, vn in zip(cur, ws, v_new)]
        states = [s_ * g_ + _dot_tn(kd, vn) for (_, _, _, kd, g_), s_, vn in zip(cur, states, v_new)]
        for (rev, hd), o in zip(chains, outs):
            c = order[rev][t]
            refs[rev][5][c * c_len:(c + 1) * c_len, hd * HEAD_DIM:(hd + 1) * HEAD_DIM] = o
    for n, s_ in enumerate(states):
        state_ref[n] = s_


def _deltanet(q, k, v, gates, gates_t, *, rows, heads):
    s = q.shape[0]
    nb = s // rows
    fwd = lambda h, b: (b, h)
    bwd = lambda h, b: (nb - 1 - b, h)
    head = lambda im: pl.BlockSpec((rows, heads * HEAD_DIM), im)
    gate = lambda im: pl.BlockSpec((rows, LANES), lambda h, b: (im(h, b)[0], 0))
    gate_t = lambda im: pl.BlockSpec((GATE_ROWS, rows), lambda h, b: (0, im(h, b)[0]))
    return pl.pallas_call(
        _deltanet_kernel,
        grid=(DN_HEADS // heads, nb),
        in_specs=[head(fwd), head(fwd), head(fwd), gate(fwd), gate_t(fwd),
                  head(bwd), head(bwd), head(bwd), gate(bwd), gate_t(bwd)],
        out_specs=[head(fwd), head(bwd)],
        out_shape=[jax.ShapeDtypeStruct(q.shape, F32), jax.ShapeDtypeStruct(q.shape, F32)],
        scratch_shapes=[pltpu.VMEM((2 * heads, HEAD_DIM, HEAD_DIM), F32)],
        compiler_params=_params("parallel", "arbitrary"),
        name="deltanet",
    )(q, k, v, gates, gates_t, q, k, v, gates, gates_t)


def _t5_bucket(rel):
    nb = NUM_BUCKETS // 2
    max_exact = nb // 2
    n = jnp.abs(rel)
    large = max_exact + (jnp.log(jnp.maximum(n, max_exact).astype(F32) / max_exact)
                         / math.log(MAX_DISTANCE / max_exact) * (nb - max_exact)).astype(jnp.int32)
    large = jnp.minimum(large, nb - 1)
    return jnp.where(rel > 0, nb, 0) + jnp.where(n < max_exact, n, large)


def _swa_kernel(q_ref, kp_ref, kc_ref, kn_ref, vp_ref, vc_ref, vn_ref, bucket_ref, rb_ref, sink_ref,
                o_ref, bias_ref, *, seq):
    n = pl.program_id(0)
    w = WINDOW
    qblk = q_ref.shape[0] // w
    group = SWA_HEADS // SWA_KV_HEADS
    ri = lax.broadcasted_iota(jnp.int32, (w, 3 * w), 0)
    ci = lax.broadcasted_iota(jnp.int32, (w, 3 * w), 1)

    @pl.when(n == 0)
    def _():
        bucket = bucket_ref[...]
        in_band = jnp.abs(ci - w - ri) <= w
        for hd in range(SWA_HEADS):
            acc = jnp.zeros((w, 3 * w), F32)
            for b in range(NUM_BUCKETS):
                acc = jnp.where(bucket == b, rb_ref[b, hd], acc)
            bias_ref[hd] = jnp.where(in_band, acc, -jnp.inf)

    key_pos = (n * qblk - 1) * w + ci
    kext, vext = [], []
    for kvh in range(SWA_KV_HEADS):
        kcols = slice(kvh * HEAD_DIM, (kvh + 1) * HEAD_DIM)
        kext.append(jnp.concatenate([kp_ref[:, kcols], kc_ref[:, kcols], kn_ref[:, kcols]], axis=0).astype(BF16))
        vext.append(jnp.concatenate([vp_ref[:, kcols], vc_ref[:, kcols], vn_ref[:, kcols]], axis=0).astype(BF16))
    heads = range(SWA_HEADS)
    sinks = [sink_ref[hd] for hd in heads]
    for t in range(qblk):
        rows = slice(t * w, (t + 1) * w)
        band = slice(t * w, (t + 3) * w)
        s = [_dot_nt(q_ref[rows, hd * HEAD_DIM:(hd + 1) * HEAD_DIM].astype(BF16), kext[hd // group][band])
             * (HEAD_DIM ** -0.5) + bias_ref[hd] for hd in heads]
        if t == 0:
            s = [jnp.where(key_pos >= 0, x, -jnp.inf) for x in s]
        if t == qblk - 1:
            s = [jnp.where(key_pos + t * w < seq, x, -jnp.inf) for x in s]
        mx = [jnp.maximum(jnp.max(x, axis=1, keepdims=True), sk) for x, sk in zip(s, sinks)]
        p = [jnp.exp(x - m) for x, m in zip(s, mx)]
        den = [jnp.sum(x, axis=1, keepdims=True) + jnp.exp(sk - m) for x, sk, m in zip(p, sinks, mx)]
        o = [_dot(x.astype(BF16), vext[hd // group][band]) / dn for hd, x, dn in zip(heads, p, den)]
        for hd, x in zip(heads, o):
            o_ref[rows, hd * HEAD_DIM:(hd + 1) * HEAD_DIM] = x.astype(o_ref.dtype)


def _swa(proj, bucket, rel_bias, sink, *, qblk):
    s = proj.shape[0]
    w = WINDOW
    rows = qblk * w
    nb = s // w
    qw = SWA_HEADS * HEAD_DIM
    kvw = SWA_KV_HEADS * HEAD_DIM
    main = lambda off: pl.BlockSpec((rows, kvw), lambda n: (n, off // kvw))
    halo = lambda off, lo: pl.BlockSpec(
        (w, kvw), lambda n: (jnp.clip(n * qblk - 1 if lo else (n + 1) * qblk, 0, nb - 1), off // kvw))
    return pl.pallas_call(
        functools.partial(_swa_kernel, seq=s),
        grid=(s // rows,),
        in_specs=[pl.BlockSpec((rows, qw), lambda n: (n, QSW_OFF // qw)),
                  halo(KSW_OFF, True), main(KSW_OFF), halo(KSW_OFF, False),
                  halo(VSW_OFF, True), main(VSW_OFF), halo(VSW_OFF, False),
                  pl.BlockSpec((w, 3 * w), lambda n: (0, 0)),
                  pl.BlockSpec(memory_space=pltpu.SMEM),
                  pl.BlockSpec(memory_space=pltpu.SMEM)],
        out_specs=pl.BlockSpec((rows, qw), lambda n: (n, 0)),
        out_shape=jax.ShapeDtypeStruct((s, qw), BF16),
        scratch_shapes=[pltpu.VMEM((SWA_HEADS, w, 3 * w), F32)],
        compiler_params=_params("arbitrary"),
        name="swa",
    )(proj, proj, proj, proj, proj, proj, proj, bucket, rel_bias, sink)


ROW_BATCH = 128


def _out_proj_kernel(of_ref, ob_ref, z_ref, sw_ref, g_ref, w_ref, x_ref, o_ref):
    for r in range(0, o_ref.shape[0], ROW_BATCH):
        rows = slice(r, r + ROW_BATCH)
        parts = []
        for hd in range(DN_HEADS):
            cols = slice(hd * HEAD_DIM, (hd + 1) * HEAD_DIM)
            z = z_ref[rows, cols]
            y = _rms(of_ref[rows, cols] + ob_ref[rows, cols], g_ref[...]) * (z * jax.nn.sigmoid(z))
            parts.append(y.astype(BF16))
        parts.append(sw_ref[rows, :])
        o_ref[rows, :] = x_ref[rows, :] + _dot(jnp.concatenate(parts, axis=1), w_ref[...])


def _out_proj(o_f, o_b, proj, y_sw, dn_g, w_out, x, *, tm):
    s, d = x.shape
    dn = DN_HEADS * HEAD_DIM
    mix = w_out.shape[0]
    return pl.pallas_call(
        _out_proj_kernel,
        grid=(s // tm,),
        in_specs=[pl.BlockSpec((tm, dn), lambda i: (i, 0)),
                  pl.BlockSpec((tm, dn), lambda i: (i, 0)),
                  pl.BlockSpec((tm, dn), lambda i: (i, Z_OFF // dn)),
                  pl.BlockSpec((tm, mix - dn), lambda i: (i, 0)),
                  pl.BlockSpec((1, HEAD_DIM), lambda i: (0, 0)),
                  pl.BlockSpec((mix, d), lambda i: (0, 0)),
                  pl.BlockSpec((tm, d), lambda i: (i, 0))],
        out_specs=pl.BlockSpec((tm, d), lambda i: (i, 0)),
        out_shape=jax.ShapeDtypeStruct((s, d), F32),
        compiler_params=_params("parallel"),
        name="out_proj",
    )(o_f, o_b, proj, y_sw, dn_g.reshape(1, HEAD_DIM), w_out, x)


def _mem_attn_kernel(h_ref, gx_ref, wq_ref, k_ref, v_ref, wo_ref, gf_ref, h2_ref, f_ref):
    h = h_ref[...]
    q = _dot(_rms(h, gx_ref[...]).astype(BF16), wq_ref[...])
    heads = []
    for hd in range(MEM_HEADS):
        cols = slice(hd * HEAD_DIM, (hd + 1) * HEAD_DIM)
        s = _dot_nt(q[:, cols].astype(BF16), k_ref[:, cols]) * (HEAD_DIM ** -0.5)
        p = jnp.exp(s - jnp.max(s, axis=1, keepdims=True))
        den = jnp.sum(p, axis=1, keepdims=True)
        heads.append((_dot(p.astype(BF16), v_ref[:, cols]) / den).astype(BF16))
    h2 = h + _dot(jnp.concatenate(heads, axis=1), wo_ref[...])
    h2_ref[...] = h2
    f_ref[...] = _rms(h2, gf_ref[...]).astype(f_ref.dtype)


def _mem_attn(h, gx, wq, k, v, wo, gf, *, tm):
    s, d = h.shape
    full = lambda a: pl.BlockSpec(a.shape, lambda i: (0, 0))
    gx, gf = gx.reshape(1, d), gf.reshape(1, d)
    return pl.pallas_call(
        _mem_attn_kernel,
        grid=(s // tm,),
        in_specs=[pl.BlockSpec((tm, d), lambda i: (i, 0)), full(gx), full(wq), full(k), full(v), full(wo),
                  full(gf)],
        out_specs=[pl.BlockSpec((tm, d), lambda i: (i, 0)), pl.BlockSpec((tm, d), lambda i: (i, 0))],
        out_shape=[jax.ShapeDtypeStruct((s, d), F32), jax.ShapeDtypeStruct((s, d), BF16)],
        compiler_params=_params("parallel"),
        name="mem_attn",
    )(h, gx, wq, k, v, wo, gf)


def _glu_kernel(f_ref, wg_ref, wu_ref, o_ref):
    f = f_ref[...]
    a = _dot(f, wg_ref[...])
    o_ref[...] = (a * jax.nn.sigmoid(a) * _dot(f, wu_ref[...])).astype(o_ref.dtype)


def _ffn_glu(f, wg, wu, *, tm, tn):
    s, d = f.shape
    dff = wg.shape[1]
    return pl.pallas_call(
        _glu_kernel,
        grid=(s // tm, dff // tn),
        in_specs=[pl.BlockSpec((tm, d), lambda i, j: (i, 0)),
                  pl.BlockSpec((d, tn), lambda i, j: (0, j)),
                  pl.BlockSpec((d, tn), lambda i, j: (0, j))],
        out_specs=pl.BlockSpec((tm, tn), lambda i, j: (i, j)),
        out_shape=jax.ShapeDtypeStruct((s, dff), BF16),
        compiler_params=_params("parallel", "parallel"),
        name="ffn_glu",
    )(f, wg, wu)


def _down_kernel(a_ref, w_ref, h_ref, g_ref, o_ref, acc_ref):
    k = pl.program_id(1)

    @pl.when(k == 0)
    def _():
        acc_ref[...] = h_ref[...]

    acc_ref[...] += _dot(a_ref[...], w_ref[...])

    @pl.when(k == pl.num_programs(1) - 1)
    def _():
        o_ref[...] = _rms(acc_ref[...], g_ref[...])


def _ffn_down(act, wd, h, g, *, tm, tk):
    s, dff = act.shape
    d = wd.shape[1]
    return pl.pallas_call(
        _down_kernel,
        grid=(s // tm, dff // tk),
        in_specs=[pl.BlockSpec((tm, tk), lambda i, k: (i, k)),
                  pl.BlockSpec((tk, d), lambda i, k: (k, 0)),
                  pl.BlockSpec((tm, d), lambda i, k: (i, 0)),
                  pl.BlockSpec((1, d), lambda i, k: (0, 0))],
        out_specs=pl.BlockSpec((tm, d), lambda i, k: (i, 0)),
        out_shape=jax.ShapeDtypeStruct((s, d), F32),
        scratch_shapes=[pltpu.VMEM((tm, d), F32)],
        compiler_params=_params("parallel", "arbitrary"),
        name="ffn_down",
    )(act, wd, h, g.reshape(1, d))


def _gate_row(fwd, bwd):
    row = jnp.zeros((1, LANES), F32)
    row = row.at[0, GATE_G_F:GATE_G_F + DN_HEADS].set(fwd.astype(F32))
    return row.at[0, GATE_G_B:GATE_G_B + DN_HEADS].set(bwd.astype(F32))


def _pick(n, *cands):
    for c in cands:
        if n % c == 0:
            return c
    return n


def kernel(x, mem, norm_mix_g, w_in, conv_w, a_log_f, a_log_b, dt_bias_f, dt_bias_b, dn_norm_g, attn_sink, rel_bias, w_out, norm_x_g, norm_mem_g, w_q_mem, w_kv_mem, w_o_mem, norm_ffn_g, w_gate, w_up, w_down, norm_final_g):
    batch, s, d = x.shape
    assert batch == 1 and mem.shape[0] == 1 and w_in.shape[0] == 1, "single sequence, single layer"
    w = WINDOW
    rel = (jnp.arange(3 * w)[None, :] - w) - jnp.arange(w)[:, None]
    bucket = _t5_bucket(rel).astype(jnp.int32)
    gate_lo = DN_QKV + DN_HEADS * HEAD_DIM
    mem_dim = MEM_HEADS * HEAD_DIM

    h = x.reshape(s, d)
    mem2 = mem.reshape(mem.shape[1], d)
    tm_big = _pick(s, 1024, 512, 256, 128)
    tm_mid = _pick(s, 512, 256, 128)
    wl = w_in[0]
    w_r = jnp.concatenate([wl[:, :gate_lo], wl[:, gate_lo + GATE_ROWS:], wl[:, gate_lo:gate_lo + GATE_ROWS],
                           jnp.zeros((d, LANES - GATE_ROWS), wl.dtype)], axis=1).astype(BF16)
    proj = _rms_matmul(h, norm_mix_g[0], w_r, tm=tm_big, tn=_pick(PROJ_WIDTH, 1152, 640, 128), name="in_proj")

    q, k, v, gates, gates_t = _dn_prep(proj, conv_w[0], _gate_row(a_log_f[0], a_log_b[0]),
                                       _gate_row(dt_bias_f[0], dt_bias_b[0]), rows=_pick(s, 256, 128))
    o_f, o_b = _deltanet(q, k, v, gates, gates_t, rows=_pick(s, 256, 128), heads=4)
    y_sw = _swa(proj, bucket, rel_bias.astype(F32), attn_sink[0].astype(F32), qblk=_pick(s // w, 4, 2, 1))
    h = _out_proj(o_f, o_b, proj, y_sw, dn_norm_g[0], w_out[0].astype(BF16), h, tm=tm_mid)

    kv = _rms_matmul(mem2, norm_mem_g[0], w_kv_mem[0].astype(BF16), tm=mem2.shape[0], tn=2 * mem_dim,
                     name="mem_kv").astype(BF16)
    h, f = _mem_attn(h, norm_x_g[0], w_q_mem[0].astype(BF16), kv[:, :mem_dim], kv[:, mem_dim:],
                     w_o_mem[0].astype(BF16), norm_ffn_g[0], tm=tm_mid)

    act = _ffn_glu(f, w_gate[0].astype(BF16), w_up[0].astype(BF16), tm=tm_big, tn=512)
    out = _ffn_down(act, w_down[0].astype(BF16), h, norm_final_g, tm=tm_mid, tk=2816)
    return out.reshape(batch, s, d)
```

```python
import functools
import math

import jax
import jax.numpy as jnp
from jax import lax
from jax.experimental import pallas as pl
from jax.experimental.pallas import tpu as pltpu

F32 = jnp.float32
BF16 = jnp.bfloat16

RMS_EPS = 1e-6
L2_EPS = 1e-6
HEAD_DIM = 128
DN_HEADS = 8
DN_CHUNK = 64
DN_SUB = 16
CONV_WIDTH = 5
CONV_HALO = 8
SWA_HEADS = 8
SWA_KV_HEADS = 2
WINDOW = 128
NUM_BUCKETS = 32
MAX_DISTANCE = 128
MEM_HEADS = 4
LANES = 128

DN_QKV = 3 * DN_HEADS * HEAD_DIM
Z_OFF = DN_QKV
QSW_OFF = Z_OFF + DN_HEADS * HEAD_DIM
KSW_OFF = QSW_OFF + SWA_HEADS * HEAD_DIM
VSW_OFF = KSW_OFF + SWA_KV_HEADS * HEAD_DIM
GATE_OFF = VSW_OFF + SWA_KV_HEADS * HEAD_DIM
PROJ_WIDTH = GATE_OFF + LANES
GATE_BETA_F, GATE_BETA_B, GATE_G_F, GATE_G_B = 0, DN_HEADS, 2 * DN_HEADS, 3 * DN_HEADS
GATE_ROWS = 4 * DN_HEADS

VMEM_LIMIT_V7X = 56 * 1024 * 1024


def _params(*sem):
    return pltpu.CompilerParams(dimension_semantics=sem, vmem_limit_bytes=VMEM_LIMIT_V7X)


def _dot(a, b):
    return jnp.dot(a, b, preferred_element_type=F32)


def _dot_nt(a, b):
    return lax.dot_general(a, b, (((1,), (1,)), ((), ())), preferred_element_type=F32)


def _dot_tn(a, b):
    return lax.dot_general(a, b, (((0,), (0,)), ((), ())), preferred_element_type=F32)


def _rms(x, g):
    return x * lax.rsqrt(jnp.mean(x * x, axis=-1, keepdims=True) + RMS_EPS) * g


def _rms_matmul_kernel(x_ref, g_ref, w_ref, o_ref, n_ref):
    @pl.when(pl.program_id(1) == 0)
    def _():
        n_ref[...] = _rms(x_ref[...], g_ref[...]).astype(n_ref.dtype)

    o_ref[...] = _dot(n_ref[...], w_ref[...]).astype(o_ref.dtype)


def _rms_matmul(x, g, w, *, tm, tn, name):
    m, k = x.shape
    n = w.shape[1]
    return pl.pallas_call(
        _rms_matmul_kernel,
        grid=(m // tm, n // tn),
        in_specs=[pl.BlockSpec((tm, k), lambda i, j: (i, 0)),
                  pl.BlockSpec((1, k), lambda i, j: (0, 0)),
                  pl.BlockSpec((k, tn), lambda i, j: (0, j))],
        out_specs=pl.BlockSpec((tm, tn), lambda i, j: (i, j)),
        out_shape=jax.ShapeDtypeStruct((m, n), F32),
        scratch_shapes=[pltpu.VMEM((tm, k), BF16)],
        compiler_params=_params("parallel", "arbitrary"),
        name=name,
    )(x, g.reshape(1, k), w)


def _split3(x):
    hi = x.astype(BF16)
    r = x - hi.astype(F32)
    mid = r.astype(BF16)
    lo = (r - mid.astype(F32)).astype(BF16)
    return hi, mid, lo


def _prep_kernel(main_ref, prev_ref, next_ref, gate_ref, cw_ref, alog_ref, dt_ref,
                 q_ref, k_ref, v_ref, go_ref, gt_ref, ext_ref):
    i = pl.program_id(0)
    rows = main_ref.shape[0]
    halo = CONV_HALO
    pad = (CONV_WIDTH - 1) // 2

    ext_ref[0:halo, :] = jnp.where(i > 0, prev_ref[...], 0.0)
    ext_ref[halo:halo + rows, :] = main_ref[...]
    ext_ref[halo + rows:, :] = jnp.where(i < pl.num_programs(0) - 1, next_ref[...], 0.0)

    for s in range(3 * DN_HEADS):
        cols = slice(s * HEAD_DIM, (s + 1) * HEAD_DIM)
        acc = cw_ref[0:1, cols] * ext_ref[halo - pad:halo - pad + rows, cols]
        for j in range(1, CONV_WIDTH):
            acc = acc + cw_ref[j:j + 1, cols] * ext_ref[halo - pad + j:halo - pad + j + rows, cols]
        y = acc * jax.nn.sigmoid(acc)
        if s < 2 * DN_HEADS:
            y = y * lax.rsqrt(jnp.sum(y * y, axis=-1, keepdims=True) + L2_EPS)
        if s < DN_HEADS:
            q_ref[:, cols] = y * (HEAD_DIM ** -0.5)
        elif s < 2 * DN_HEADS:
            k_ref[:, slice((s - DN_HEADS) * HEAD_DIM, (s - DN_HEADS + 1) * HEAD_DIM)] = y
        else:
            v_ref[:, slice((s - 2 * DN_HEADS) * HEAD_DIM, (s - 2 * DN_HEADS + 1) * HEAD_DIM)] = y

    t = gate_ref[...]
    beta = jax.nn.sigmoid(t)
    a = t + dt_ref[...]
    softplus = jnp.maximum(a, 0.0) + jnp.log1p(jnp.exp(-jnp.abs(a)))
    g = -jnp.exp(alog_ref[...]) * softplus

    ri = lax.broadcasted_iota(jnp.int32, (rows, rows), 0)
    ci = lax.broadcasted_iota(jnp.int32, (rows, rows), 1)
    shift = DN_CHUNK.bit_length() - 1
    same_chunk = (ri >> shift) == (ci >> shift)
    lower = jnp.where(same_chunk & (ci <= ri), 1.0, 0.0).astype(BF16)
    upper = jnp.where(same_chunk & (ci >= ri), 1.0, 0.0).astype(BF16)
    parts = _split3(g)
    gc_f = _dot(lower, parts[0]) + _dot(lower, parts[1]) + _dot(lower, parts[2])
    gc_b = _dot(upper, parts[0]) + _dot(upper, parts[1]) + _dot(upper, parts[2])

    col = lax.broadcasted_iota(jnp.int32, t.shape, 1)
    out = jnp.where(col < GATE_G_F, beta, jnp.where(col < GATE_G_B, gc_f, gc_b))
    go_ref[...] = out
    gt_ref[...] = out.T[0:GATE_ROWS, :]


def _dn_prep(proj, conv_w, alog_row, dt_row, *, rows):
    s = proj.shape[0]
    nblk = s // rows
    hb = rows // CONV_HALO
    last_halo = s // CONV_HALO - 1
    head_cols = DN_HEADS * HEAD_DIM
    return pl.pallas_call(
        _prep_kernel,
        grid=(nblk,),
        in_specs=[pl.BlockSpec((rows, DN_QKV), lambda i: (i, 0)),
                  pl.BlockSpec((CONV_HALO, DN_QKV), lambda i: (jnp.maximum(i * hb - 1, 0), 0)),
                  pl.BlockSpec((CONV_HALO, DN_QKV), lambda i: (jnp.minimum((i + 1) * hb, last_halo), 0)),
                  pl.BlockSpec((rows, LANES), lambda i: (i, GATE_OFF // LANES)),
                  pl.BlockSpec((CONV_WIDTH, DN_QKV), lambda i: (0, 0)),
                  pl.BlockSpec((1, LANES), lambda i: (0, 0)),
                  pl.BlockSpec((1, LANES), lambda i: (0, 0))],
        out_specs=[pl.BlockSpec((rows, head_cols), lambda i: (i, 0)),
                   pl.BlockSpec((rows, head_cols), lambda i: (i, 0)),
                   pl.BlockSpec((rows, head_cols), lambda i: (i, 0)),
                   pl.BlockSpec((rows, LANES), lambda i: (i, 0)),
                   pl.BlockSpec((GATE_ROWS, rows), lambda i: (0, i))],
        out_shape=[jax.ShapeDtypeStruct((s, head_cols), F32),
                   jax.ShapeDtypeStruct((s, head_cols), F32),
                   jax.ShapeDtypeStruct((s, head_cols), F32),
                   jax.ShapeDtypeStruct((s, LANES), F32),
                   jax.ShapeDtypeStruct((GATE_ROWS, s), F32)],
        scratch_shapes=[pltpu.VMEM((rows + 2 * CONV_HALO, DN_QKV), F32)],
        compiler_params=_params("parallel"),
        name="dn_prep",
    )(proj, proj, proj, proj, conv_w, alog_row, dt_row)


def _unit_tri_inverse(ms, eye, same_sub, _mm):
    mds = [jnp.where(same_sub, m, 0.0) for m in ms]
    es = [m - md for m, md in zip(ms, mds)]
    m2 = [_mm(md, md) for md in mds]
    xs = [eye - md for md in mds]
    m4 = [_mm(a, a) for a in m2]
    xs = [x + _mm(x, a) for x, a in zip(xs, m2)]
    m8 = [_mm(a, a) for a in m4]
    xs = [x + _mm(x, a) for x, a in zip(xs, m4)]
    xs = [x + _mm(x, a) for x, a in zip(xs, m8)]
    ns = [_mm(x, e) for x, e in zip(xs, es)]
    n2 = [_mm(n, n) for n in ns]
    ys = [(b - n) - _mm(n, b) for n, b in zip(ns, n2)]
    return [x + _mm(y, x) for x, y in zip(xs, ys)]


def _deltanet_kernel(qf, kf, vf, gf, gtf, qb, kb, vb, gb, gtb, of_ref, ob_ref, state_ref):
    c_len = DN_CHUNK
    nchunk = qf.shape[0] // c_len
    heads = qf.shape[1] // HEAD_DIM
    head0 = pl.program_id(0) * heads

    @pl.when(pl.program_id(1) == 0)
    def _():
        state_ref[...] = jnp.zeros_like(state_ref)

    p_len = 2 * c_len
    ri = lax.broadcasted_iota(jnp.int32, (c_len, p_len), 0)
    cl = lax.broadcasted_iota(jnp.int32, (c_len, p_len), 1)
    ci = cl & (c_len - 1)
    first = cl < c_len
    eye = jnp.where(ri == ci, 1.0, 0.0)
    sub_shift = DN_SUB.bit_length() - 1
    same_sub = (ri >> sub_shift) == (ci >> sub_shift)
    lane = lax.broadcasted_iota(jnp.int32, (p_len, LANES), 1)
    upper_rows = lax.broadcasted_iota(jnp.int32, (p_len, 1), 0) >= c_len
    first_tall = lax.broadcasted_iota(jnp.int32, (c_len + HEAD_DIM, p_len), 1) < c_len
    zeros = jnp.zeros((c_len, HEAD_DIM), BF16)
    masks = {False: (ci <= ri, ci < ri), True: (ci >= ri, ci > ri)}
    refs = {False: (qf, kf, vf, gf, gtf, of_ref), True: (qb, kb, vb, gb, gtb, ob_ref)}

    def pair_mm(x, p):
        blockdiag = jnp.concatenate([jnp.where(first, p, 0.0), jnp.where(first, 0.0, p)], axis=0)
        return _dot(x.astype(BF16), blockdiag.astype(BF16))

    chains = [(rev, hd) for rev in (False, True) for hd in range(heads)]
    inst = [(rev, hd, p) for rev, hd in chains for p in range(nchunk // 2)]

    g_rows = {}
    for rev, hd in chains:
        g_col = head0 + hd + (GATE_G_B if rev else GATE_G_F)
        g_rows[rev, hd] = refs[rev][4][pl.ds(g_col, 1), :]

    def load(rev, hd, p):
        q_ref, k_ref, v_ref, g_ref = refs[rev][:4]
        rows = slice(p * p_len, (p + 1) * p_len)
        cols = slice(hd * HEAD_DIM, (hd + 1) * HEAD_DIM)
        gates = g_ref[rows, :]
        beta_col = head0 + hd + (GATE_BETA_B if rev else GATE_BETA_F)
        g_col = head0 + hd + (GATE_G_B if rev else GATE_G_F)
        beta = jnp.sum(jnp.where(lane == beta_col, gates, 0.0), axis=1, keepdims=True)
        gcol = jnp.sum(jnp.where(lane == g_col, gates, 0.0), axis=1, keepdims=True)
        grow = g_rows[rev, hd][:, rows]
        if rev:
            glast = (grow[:, 0:1], grow[:, c_len:c_len + 1])
        else:
            glast = (grow[:, c_len - 1:c_len], grow[:, p_len - 1:p_len])
        return q_ref[rows, cols], k_ref[rows, cols], v_ref[rows, cols], beta, gcol, grow, glast

    lo, hi = slice(0, c_len), slice(c_len, p_len)

    data = [load(*i) for i in inst]
    kbeta = [k * beta for (_, k, _, beta, _, _, _) in data]
    prod = []
    for kb_, (q, k, _, _, _, _, _) in zip(kbeta, data):
        lhs = jnp.concatenate([jnp.concatenate([kb_[lo], kb_[hi]], axis=1),
                               jnp.concatenate([q[lo], q[hi]], axis=1)], axis=0).astype(BF16)
        kb16 = k.astype(BF16)
        rhs = jnp.concatenate([jnp.concatenate([kb16[lo], zeros], axis=1),
                               jnp.concatenate([zeros, kb16[hi]], axis=1)], axis=0)
        prod.append(_dot_nt(lhs, rhs))
    decay = [jnp.exp(jnp.where(masks[rev][0], jnp.where(first, gcol[lo], gcol[hi]) - grow, -jnp.inf))
             for (rev, _, _), (_, _, _, _, gcol, grow, _) in zip(inst, data)]
    ms = [jnp.where(masks[rev][1], p[lo] * dec, 0.0) for (rev, _, _), p, dec in zip(inst, prod, decay)]
    a_mat = [p[hi] * dec for p, dec in zip(prod, decay)]
    ts = _unit_tri_inverse(ms, eye, same_sub, pair_mm)
    egs = [jnp.exp(gcol) for (_, _, _, _, gcol, _, _) in data]
    uw = []
    for t, kb_, eg, (_, _, v, beta, _, _, _) in zip(ts, kbeta, egs, data):
        vb_ = (v * beta).astype(BF16)
        kg = (kb_ * eg).astype(BF16)
        rhs = jnp.concatenate([jnp.concatenate([vb_[lo], kg[lo], zeros, zeros], axis=1),
                               jnp.concatenate([zeros, zeros, vb_[hi], kg[hi]], axis=1)], axis=0)
        uw.append(_dot(t.astype(BF16), rhs))
    qdec = [q * eg for eg, (q, _, _, _, _, _, _) in zip(egs, data)]
    kdec_t = [(k * jnp.exp(jnp.where(upper_rows, glast[1], glast[0]) - gcol)).T
              for (_, k, _, _, gcol, _, glast) in data]
    tall = [jnp.concatenate([a_, kt], axis=0) for a_, kt in zip(a_mat, kdec_t)]
    pre = {}
    for (rev, hd, p), x, qd, tl, (_, _, _, _, _, _, glast) in zip(inst, uw, qdec, tall, data):
        for half, rs in enumerate((lo, hi)):
            u = x[:, 2 * half * HEAD_DIM:(2 * half + 1) * HEAD_DIM]
            w_ = x[:, (2 * half + 1) * HEAD_DIM:(2 * half + 2) * HEAD_DIM]
            wq = jnp.concatenate([w_, qd[rs]], axis=0).astype(BF16)
            keep = first_tall if half == 0 else ~first_tall
            pre[rev, hd, 2 * p + half] = (u, wq, jnp.where(keep, tl, 0.0).astype(BF16), jnp.exp(glast[half]))

    order = {False: list(range(nchunk)), True: list(range(nchunk - 1, -1, -1))}
    states = [state_ref[n] for n in range(len(chains))]
    for t in range(nchunk):
        cur = [pre[rev, hd, order[rev][t]] for rev, hd in chains]
        ws = [_dot(wq, s_.astype(BF16)) for (_, wq, _, _), s_ in zip(cur, states)]
        v_new = [(u - x[lo]).astype(BF16) for (u, _, _, _), x in zip(cur, ws)]
        upd = [_dot(tl, jnp.concatenate([vn, vn], axis=0)) for (_, _, tl, _), vn in zip(cur, v_new)]
        states = [s_ * g_ + x[c_len:] for (_, _, _, g_), s_, x in zip(cur, states, upd)]
        for (rev, hd), x, y in zip(chains, ws, upd):
            c = order[rev][t]
            refs[rev][5][c * c_len:(c + 1) * c_len, hd * HEAD_DIM:(hd + 1) * HEAD_DIM] = x[hi] + y[lo]
    for n, s_ in enumerate(states):
        state_ref[n] = s_


def _deltanet(q, k, v, gates, gates_t, *, rows, heads):
    s = q.shape[0]
    nb = s // rows
    fwd = lambda h, b: (b, h)
    bwd = lambda h, b: (nb - 1 - b, h)
    head = lambda im: pl.BlockSpec((rows, heads * HEAD_DIM), im)
    gate = lambda im: pl.BlockSpec((rows, LANES), lambda h, b: (im(h, b)[0], 0))
    gate_t = lambda im: pl.BlockSpec((GATE_ROWS, rows), lambda h, b: (0, im(h, b)[0]))
    return pl.pallas_call(
        _deltanet_kernel,
        grid=(DN_HEADS // heads, nb),
        in_specs=[head(fwd), head(fwd), head(fwd), gate(fwd), gate_t(fwd),
                  head(bwd), head(bwd), head(bwd), gate(bwd), gate_t(bwd)],
        out_specs=[head(fwd), head(bwd)],
        out_shape=[jax.ShapeDtypeStruct(q.shape, F32), jax.ShapeDtypeStruct(q.shape, F32)],
        scratch_shapes=[pltpu.VMEM((2 * heads, HEAD_DIM, HEAD_DIM), F32)],
        compiler_params=_params("parallel", "arbitrary"),
        name="deltanet",
    )(q, k, v, gates, gates_t, q, k, v, gates, gates_t)


def _t5_bucket(rel):
    nb = NUM_BUCKETS // 2
    max_exact = nb // 2
    n = jnp.abs(rel)
    large = max_exact + (jnp.log(jnp.maximum(n, max_exact).astype(F32) / max_exact)
                         / math.log(MAX_DISTANCE / max_exact) * (nb - max_exact)).astype(jnp.int32)
    large = jnp.minimum(large, nb - 1)
    return jnp.where(rel > 0, nb, 0) + jnp.where(n < max_exact, n, large)


def _swa_kernel(q_ref, kp_ref, kc_ref, kn_ref, vp_ref, vc_ref, vn_ref, bucket_ref, rb_ref, sink_ref,
                o_ref, bias_ref, *, seq):
    n = pl.program_id(0)
    w = WINDOW
    qblk = q_ref.shape[0] // w
    group = SWA_HEADS // SWA_KV_HEADS
    ri = lax.broadcasted_iota(jnp.int32, (w, 3 * w), 0)
    ci = lax.broadcasted_iota(jnp.int32, (w, 3 * w), 1)

    @pl.when(n == 0)
    def _():
        bucket = bucket_ref[...]
        in_band = jnp.abs(ci - w - ri) <= w
        for hd in range(SWA_HEADS):
            acc = jnp.zeros((w, 3 * w), F32)
            for b in range(NUM_BUCKETS):
                acc = jnp.where(bucket == b, rb_ref[b, hd], acc)
            bias_ref[hd] = jnp.where(in_band, acc, -jnp.inf)

    key_pos = (n * qblk - 1) * w + ci
    kext, vext = [], []
    for kvh in range(SWA_KV_HEADS):
        kcols = slice(kvh * HEAD_DIM, (kvh + 1) * HEAD_DIM)
        kext.append(jnp.concatenate([kp_ref[:, kcols], kc_ref[:, kcols], kn_ref[:, kcols]], axis=0).astype(BF16))
        vext.append(jnp.concatenate([vp_ref[:, kcols], vc_ref[:, kcols], vn_ref[:, kcols]], axis=0).astype(BF16))
    heads = range(SWA_HEADS)
    sinks = [sink_ref[hd] for hd in heads]
    for t in range(qblk):
        rows = slice(t * w, (t + 1) * w)
        band = slice(t * w, (t + 3) * w)
        s = [_dot_nt(q_ref[rows, hd * HEAD_DIM:(hd + 1) * HEAD_DIM].astype(BF16), kext[hd // group][band])
             * (HEAD_DIM ** -0.5) + bias_ref[hd] for hd in heads]
        if t == 0:
            s = [jnp.where(key_pos >= 0, x, -jnp.inf) for x in s]
        if t == qblk - 1:
            s = [jnp.where(key_pos + t * w < seq, x, -jnp.inf) for x in s]
        mx = [jnp.maximum(jnp.max(x, axis=1, keepdims=True), sk) for x, sk in zip(s, sinks)]
        p = [jnp.exp(x - m) for x, m in zip(s, mx)]
        den = [jnp.sum(x, axis=1, keepdims=True) + jnp.exp(sk - m) for x, sk, m in zip(p, sinks, mx)]
        o = [_dot(x.astype(BF16), vext[hd // group][band]) / dn for hd, x, dn in zip(heads, p, den)]
        for hd, x in zip(heads, o):
            o_ref[rows, hd * HEAD_DIM:(hd + 1) * HEAD_DIM] = x.astype(o_ref.dtype)


def _swa(proj, bucket, rel_bias, sink, *, qblk):
    s = proj.shape[0]
    w = WINDOW
    rows = qblk * w
    nb = s // w
    qw = SWA_HEADS * HEAD_DIM
    kvw = SWA_KV_HEADS * HEAD_DIM
    main = lambda off: pl.BlockSpec((rows, kvw), lambda n: (n, off // kvw))
    halo = lambda off, lo: pl.BlockSpec(
        (w, kvw), lambda n: (jnp.clip(n * qblk - 1 if lo else (n + 1) * qblk, 0, nb - 1), off // kvw))
    return pl.pallas_call(
        functools.partial(_swa_kernel, seq=s),
        grid=(s // rows,),
        in_specs=[pl.BlockSpec((rows, qw), lambda n: (n, QSW_OFF // qw)),
                  halo(KSW_OFF, True), main(KSW_OFF), halo(KSW_OFF, False),
                  halo(VSW_OFF, True), main(VSW_OFF), halo(VSW_OFF, False),
                  pl.BlockSpec((w, 3 * w), lambda n: (0, 0)),
                  pl.BlockSpec(memory_space=pltpu.SMEM),
                  pl.BlockSpec(memory_space=pltpu.SMEM)],
        out_specs=pl.BlockSpec((rows, qw), lambda n: (n, 0)),
        out_shape=jax.ShapeDtypeStruct((s, qw), BF16),
        scratch_shapes=[pltpu.VMEM((SWA_HEADS, w, 3 * w), F32)],
        compiler_params=_params("arbitrary"),
        name="swa",
    )(proj, proj, proj, proj, proj, proj, proj, bucket, rel_bias, sink)


ROW_BATCH = 128


def _out_proj_kernel(of_ref, ob_ref, z_ref, sw_ref, g_ref, w_ref, x_ref, o_ref):
    for r in range(0, o_ref.shape[0], ROW_BATCH):
        rows = slice(r, r + ROW_BATCH)
        parts = []
        for hd in range(DN_HEADS):
            cols = slice(hd * HEAD_DIM, (hd + 1) * HEAD_DIM)
            z = z_ref[rows, cols]
            y = _rms(of_ref[rows, cols] + ob_ref[rows, cols], g_ref[...]) * (z * jax.nn.sigmoid(z))
            parts.append(y.astype(BF16))
        parts.append(sw_ref[rows, :])
        o_ref[rows, :] = x_ref[rows, :] + _dot(jnp.concatenate(parts, axis=1), w_ref[...])


def _out_proj(o_f, o_b, proj, y_sw, dn_g, w_out, x, *, tm):
    s, d = x.shape
    dn = DN_HEADS * HEAD_DIM
    mix = w_out.shape[0]
    return pl.pallas_call(
        _out_proj_kernel,
        grid=(s // tm,),
        in_specs=[pl.BlockSpec((tm, dn), lambda i: (i, 0)),
                  pl.BlockSpec((tm, dn), lambda i: (i, 0)),
                  pl.BlockSpec((tm, dn), lambda i: (i, Z_OFF // dn)),
                  pl.BlockSpec((tm, mix - dn), lambda i: (i, 0)),
                  pl.BlockSpec((1, HEAD_DIM), lambda i: (0, 0)),
                  pl.BlockSpec((mix, d), lambda i: (0, 0)),
                  pl.BlockSpec((tm, d), lambda i: (i, 0))],
        out_specs=pl.BlockSpec((tm, d), lambda i: (i, 0)),
        out_shape=jax.ShapeDtypeStruct((s, d), F32),
        compiler_params=_params("parallel"),
        name="out_proj",
    )(o_f, o_b, proj, y_sw, dn_g.reshape(1, HEAD_DIM), w_out, x)


def _mem_attn_kernel(h_ref, gx_ref, wq_ref, k_ref, v_ref, wo_ref, gf_ref, h2_ref, f_ref):
    h = h_ref[...]
    q = _dot(_rms(h, gx_ref[...]).astype(BF16), wq_ref[...])
    heads = []
    for hd in range(MEM_HEADS):
        cols = slice(hd * HEAD_DIM, (hd + 1) * HEAD_DIM)
        s = _dot_nt(q[:, cols].astype(BF16), k_ref[:, cols]) * (HEAD_DIM ** -0.5)
        p = jnp.exp(s - jnp.max(s, axis=1, keepdims=True))
        den = jnp.sum(p, axis=1, keepdims=True)
        heads.append((_dot(p.astype(BF16), v_ref[:, cols]) / den).astype(BF16))
    h2 = h + _dot(jnp.concatenate(heads, axis=1), wo_ref[...])
    h2_ref[...] = h2
    f_ref[...] = _rms(h2, gf_ref[...]).astype(f_ref.dtype)


def _mem_attn(h, gx, wq, k, v, wo, gf, *, tm):
    s, d = h.shape
    full = lambda a: pl.BlockSpec(a.shape, lambda i: (0, 0))
    gx, gf = gx.reshape(1, d), gf.reshape(1, d)
    return pl.pallas_call(
        _mem_attn_kernel,
        grid=(s // tm,),
        in_specs=[pl.BlockSpec((tm, d), lambda i: (i, 0)), full(gx), full(wq), full(k), full(v), full(wo),
                  full(gf)],
        out_specs=[pl.BlockSpec((tm, d), lambda i: (i, 0)), pl.BlockSpec((tm, d), lambda i: (i, 0))],
        out_shape=[jax.ShapeDtypeStruct((s, d), F32), jax.ShapeDtypeStruct((s, d), BF16)],
        compiler_params=_params("parallel"),
        name="mem_attn",
    )(h, gx, wq, k, v, wo, gf)


def _glu_kernel(f_ref, wg_ref, wu_ref, o_ref):
    f = f_ref[...]
    a = _dot(f, wg_ref[...].astype(BF16))
    o_ref[...] = (a * jax.nn.sigmoid(a) * _dot(f, wu_ref[...].astype(BF16))).astype(o_ref.dtype)


def _ffn_glu(f, wg, wu, *, tm, tn):
    s, d = f.shape
    dff = wg.shape[1]
    return pl.pallas_call(
        _glu_kernel,
        grid=(s // tm, dff // tn),
        in_specs=[pl.BlockSpec((tm, d), lambda i, j: (i, 0)),
                  pl.BlockSpec((d, tn), lambda i, j: (0, j)),
                  pl.BlockSpec((d, tn), lambda i, j: (0, j))],
        out_specs=pl.BlockSpec((tm, tn), lambda i, j: (i, j)),
        out_shape=jax.ShapeDtypeStruct((s, dff), BF16),
        compiler_params=_params("parallel", "parallel"),
        name="ffn_glu",
    )(f, wg, wu)


def _down_kernel(a_ref, w_ref, h_ref, g_ref, o_ref, acc_ref):
    k = pl.program_id(1)

    @pl.when(k == 0)
    def _():
        acc_ref[...] = h_ref[...]

    acc_ref[...] += _dot(a_ref[...], w_ref[...])

    @pl.when(k == pl.num_programs(1) - 1)
    def _():
        o_ref[...] = _rms(acc_ref[...], g_ref[...])


def _ffn_down(act, wd, h, g, *, tm, tk):
    s, dff = act.shape
    d = wd.shape[1]
    return pl.pallas_call(
        _down_kernel,
        grid=(s // tm, dff // tk),
        in_specs=[pl.BlockSpec((tm, tk), lambda i, k: (i, k)),
                  pl.BlockSpec((tk, d), lambda i, k: (k, 0)),
                  pl.BlockSpec((tm, d), lambda i, k: (i, 0)),
                  pl.BlockSpec((1, d), lambda i, k: (0, 0))],
        out_specs=pl.BlockSpec((tm, d), lambda i, k: (i, 0)),
        out_shape=jax.ShapeDtypeStruct((s, d), F32),
        scratch_shapes=[pltpu.VMEM((tm, d), F32)],
        compiler_params=_params("parallel", "arbitrary"),
        name="ffn_down",
    )(act, wd, h, g.reshape(1, d))


def _gate_row(fwd, bwd):
    row = jnp.zeros((1, LANES), F32)
    row = row.at[0, GATE_G_F:GATE_G_F + DN_HEADS].set(fwd.astype(F32))
    return row.at[0, GATE_G_B:GATE_G_B + DN_HEADS].set(bwd.astype(F32))


def _pick(n, *cands):
    for c in cands:
        if n % c == 0:
            return c
    return n


def kernel(x, mem, norm_mix_g, w_in, conv_w, a_log_f, a_log_b, dt_bias_f, dt_bias_b, dn_norm_g, attn_sink, rel_bias, w_out, norm_x_g, norm_mem_g, w_q_mem, w_kv_mem, w_o_mem, norm_ffn_g, w_gate, w_up, w_down, norm_final_g):
    batch, s, d = x.shape
    assert batch == 1 and mem.shape[0] == 1 and w_in.shape[0] == 1, "single sequence, single layer"
    w = WINDOW
    rel = (jnp.arange(3 * w)[None, :] - w) - jnp.arange(w)[:, None]
    bucket = _t5_bucket(rel).astype(jnp.int32)
    gate_lo = DN_QKV + DN_HEADS * HEAD_DIM
    mem_dim = MEM_HEADS * HEAD_DIM

    h = x.reshape(s, d)
    mem2 = mem.reshape(mem.shape[1], d)
    tm_big = _pick(s, 1024, 512, 256, 128)
    tm_mid = _pick(s, 512, 256, 128)
    wl = w_in[0]
    wl = wl.astype(BF16)
    w_r = jnp.concatenate([wl[:, :gate_lo], wl[:, gate_lo + GATE_ROWS:], wl[:, gate_lo:gate_lo + GATE_ROWS],
                           jnp.zeros((d, LANES - GATE_ROWS), BF16)], axis=1)
    proj = _rms_matmul(h, norm_mix_g[0], w_r, tm=tm_big, tn=_pick(PROJ_WIDTH, 1152, 640, 128), name="in_proj")

    q, k, v, gates, gates_t = _dn_prep(proj, conv_w[0], _gate_row(a_log_f[0], a_log_b[0]),
                                       _gate_row(dt_bias_f[0], dt_bias_b[0]), rows=_pick(s, 256, 128))
    o_f, o_b = _deltanet(q, k, v, gates, gates_t, rows=_pick(s, 256, 128), heads=8)
    y_sw = _swa(proj, bucket, rel_bias.astype(F32), attn_sink[0].astype(F32), qblk=_pick(s // w, 4, 2, 1))
    h = _out_proj(o_f, o_b, proj, y_sw, dn_norm_g[0], w_out[0].astype(BF16), h, tm=tm_mid)

    kv = _rms_matmul(mem2, norm_mem_g[0], w_kv_mem[0].astype(BF16), tm=mem2.shape[0], tn=2 * mem_dim,
                     name="mem_kv").astype(BF16)
    h, f = _mem_attn(h, norm_x_g[0], w_q_mem[0].astype(BF16), kv[:, :mem_dim], kv[:, mem_dim:],
                     w_o_mem[0].astype(BF16), norm_ffn_g[0], tm=tm_mid)

    act = _ffn_glu(f, w_gate[0], w_up[0], tm=tm_big, tn=512)
    out = _ffn_down(act, w_down[0].astype(BF16), h, norm_final_g, tm=tm_mid, tk=2816)
    return out.reshape(batch, s, d)
```

```python
import functools
import math

import jax
import jax.numpy as jnp
from jax import lax
from jax.experimental import pallas as pl
from jax.experimental.pallas import tpu as pltpu

F32 = jnp.float32
BF16 = jnp.bfloat16

RMS_EPS = 1e-6
L2_EPS = 1e-6
HEAD_DIM = 128
DN_HEADS = 8
DN_CHUNK = 64
DN_SUB = 16
CONV_WIDTH = 5
CONV_HALO = 8
SWA_HEADS = 8
SWA_KV_HEADS = 2
WINDOW = 128
NUM_BUCKETS = 32
MAX_DISTANCE = 128
MEM_HEADS = 4
LANES = 128

DN_QKV = 3 * DN_HEADS * HEAD_DIM
Z_OFF = DN_QKV
QSW_OFF = Z_OFF + DN_HEADS * HEAD_DIM
KSW_OFF = QSW_OFF + SWA_HEADS * HEAD_DIM
VSW_OFF = KSW_OFF + SWA_KV_HEADS * HEAD_DIM
GATE_OFF = VSW_OFF + SWA_KV_HEADS * HEAD_DIM
PROJ_WIDTH = GATE_OFF + LANES
GATE_BETA_F, GATE_BETA_B, GATE_G_F, GATE_G_B = 0, DN_HEADS, 2 * DN_HEADS, 3 * DN_HEADS
GATE_ROWS = 4 * DN_HEADS

VMEM_LIMIT_V7X = 56 * 1024 * 1024


def _params(*sem):
    return pltpu.CompilerParams(dimension_semantics=sem, vmem_limit_bytes=VMEM_LIMIT_V7X)


def _dot(a, b):
    return jnp.dot(a, b, preferred_element_type=F32)


def _dot_nt(a, b):
    return lax.dot_general(a, b, (((1,), (1,)), ((), ())), preferred_element_type=F32)


def _dot_tn(a, b):
    return lax.dot_general(a, b, (((0,), (0,)), ((), ())), preferred_element_type=F32)


def _rms(x, g):
    return x * lax.rsqrt(jnp.mean(x * x, axis=-1, keepdims=True) + RMS_EPS) * g


def _w_in_kernel(w_ref, o_ref):
    gate_lo = QSW_OFF
    o_ref[:, :gate_lo] = w_ref[:, :gate_lo].astype(o_ref.dtype)
    o_ref[:, gate_lo:GATE_OFF] = w_ref[:, gate_lo + GATE_ROWS:].astype(o_ref.dtype)
    gates = w_ref[:, gate_lo:gate_lo + GATE_ROWS]
    pad = jnp.zeros((gates.shape[0], LANES - GATE_ROWS), gates.dtype)
    o_ref[:, GATE_OFF:] = jnp.concatenate([gates, pad], axis=1).astype(o_ref.dtype)


def _w_in_layout(w, *, rows):
    k, n = w.shape
    assert n == PROJ_WIDTH - (LANES - GATE_ROWS)
    return pl.pallas_call(
        _w_in_kernel,
        grid=(k // rows,),
        in_specs=[pl.BlockSpec((rows, n), lambda i: (i, 0))],
        out_specs=pl.BlockSpec((rows, PROJ_WIDTH), lambda i: (i, 0)),
        out_shape=jax.ShapeDtypeStruct((k, PROJ_WIDTH), BF16),
        compiler_params=_params("parallel"),
        name="w_in_layout",
    )(w)


def _rms_matmul_kernel(x_ref, g_ref, w_ref, o_ref, n_ref):
    @pl.when(pl.program_id(1) == 0)
    def _():
        n_ref[...] = _rms(x_ref[...], g_ref[...]).astype(n_ref.dtype)

    o_ref[...] = _dot(n_ref[...], w_ref[...]).astype(o_ref.dtype)


def _rms_matmul(x, g, w, *, tm, tn, name):
    m, k = x.shape
    n = w.shape[1]
    return pl.pallas_call(
        _rms_matmul_kernel,
        grid=(m // tm, n // tn),
        in_specs=[pl.BlockSpec((tm, k), lambda i, j: (i, 0)),
                  pl.BlockSpec((1, k), lambda i, j: (0, 0)),
                  pl.BlockSpec((k, tn), lambda i, j: (0, j))],
        out_specs=pl.BlockSpec((tm, tn), lambda i, j: (i, j)),
        out_shape=jax.ShapeDtypeStruct((m, n), F32),
        scratch_shapes=[pltpu.VMEM((tm, k), BF16)],
        compiler_params=_params("parallel", "arbitrary"),
        name=name,
    )(x, g.reshape(1, k), w)


def _split3(x):
    hi = x.astype(BF16)
    r = x - hi.astype(F32)
    mid = r.astype(BF16)
    lo = (r - mid.astype(F32)).astype(BF16)
    return hi, mid, lo


def _prep_kernel(main_ref, prev_ref, next_ref, gate_ref, cw_ref, alog_ref, dt_ref,
                 q_ref, k_ref, v_ref, go_ref, gt_ref, ext_ref):
    i = pl.program_id(0)
    rows = main_ref.shape[0]
    halo = CONV_HALO
    pad = (CONV_WIDTH - 1) // 2

    ext_ref[0:halo, :] = jnp.where(i > 0, prev_ref[...], 0.0)
    ext_ref[halo:halo + rows, :] = main_ref[...]
    ext_ref[halo + rows:, :] = jnp.where(i < pl.num_programs(0) - 1, next_ref[...], 0.0)

    for s in range(3 * DN_HEADS):
        cols = slice(s * HEAD_DIM, (s + 1) * HEAD_DIM)
        xe = ext_ref[:, cols]
        acc = cw_ref[pad:pad + 1, cols] * xe[halo:halo + rows]
        for j in range(CONV_WIDTH):
            if j != pad:
                shifted = pltpu.roll(xe, shift=(pad - j) % xe.shape[0], axis=0)[halo:halo + rows]
                acc = acc + cw_ref[j:j + 1, cols] * shifted
        y = acc * jax.nn.sigmoid(acc)
        if s < 2 * DN_HEADS:
            y = y * lax.rsqrt(jnp.sum(y * y, axis=-1, keepdims=True) + L2_EPS)
        if s < DN_HEADS:
            q_ref[:, cols] = y * (HEAD_DIM ** -0.5)
        elif s < 2 * DN_HEADS:
            k_ref[:, slice((s - DN_HEADS) * HEAD_DIM, (s - DN_HEADS + 1) * HEAD_DIM)] = y
        else:
            v_ref[:, slice((s - 2 * DN_HEADS) * HEAD_DIM, (s - 2 * DN_HEADS + 1) * HEAD_DIM)] = y

    t = gate_ref[...]
    beta = jax.nn.sigmoid(t)
    a = t + dt_ref[...]
    softplus = jnp.maximum(a, 0.0) + jnp.log1p(jnp.exp(-jnp.abs(a)))
    g = -jnp.exp(alog_ref[...]) * softplus

    ri = lax.broadcasted_iota(jnp.int32, (rows, rows), 0)
    ci = lax.broadcasted_iota(jnp.int32, (rows, rows), 1)
    shift = DN_CHUNK.bit_length() - 1
    same_chunk = (ri >> shift) == (ci >> shift)
    lower = jnp.where(same_chunk & (ci <= ri), 1.0, 0.0).astype(BF16)
    upper = jnp.where(same_chunk & (ci >= ri), 1.0, 0.0).astype(BF16)
    parts = _split3(g)
    gc_f = _dot(lower, parts[0]) + _dot(lower, parts[1]) + _dot(lower, parts[2])
    gc_b = _dot(upper, parts[0]) + _dot(upper, parts[1]) + _dot(upper, parts[2])

    col = lax.broadcasted_iota(jnp.int32, t.shape, 1)
    out = jnp.where(col < GATE_G_F, beta, jnp.where(col < GATE_G_B, gc_f, gc_b))
    go_ref[...] = out
    gt_ref[...] = out.T[0:GATE_ROWS, :]


def _dn_prep(proj, conv_w, alog_row, dt_row, *, rows):
    s = proj.shape[0]
    nblk = s // rows
    hb = rows // CONV_HALO
    last_halo = s // CONV_HALO - 1
    head_cols = DN_HEADS * HEAD_DIM
    return pl.pallas_call(
        _prep_kernel,
        grid=(nblk,),
        in_specs=[pl.BlockSpec((rows, DN_QKV), lambda i: (i, 0)),
                  pl.BlockSpec((CONV_HALO, DN_QKV), lambda i: (jnp.maximum(i * hb - 1, 0), 0)),
                  pl.BlockSpec((CONV_HALO, DN_QKV), lambda i: (jnp.minimum((i + 1) * hb, last_halo), 0)),
                  pl.BlockSpec((rows, LANES), lambda i: (i, GATE_OFF // LANES)),
                  pl.BlockSpec((CONV_WIDTH, DN_QKV), lambda i: (0, 0)),
                  pl.BlockSpec((1, LANES), lambda i: (0, 0)),
                  pl.BlockSpec((1, LANES), lambda i: (0, 0))],
        out_specs=[pl.BlockSpec((rows, head_cols), lambda i: (i, 0)),
                   pl.BlockSpec((rows, head_cols), lambda i: (i, 0)),
                   pl.BlockSpec((rows, head_cols), lambda i: (i, 0)),
                   pl.BlockSpec((rows, LANES), lambda i: (i, 0)),
                   pl.BlockSpec((GATE_ROWS, rows), lambda i: (0, i))],
        out_shape=[jax.ShapeDtypeStruct((s, head_cols), F32),
                   jax.ShapeDtypeStruct((s, head_cols), F32),
                   jax.ShapeDtypeStruct((s, head_cols), F32),
                   jax.ShapeDtypeStruct((s, LANES), F32),
                   jax.ShapeDtypeStruct((GATE_ROWS, s), F32)],
        scratch_shapes=[pltpu.VMEM((rows + 2 * CONV_HALO, DN_QKV), F32)],
        compiler_params=_params("parallel"),
        name="dn_prep",
    )(proj, proj, proj, proj, conv_w, alog_row, dt_row)


def _unit_tri_inverse(ms, eye, same_sub, _mm):
    mds = [jnp.where(same_sub, m, 0.0) for m in ms]
    es = [m - md for m, md in zip(ms, mds)]
    m2 = [_mm(md, md) for md in mds]
    xs = [eye - md for md in mds]
    m4 = [_mm(a, a) for a in m2]
    xs = [x + _mm(x, a) for x, a in zip(xs, m2)]
    m8 = [_mm(a, a) for a in m4]
    xs = [x + _mm(x, a) for x, a in zip(xs, m4)]
    xs = [x + _mm(x, a) for x, a in zip(xs, m8)]
    ns = [_mm(x, e) for x, e in zip(xs, es)]
    n2 = [_mm(n, n) for n in ns]
    ys = [(b - n) - _mm(n, b) for n, b in zip(ns, n2)]
    return [x + _mm(y, x) for x, y in zip(xs, ys)]


def _deltanet_kernel(qf, kf, vf, gf, gtf, qb, kb, vb, gb, gtb, of_ref, ob_ref, state_ref):
    c_len = DN_CHUNK
    nchunk = qf.shape[0] // c_len
    heads = qf.shape[1] // HEAD_DIM
    head0 = pl.program_id(0) * heads

    @pl.when(pl.program_id(1) == 0)
    def _():
        state_ref[...] = jnp.zeros_like(state_ref)

    p_len = 2 * c_len
    ri = lax.broadcasted_iota(jnp.int32, (c_len, p_len), 0)
    cl = lax.broadcasted_iota(jnp.int32, (c_len, p_len), 1)
    ci = cl & (c_len - 1)
    first = cl < c_len
    eye = jnp.where(ri == ci, 1.0, 0.0)
    sub_shift = DN_SUB.bit_length() - 1
    same_sub = (ri >> sub_shift) == (ci >> sub_shift)
    lane = lax.broadcasted_iota(jnp.int32, (p_len, LANES), 1)
    upper_rows = lax.broadcasted_iota(jnp.int32, (p_len, 1), 0) >= c_len
    first_tall = lax.broadcasted_iota(jnp.int32, (c_len + HEAD_DIM, p_len), 1) < c_len
    zeros = jnp.zeros((c_len, HEAD_DIM), BF16)
    masks = {False: (ci <= ri, ci < ri), True: (ci >= ri, ci > ri)}
    refs = {False: (qf, kf, vf, gf, gtf, of_ref), True: (qb, kb, vb, gb, gtb, ob_ref)}

    def pair_mm(x, p):
        blockdiag = jnp.concatenate([jnp.where(first, p, 0.0), jnp.where(first, 0.0, p)], axis=0)
        return _dot(x.astype(BF16), blockdiag.astype(BF16))

    chains = [(rev, hd) for rev in (False, True) for hd in range(heads)]
    inst = [(rev, hd, p) for rev, hd in chains for p in range(nchunk // 2)]

    g_rows = {}
    for rev, hd in chains:
        g_col = head0 + hd + (GATE_G_B if rev else GATE_G_F)
        g_rows[rev, hd] = refs[rev][4][pl.ds(g_col, 1), :]

    def load(rev, hd, p):
        q_ref, k_ref, v_ref, g_ref = refs[rev][:4]
        rows = slice(p * p_len, (p + 1) * p_len)
        cols = slice(hd * HEAD_DIM, (hd + 1) * HEAD_DIM)
        gates = g_ref[rows, :]
        beta_col = head0 + hd + (GATE_BETA_B if rev else GATE_BETA_F)
        g_col = head0 + hd + (GATE_G_B if rev else GATE_G_F)
        beta = jnp.sum(jnp.where(lane == beta_col, gates, 0.0), axis=1, keepdims=True)
        gcol = jnp.sum(jnp.where(lane == g_col, gates, 0.0), axis=1, keepdims=True)
        grow = g_rows[rev, hd][:, rows]
        if rev:
            glast = (grow[:, 0:1], grow[:, c_len:c_len + 1])
        else:
            glast = (grow[:, c_len - 1:c_len], grow[:, p_len - 1:p_len])
        return q_ref[rows, cols], k_ref[rows, cols], v_ref[rows, cols], beta, gcol, grow, glast

    lo, hi = slice(0, c_len), slice(c_len, p_len)

    data = [load(*i) for i in inst]
    kbeta = [k * beta for (_, k, _, beta, _, _, _) in data]
    prod = []
    for kb_, (q, k, _, _, _, _, _) in zip(kbeta, data):
        lhs = jnp.concatenate([jnp.concatenate([kb_[lo], kb_[hi]], axis=1),
                               jnp.concatenate([q[lo], q[hi]], axis=1)], axis=0).astype(BF16)
        kb16 = k.astype(BF16)
        rhs = jnp.concatenate([jnp.concatenate([kb16[lo], zeros], axis=1),
                               jnp.concatenate([zeros, kb16[hi]], axis=1)], axis=0)
        prod.append(_dot_nt(lhs, rhs))
    decay = [jnp.exp(jnp.where(masks[rev][0], jnp.where(first, gcol[lo], gcol[hi]) - grow, -jnp.inf))
             for (rev, _, _), (_, _, _, _, gcol, grow, _) in zip(inst, data)]
    ms = [jnp.where(masks[rev][1], p[lo] * dec, 0.0) for (rev, _, _), p, dec in zip(inst, prod, decay)]
    a_mat = [p[hi] * dec for p, dec in zip(prod, decay)]
    ts = _unit_tri_inverse(ms, eye, same_sub, pair_mm)
    egs = [jnp.exp(gcol) for (_, _, _, _, gcol, _, _) in data]
    uw = []
    for t, kb_, eg, (_, _, v, beta, _, _, _) in zip(ts, kbeta, egs, data):
        vb_ = (v * beta).astype(BF16)
        kg = (kb_ * eg).astype(BF16)
        rhs = jnp.concatenate([jnp.concatenate([vb_[lo], kg[lo], zeros, zeros], axis=1),
                               jnp.concatenate([zeros, zeros, vb_[hi], kg[hi]], axis=1)], axis=0)
        uw.append(_dot(t.astype(BF16), rhs))
    qdec = [q * eg for eg, (q, _, _, _, _, _, _) in zip(egs, data)]
    kdec_t = [(k * jnp.exp(jnp.where(upper_rows, glast[1], glast[0]) - gcol)).T
              for (_, k, _, _, gcol, _, glast) in data]
    tall = [jnp.concatenate([a_, kt], axis=0) for a_, kt in zip(a_mat, kdec_t)]
    pre = {}
    for (rev, hd, p), x, qd, tl, (_, _, _, _, _, _, glast) in zip(inst, uw, qdec, tall, data):
        for half, rs in enumerate((lo, hi)):
            u = x[:, 2 * half * HEAD_DIM:(2 * half + 1) * HEAD_DIM]
            w_ = x[:, (2 * half + 1) * HEAD_DIM:(2 * half + 2) * HEAD_DIM]
            wq = jnp.concatenate([w_, qd[rs]], axis=0).astype(BF16)
            keep = first_tall if half == 0 else ~first_tall
            pre[rev, hd, 2 * p + half] = (u, wq, jnp.where(keep, tl, 0.0).astype(BF16), jnp.exp(glast[half]))

    order = {False: list(range(nchunk)), True: list(range(nchunk - 1, -1, -1))}
    states = [state_ref[n] for n in range(len(chains))]
    for t in range(nchunk):
        cur = [pre[rev, hd, order[rev][t]] for rev, hd in chains]
        ws = [_dot(wq, s_.astype(BF16)) for (_, wq, _, _), s_ in zip(cur, states)]
        v_new = [(u - x[lo]).astype(BF16) for (u, _, _, _), x in zip(cur, ws)]
        upd = [_dot(tl, jnp.concatenate([vn, vn], axis=0)) for (_, _, tl, _), vn in zip(cur, v_new)]
        states = [s_ * g_ + x[c_len:] for (_, _, _, g_), s_, x in zip(cur, states, upd)]
        for (rev, hd), x, y in zip(chains, ws, upd):
            c = order[rev][t]
            refs[rev][5][c * c_len:(c + 1) * c_len, hd * HEAD_DIM:(hd + 1) * HEAD_DIM] = x[hi] + y[lo]
    for n, s_ in enumerate(states):
        state_ref[n] = s_


def _deltanet(q, k, v, gates, gates_t, *, rows, heads):
    s = q.shape[0]
    nb = s // rows
    fwd = lambda h, b: (b, h)
    bwd = lambda h, b: (nb - 1 - b, h)
    head = lambda im: pl.BlockSpec((rows, heads * HEAD_DIM), im)
    gate = lambda im: pl.BlockSpec((rows, LANES), lambda h, b: (im(h, b)[0], 0))
    gate_t = lambda im: pl.BlockSpec((GATE_ROWS, rows), lambda h, b: (0, im(h, b)[0]))
    return pl.pallas_call(
        _deltanet_kernel,
        grid=(DN_HEADS // heads, nb),
        in_specs=[head(fwd), head(fwd), head(fwd), gate(fwd), gate_t(fwd),
                  head(bwd), head(bwd), head(bwd), gate(bwd), gate_t(bwd)],
        out_specs=[head(fwd), head(bwd)],
        out_shape=[jax.ShapeDtypeStruct(q.shape, F32), jax.ShapeDtypeStruct(q.shape, F32)],
        scratch_shapes=[pltpu.VMEM((2 * heads, HEAD_DIM, HEAD_DIM), F32)],
        compiler_params=_params("parallel", "arbitrary"),
        name="deltanet",
    )(q, k, v, gates, gates_t, q, k, v, gates, gates_t)


def _t5_bucket(rel):
    nb = NUM_BUCKETS // 2
    max_exact = nb // 2
    n = jnp.abs(rel)
    large = max_exact + (jnp.log(jnp.maximum(n, max_exact).astype(F32) / max_exact)
                         / math.log(MAX_DISTANCE / max_exact) * (nb - max_exact)).astype(jnp.int32)
    large = jnp.minimum(large, nb - 1)
    return jnp.where(rel > 0, nb, 0) + jnp.where(n < max_exact, n, large)


def _swa_kernel(q_ref, kp_ref, kc_ref, kn_ref, vp_ref, vc_ref, vn_ref, bucket_ref, rb_ref, sink_ref,
                o_ref, bias_ref, *, seq):
    n = pl.program_id(0)
    w = WINDOW
    qblk = q_ref.shape[0] // w
    group = SWA_HEADS // SWA_KV_HEADS
    ri = lax.broadcasted_iota(jnp.int32, (w, 3 * w), 0)
    ci = lax.broadcasted_iota(jnp.int32, (w, 3 * w), 1)

    @pl.when(n == 0)
    def _():
        bucket = bucket_ref[...]
        in_band = jnp.abs(ci - w - ri) <= w
        for hd in range(SWA_HEADS):
            acc = jnp.zeros((w, 3 * w), F32)
            for b in range(NUM_BUCKETS):
                acc = jnp.where(bucket == b, rb_ref[b, hd], acc)
            bias_ref[hd] = jnp.where(in_band, acc, -jnp.inf)

    key_pos = (n * qblk - 1) * w + ci
    kext, vext = [], []
    for kvh in range(SWA_KV_HEADS):
        kcols = slice(kvh * HEAD_DIM, (kvh + 1) * HEAD_DIM)
        kext.append(jnp.concatenate([kp_ref[:, kcols], kc_ref[:, kcols], kn_ref[:, kcols]], axis=0).astype(BF16))
        vext.append(jnp.concatenate([vp_ref[:, kcols], vc_ref[:, kcols], vn_ref[:, kcols]], axis=0).astype(BF16))
    heads = range(SWA_HEADS)
    sinks = [sink_ref[hd] for hd in heads]
    for t in range(qblk):
        rows = slice(t * w, (t + 1) * w)
        band = slice(t * w, (t + 3) * w)
        s = [_dot_nt(q_ref[rows, hd * HEAD_DIM:(hd + 1) * HEAD_DIM].astype(BF16), kext[hd // group][band])
             * (HEAD_DIM ** -0.5) + bias_ref[hd] for hd in heads]
        if t == 0:
            s = [jnp.where(key_pos >= 0, x, -jnp.inf) for x in s]
        if t == qblk - 1:
            s = [jnp.where(key_pos + t * w < seq, x, -jnp.inf) for x in s]
        mx = [jnp.maximum(jnp.max(x, axis=1, keepdims=True), sk) for x, sk in zip(s, sinks)]
        p = [jnp.exp(x - m) for x, m in zip(s, mx)]
        den = [jnp.sum(x, axis=1, keepdims=True) + jnp.exp(sk - m) for x, sk, m in zip(p, sinks, mx)]
        o = [_dot(x.astype(BF16), vext[hd // group][band]) / dn for hd, x, dn in zip(heads, p, den)]
        for hd, x in zip(heads, o):
            o_ref[rows, hd * HEAD_DIM:(hd + 1) * HEAD_DIM] = x.astype(o_ref.dtype)


def _swa(proj, bucket, rel_bias, sink, *, qblk):
    s = proj.shape[0]
    w = WINDOW
    rows = qblk * w
    nb = s // w
    qw = SWA_HEADS * HEAD_DIM
    kvw = SWA_KV_HEADS * HEAD_DIM
    main = lambda off: pl.BlockSpec((rows, kvw), lambda n: (n, off // kvw))
    halo = lambda off, lo: pl.BlockSpec(
        (w, kvw), lambda n: (jnp.clip(n * qblk - 1 if lo else (n + 1) * qblk, 0, nb - 1), off // kvw))
    return pl.pallas_call(
        functools.partial(_swa_kernel, seq=s),
        grid=(s // rows,),
        in_specs=[pl.BlockSpec((rows, qw), lambda n: (n, QSW_OFF // qw)),
                  halo(KSW_OFF, True), main(KSW_OFF), halo(KSW_OFF, False),
                  halo(VSW_OFF, True), main(VSW_OFF), halo(VSW_OFF, False),
                  pl.BlockSpec((w, 3 * w), lambda n: (0, 0)),
                  pl.BlockSpec(memory_space=pltpu.SMEM),
                  pl.BlockSpec(memory_space=pltpu.SMEM)],
        out_specs=pl.BlockSpec((rows, qw), lambda n: (n, 0)),
        out_shape=jax.ShapeDtypeStruct((s, qw), BF16),
        scratch_shapes=[pltpu.VMEM((SWA_HEADS, w, 3 * w), F32)],
        compiler_params=_params("arbitrary"),
        name="swa",
    )(proj, proj, proj, proj, proj, proj, proj, bucket, rel_bias, sink)


ROW_BATCH = 128


def _out_proj_kernel(of_ref, ob_ref, z_ref, sw_ref, g_ref, w_ref, x_ref, o_ref):
    for r in range(0, o_ref.shape[0], ROW_BATCH):
        rows = slice(r, r + ROW_BATCH)
        parts = []
        for hd in range(DN_HEADS):
            cols = slice(hd * HEAD_DIM, (hd + 1) * HEAD_DIM)
            z = z_ref[rows, cols]
            y = _rms(of_ref[rows, cols] + ob_ref[rows, cols], g_ref[...]) * (z * jax.nn.sigmoid(z))
            parts.append(y.astype(BF16))
        parts.append(sw_ref[rows, :])
        o_ref[rows, :] = x_ref[rows, :] + _dot(jnp.concatenate(parts, axis=1), w_ref[...])


def _out_proj(o_f, o_b, proj, y_sw, dn_g, w_out, x, *, tm):
    s, d = x.shape
    dn = DN_HEADS * HEAD_DIM
    mix = w_out.shape[0]
    return pl.pallas_call(
        _out_proj_kernel,
        grid=(s // tm,),
        in_specs=[pl.BlockSpec((tm, dn), lambda i: (i, 0)),
                  pl.BlockSpec((tm, dn), lambda i: (i, 0)),
                  pl.BlockSpec((tm, dn), lambda i: (i, Z_OFF // dn)),
                  pl.BlockSpec((tm, mix - dn), lambda i: (i, 0)),
                  pl.BlockSpec((1, HEAD_DIM), lambda i: (0, 0)),
                  pl.BlockSpec((mix, d), lambda i: (0, 0)),
                  pl.BlockSpec((tm, d), lambda i: (i, 0))],
        out_specs=pl.BlockSpec((tm, d), lambda i: (i, 0)),
        out_shape=jax.ShapeDtypeStruct((s, d), F32),
        compiler_params=_params("parallel"),
        name="out_proj",
    )(o_f, o_b, proj, y_sw, dn_g.reshape(1, HEAD_DIM), w_out, x)


def _mem_attn_kernel(h_ref, gx_ref, wq_ref, k_ref, v_ref, wo_ref, gf_ref, h2_ref, f_ref):
    h = h_ref[...]
    q = _dot(_rms(h, gx_ref[...]).astype(BF16), wq_ref[...])
    heads = []
    for hd in range(MEM_HEADS):
        cols = slice(hd * HEAD_DIM, (hd + 1) * HEAD_DIM)
        s = _dot_nt(q[:, cols].astype(BF16), k_ref[:, cols]) * (HEAD_DIM ** -0.5)
        p = jnp.exp(s - jnp.max(s, axis=1, keepdims=True))
        den = jnp.sum(p, axis=1, keepdims=True)
        heads.append((_dot(p.astype(BF16), v_ref[:, cols]) / den).astype(BF16))
    h2 = h + _dot(jnp.concatenate(heads, axis=1), wo_ref[...])
    h2_ref[...] = h2
    f_ref[...] = _rms(h2, gf_ref[...]).astype(f_ref.dtype)


def _mem_attn(h, gx, wq, k, v, wo, gf, *, tm):
    s, d = h.shape
    full = lambda a: pl.BlockSpec(a.shape, lambda i: (0, 0))
    gx, gf = gx.reshape(1, d), gf.reshape(1, d)
    return pl.pallas_call(
        _mem_attn_kernel,
        grid=(s // tm,),
        in_specs=[pl.BlockSpec((tm, d), lambda i: (i, 0)), full(gx), full(wq), full(k), full(v), full(wo),
                  full(gf)],
        out_specs=[pl.BlockSpec((tm, d), lambda i: (i, 0)), pl.BlockSpec((tm, d), lambda i: (i, 0))],
        out_shape=[jax.ShapeDtypeStruct((s, d), F32), jax.ShapeDtypeStruct((s, d), BF16)],
        compiler_params=_params("parallel"),
        name="mem_attn",
    )(h, gx, wq, k, v, wo, gf)


def _glu_kernel(f_ref, wg_ref, wu_ref, o_ref):
    f = f_ref[...]
    a = _dot(f, wg_ref[...].astype(BF16))
    o_ref[...] = (a * jax.nn.sigmoid(a) * _dot(f, wu_ref[...].astype(BF16))).astype(o_ref.dtype)


def _ffn_glu(f, wg, wu, *, tm, tn):
    s, d = f.shape
    dff = wg.shape[1]
    return pl.pallas_call(
        _glu_kernel,
        grid=(s // tm, dff // tn),
        in_specs=[pl.BlockSpec((tm, d), lambda i, j: (i, 0)),
                  pl.BlockSpec((d, tn), lambda i, j: (0, j)),
                  pl.BlockSpec((d, tn), lambda i, j: (0, j))],
        out_specs=pl.BlockSpec((tm, tn), lambda i, j: (i, j)),
        out_shape=jax.ShapeDtypeStruct((s, dff), BF16),
        compiler_params=_params("parallel", "parallel"),
        name="ffn_glu",
    )(f, wg, wu)


def _down_kernel(a_ref, w_ref, h_ref, g_ref, o_ref, acc_ref):
    k = pl.program_id(1)

    @pl.when(k == 0)
    def _():
        acc_ref[...] = h_ref[...]

    acc_ref[...] += _dot(a_ref[...], w_ref[...])

    @pl.when(k == pl.num_programs(1) - 1)
    def _():
        o_ref[...] = _rms(acc_ref[...], g_ref[...])


def _ffn_down(act, wd, h, g, *, tm, tk):
    s, dff = act.shape
    d = wd.shape[1]
    return pl.pallas_call(
        _down_kernel,
        grid=(s // tm, dff // tk),
        in_specs=[pl.BlockSpec((tm, tk), lambda i, k: (i, k)),
                  pl.BlockSpec((tk, d), lambda i, k: (k, 0)),
                  pl.BlockSpec((tm, d), lambda i, k: (i, 0)),
                  pl.BlockSpec((1, d), lambda i, k: (0, 0))],
        out_specs=pl.BlockSpec((tm, d), lambda i, k: (i, 0)),
        out_shape=jax.ShapeDtypeStruct((s, d), F32),
        scratch_shapes=[pltpu.VMEM((tm, d), F32)],
        compiler_params=_params("parallel", "arbitrary"),
        name="ffn_down",
    )(act, wd, h, g.reshape(1, d))


def _gate_row(fwd, bwd):
    row = jnp.zeros((1, LANES), F32)
    row = row.at[0, GATE_G_F:GATE_G_F + DN_HEADS].set(fwd.astype(F32))
    return row.at[0, GATE_G_B:GATE_G_B + DN_HEADS].set(bwd.astype(F32))


def _pick(n, *cands):
    for c in cands:
        if n % c == 0:
            return c
    return n


def kernel(x, mem, norm_mix_g, w_in, conv_w, a_log_f, a_log_b, dt_bias_f, dt_bias_b, dn_norm_g, attn_sink, rel_bias, w_out, norm_x_g, norm_mem_g, w_q_mem, w_kv_mem, w_o_mem, norm_ffn_g, w_gate, w_up, w_down, norm_final_g):
    batch, s, d = x.shape
    assert batch == 1 and mem.shape[0] == 1 and w_in.shape[0] == 1, "single sequence, single layer"
    w = WINDOW
    rel = (jnp.arange(3 * w)[None, :] - w) - jnp.arange(w)[:, None]
    bucket = _t5_bucket(rel).astype(jnp.int32)
    mem_dim = MEM_HEADS * HEAD_DIM

    h = x.reshape(s, d)
    mem2 = mem.reshape(mem.shape[1], d)
    tm_big = _pick(s, 1024, 512, 256, 128)
    tm_mid = _pick(s, 512, 256, 128)
    w_r = _w_in_layout(w_in[0], rows=256)
    proj = _rms_matmul(h, norm_mix_g[0], w_r, tm=tm_big, tn=_pick(PROJ_WIDTH, 1920, 1152, 640, 128), name="in_proj")

    q, k, v, gates, gates_t = _dn_prep(proj, conv_w[0], _gate_row(a_log_f[0], a_log_b[0]),
                                       _gate_row(dt_bias_f[0], dt_bias_b[0]), rows=_pick(s, 256, 128))
    o_f, o_b = _deltanet(q, k, v, gates, gates_t, rows=_pick(s, 256, 128), heads=8)
    y_sw = _swa(proj, bucket, rel_bias.astype(F32), attn_sink[0].astype(F32), qblk=_pick(s // w, 4, 2, 1))
    h = _out_proj(o_f, o_b, proj, y_sw, dn_norm_g[0], w_out[0].astype(BF16), h, tm=tm_mid)

    kv = _rms_matmul(mem2, norm_mem_g[0], w_kv_mem[0].astype(BF16), tm=mem2.shape[0], tn=2 * mem_dim,
                     name="mem_kv").astype(BF16)
    h, f = _mem_attn(h, norm_x_g[0], w_q_mem[0].astype(BF16), kv[:, :mem_dim], kv[:, mem_dim:],
                     w_o_mem[0].astype(BF16), norm_ffn_g[0], tm=tm_mid)

    act = _ffn_glu(f, w_gate[0], w_up[0], tm=tm_big, tn=512)
    out = _ffn_down(act, w_down[0].astype(BF16), h, norm_final_g, tm=tm_mid, tk=2816)
    return out.reshape(batch, s, d)
```

```python
import functools
import math

import jax
import jax.numpy as jnp
from jax import lax
from jax.experimental import pallas as pl
from jax.experimental.pallas import tpu as pltpu

F32 = jnp.float32
BF16 = jnp.bfloat16

RMS_EPS = 1e-6
L2_EPS = 1e-6
HEAD_DIM = 128
DN_HEADS = 8
DN_CHUNK = 64
DN_SUB = 16
CONV_WIDTH = 5
CONV_HALO = 8
SWA_HEADS = 8
SWA_KV_HEADS = 2
WINDOW = 128
NUM_BUCKETS = 32
MAX_DISTANCE = 128
MEM_HEADS = 4
LANES = 128

DN_QKV = 3 * DN_HEADS * HEAD_DIM
Z_OFF = DN_QKV
QSW_OFF = Z_OFF + DN_HEADS * HEAD_DIM
KSW_OFF = QSW_OFF + SWA_HEADS * HEAD_DIM
VSW_OFF = KSW_OFF + SWA_KV_HEADS * HEAD_DIM
GATE_OFF = VSW_OFF + SWA_KV_HEADS * HEAD_DIM
PROJ_WIDTH = GATE_OFF + LANES
GATE_BETA_F, GATE_BETA_B, GATE_G_F, GATE_G_B = 0, DN_HEADS, 2 * DN_HEADS, 3 * DN_HEADS
GATE_ROWS = 4 * DN_HEADS

VMEM_LIMIT_V7X = 56 * 1024 * 1024


def _params(*sem):
    return pltpu.CompilerParams(dimension_semantics=sem, vmem_limit_bytes=VMEM_LIMIT_V7X)


def _dot(a, b):
    return jnp.dot(a, b, preferred_element_type=F32)


def _dot_nt(a, b):
    return lax.dot_general(a, b, (((1,), (1,)), ((), ())), preferred_element_type=F32)


def _dot_tn(a, b):
    return lax.dot_general(a, b, (((0,), (0,)), ((), ())), preferred_element_type=F32)


def _rms(x, g):
    return x * lax.rsqrt(jnp.mean(x * x, axis=-1, keepdims=True) + RMS_EPS) * g


def _w_in_kernel(w_ref, o_ref):
    gate_lo = QSW_OFF
    o_ref[:gate_lo, :] = w_ref[:gate_lo, :].astype(o_ref.dtype)
    o_ref[gate_lo:GATE_OFF, :] = w_ref[gate_lo + GATE_ROWS:, :].astype(o_ref.dtype)
    o_ref[GATE_OFF:GATE_OFF + GATE_ROWS, :] = w_ref[gate_lo:gate_lo + GATE_ROWS, :].astype(o_ref.dtype)
    o_ref[GATE_OFF + GATE_ROWS:, :] = jnp.zeros((LANES - GATE_ROWS, o_ref.shape[1]), o_ref.dtype)


def _w_in_layout(w_t, *, cols):
    n, k = w_t.shape
    assert n == PROJ_WIDTH - (LANES - GATE_ROWS)
    return pl.pallas_call(
        _w_in_kernel,
        grid=(k // cols,),
        in_specs=[pl.BlockSpec((n, cols), lambda i: (0, i))],
        out_specs=pl.BlockSpec((PROJ_WIDTH, cols), lambda i: (0, i)),
        out_shape=jax.ShapeDtypeStruct((PROJ_WIDTH, k), BF16),
        compiler_params=_params("parallel"),
        name="w_in_layout",
    )(w_t)


def _rms_matmul_kernel(x_ref, g_ref, w_ref, o_ref, n_ref, *, w_transposed):
    @pl.when(pl.program_id(1) == 0)
    def _():
        n_ref[...] = _rms(x_ref[...], g_ref[...]).astype(n_ref.dtype)

    dot = _dot_nt if w_transposed else _dot
    o_ref[...] = dot(n_ref[...], w_ref[...]).astype(o_ref.dtype)


def _rms_matmul(x, g, w, *, tm, tn, name, w_transposed=False):
    m, k = x.shape
    n = w.shape[0] if w_transposed else w.shape[1]
    w_spec = pl.BlockSpec((tn, k), lambda i, j: (j, 0)) if w_transposed else pl.BlockSpec((k, tn), lambda i, j: (0, j))
    return pl.pallas_call(
        functools.partial(_rms_matmul_kernel, w_transposed=w_transposed),
        grid=(m // tm, n // tn),
        in_specs=[pl.BlockSpec((tm, k), lambda i, j: (i, 0)),
                  pl.BlockSpec((1, k), lambda i, j: (0, 0)),
                  w_spec],
        out_specs=pl.BlockSpec((tm, tn), lambda i, j: (i, j)),
        out_shape=jax.ShapeDtypeStruct((m, n), F32),
        scratch_shapes=[pltpu.VMEM((tm, k), BF16)],
        compiler_params=_params("parallel", "arbitrary"),
        name=name,
    )(x, g.reshape(1, k), w)


def _split3(x):
    hi = x.astype(BF16)
    r = x - hi.astype(F32)
    mid = r.astype(BF16)
    lo = (r - mid.astype(F32)).astype(BF16)
    return hi, mid, lo


def _prep_kernel(main_ref, prev_ref, next_ref, gate_ref, cw_ref, alog_ref, dt_ref,
                 q_ref, k_ref, v_ref, go_ref, gt_ref, ext_ref):
    i = pl.program_id(0)
    rows = main_ref.shape[0]
    halo = CONV_HALO
    pad = (CONV_WIDTH - 1) // 2

    ext_ref[0:halo, :] = jnp.where(i > 0, prev_ref[...], 0.0)
    ext_ref[halo:halo + rows, :] = main_ref[...]
    ext_ref[halo + rows:, :] = jnp.where(i < pl.num_programs(0) - 1, next_ref[...], 0.0)

    for s in range(3 * DN_HEADS):
        cols = slice(s * HEAD_DIM, (s + 1) * HEAD_DIM)
        xe = ext_ref[:, cols]
        acc = cw_ref[pad:pad + 1, cols] * xe[halo:halo + rows]
        for j in range(CONV_WIDTH):
            if j != pad:
                shifted = pltpu.roll(xe, shift=(pad - j) % xe.shape[0], axis=0)[halo:halo + rows]
                acc = acc + cw_ref[j:j + 1, cols] * shifted
        y = acc * jax.nn.sigmoid(acc)
        if s < 2 * DN_HEADS:
            y = y * lax.rsqrt(jnp.sum(y * y, axis=-1, keepdims=True) + L2_EPS)
        if s < DN_HEADS:
            q_ref[:, cols] = y * (HEAD_DIM ** -0.5)
        elif s < 2 * DN_HEADS:
            k_ref[:, slice((s - DN_HEADS) * HEAD_DIM, (s - DN_HEADS + 1) * HEAD_DIM)] = y
        else:
            v_ref[:, slice((s - 2 * DN_HEADS) * HEAD_DIM, (s - 2 * DN_HEADS + 1) * HEAD_DIM)] = y

    t = gate_ref[...]
    beta = jax.nn.sigmoid(t)
    a = t + dt_ref[...]
    softplus = jnp.maximum(a, 0.0) + jnp.log1p(jnp.exp(-jnp.abs(a)))
    g = -jnp.exp(alog_ref[...]) * softplus

    ri = lax.broadcasted_iota(jnp.int32, (rows, rows), 0)
    ci = lax.broadcasted_iota(jnp.int32, (rows, rows), 1)
    shift = DN_CHUNK.bit_length() - 1
    same_chunk = (ri >> shift) == (ci >> shift)
    lower = jnp.where(same_chunk & (ci <= ri), 1.0, 0.0).astype(BF16)
    upper = jnp.where(same_chunk & (ci >= ri), 1.0, 0.0).astype(BF16)
    parts = _split3(g)
    gc_f = _dot(lower, parts[0]) + _dot(lower, parts[1]) + _dot(lower, parts[2])
    gc_b = _dot(upper, parts[0]) + _dot(upper, parts[1]) + _dot(upper, parts[2])

    col = lax.broadcasted_iota(jnp.int32, t.shape, 1)
    out = jnp.where(col < GATE_G_F, beta, jnp.where(col < GATE_G_B, gc_f, gc_b))
    go_ref[...] = out
    gt_ref[...] = out.T[0:GATE_ROWS, :]


def _dn_prep(proj, conv_w, alog_row, dt_row, *, rows):
    s = proj.shape[0]
    nblk = s // rows
    hb = rows // CONV_HALO
    last_halo = s // CONV_HALO - 1
    head_cols = DN_HEADS * HEAD_DIM
    return pl.pallas_call(
        _prep_kernel,
        grid=(nblk,),
        in_specs=[pl.BlockSpec((rows, DN_QKV), lambda i: (i, 0)),
                  pl.BlockSpec((CONV_HALO, DN_QKV), lambda i: (jnp.maximum(i * hb - 1, 0), 0)),
                  pl.BlockSpec((CONV_HALO, DN_QKV), lambda i: (jnp.minimum((i + 1) * hb, last_halo), 0)),
                  pl.BlockSpec((rows, LANES), lambda i: (i, GATE_OFF // LANES)),
                  pl.BlockSpec((CONV_WIDTH, DN_QKV), lambda i: (0, 0)),
                  pl.BlockSpec((1, LANES), lambda i: (0, 0)),
                  pl.BlockSpec((1, LANES), lambda i: (0, 0))],
        out_specs=[pl.BlockSpec((rows, head_cols), lambda i: (i, 0)),
                   pl.BlockSpec((rows, head_cols), lambda i: (i, 0)),
                   pl.BlockSpec((rows, head_cols), lambda i: (i, 0)),
                   pl.BlockSpec((rows, LANES), lambda i: (i, 0)),
                   pl.BlockSpec((GATE_ROWS, rows), lambda i: (0, i))],
        out_shape=[jax.ShapeDtypeStruct((s, head_cols), F32),
                   jax.ShapeDtypeStruct((s, head_cols), F32),
                   jax.ShapeDtypeStruct((s, head_cols), F32),
                   jax.ShapeDtypeStruct((s, LANES), F32),
                   jax.ShapeDtypeStruct((GATE_ROWS, s), F32)],
        scratch_shapes=[pltpu.VMEM((rows + 2 * CONV_HALO, DN_QKV), F32)],
        compiler_params=_params("parallel"),
        name="dn_prep",
    )(proj, proj, proj, proj, conv_w, alog_row, dt_row)


def _unit_tri_inverse(ms, eye, same_sub, _mm):
    mds = [jnp.where(same_sub, m, 0.0) for m in ms]
    es = [m - md for m, md in zip(ms, mds)]
    m2 = [_mm(md, md) for md in mds]
    xs = [eye - md for md in mds]
    m4 = [_mm(a, a) for a in m2]
    xs = [x + _mm(x, a) for x, a in zip(xs, m2)]
    m8 = [_mm(a, a) for a in m4]
    xs = [x + _mm(x, a) for x, a in zip(xs, m4)]
    xs = [x + _mm(x, a) for x, a in zip(xs, m8)]
    ns = [_mm(x, e) for x, e in zip(xs, es)]
    n2 = [_mm(n, n) for n in ns]
    ys = [(b - n) - _mm(n, b) for n, b in zip(ns, n2)]
    return [x + _mm(y, x) for x, y in zip(xs, ys)]


def _deltanet_kernel(qf, kf, vf, gf, gtf, qb, kb, vb, gb, gtb, of_ref, ob_ref, state_ref):
    c_len = DN_CHUNK
    nchunk = qf.shape[0] // c_len
    heads = qf.shape[1] // HEAD_DIM
    head0 = pl.program_id(0) * heads

    @pl.when(pl.program_id(1) == 0)
    def _():
        state_ref[...] = jnp.zeros_like(state_ref)

    p_len = 2 * c_len
    ri = lax.broadcasted_iota(jnp.int32, (c_len, p_len), 0)
    cl = lax.broadcasted_iota(jnp.int32, (c_len, p_len), 1)
    ci = cl & (c_len - 1)
    first = cl < c_len
    eye = jnp.where(ri == ci, 1.0, 0.0)
    sub_shift = DN_SUB.bit_length() - 1
    same_sub = (ri >> sub_shift) == (ci >> sub_shift)
    lane = lax.broadcasted_iota(jnp.int32, (p_len, LANES), 1)
    upper_rows = lax.broadcasted_iota(jnp.int32, (p_len, 1), 0) >= c_len
    first_tall = lax.broadcasted_iota(jnp.int32, (c_len + HEAD_DIM, p_len), 1) < c_len
    zeros = jnp.zeros((c_len, HEAD_DIM), BF16)
    masks = {False: (ci <= ri, ci < ri), True: (ci >= ri, ci > ri)}
    refs = {False: (qf, kf, vf, gf, gtf, of_ref), True: (qb, kb, vb, gb, gtb, ob_ref)}

    def pair_mm(x, p):
        blockdiag = jnp.concatenate([jnp.where(first, p, 0.0), jnp.where(first, 0.0, p)], axis=0)
        return _dot(x.astype(BF16), blockdiag.astype(BF16))

    chains = [(rev, hd) for rev in (False, True) for hd in range(heads)]
    inst = [(rev, hd, p) for rev, hd in chains for p in range(nchunk // 2)]

    g_rows = {}
    for rev, hd in chains:
        g_col = head0 + hd + (GATE_G_B if rev else GATE_G_F)
        g_rows[rev, hd] = refs[rev][4][pl.ds(g_col, 1), :]

    def load(rev, hd, p):
        q_ref, k_ref, v_ref, g_ref = refs[rev][:4]
        rows = slice(p * p_len, (p + 1) * p_len)
        cols = slice(hd * HEAD_DIM, (hd + 1) * HEAD_DIM)
        gates = g_ref[rows, :]
        beta_col = head0 + hd + (GATE_BETA_B if rev else GATE_BETA_F)
        g_col = head0 + hd + (GATE_G_B if rev else GATE_G_F)
        beta = jnp.sum(jnp.where(lane == beta_col, gates, 0.0), axis=1, keepdims=True)
        gcol = jnp.sum(jnp.where(lane == g_col, gates, 0.0), axis=1, keepdims=True)
        grow = g_rows[rev, hd][:, rows]
        if rev:
            glast = (grow[:, 0:1], grow[:, c_len:c_len + 1])
        else:
            glast = (grow[:, c_len - 1:c_len], grow[:, p_len - 1:p_len])
        return q_ref[rows, cols], k_ref[rows, cols], v_ref[rows, cols], beta, gcol, grow, glast

    lo, hi = slice(0, c_len), slice(c_len, p_len)

    data = [load(*i) for i in inst]
    kbeta = [k * beta for (_, k, _, beta, _, _, _) in data]
    prod = []
    for kb_, (q, k, _, _, _, _, _) in zip(kbeta, data):
        lhs = jnp.concatenate([jnp.concatenate([kb_[lo], kb_[hi]], axis=1),
                               jnp.concatenate([q[lo], q[hi]], axis=1)], axis=0).astype(BF16)
        kb16 = k.astype(BF16)
        rhs = jnp.concatenate([jnp.concatenate([kb16[lo], zeros], axis=1),
                               jnp.concatenate([zeros, kb16[hi]], axis=1)], axis=0)
        prod.append(_dot_nt(lhs, rhs))
    decay = [jnp.exp(jnp.where(masks[rev][0], jnp.where(first, gcol[lo], gcol[hi]) - grow, -jnp.inf))
             for (rev, _, _), (_, _, _, _, gcol, grow, _) in zip(inst, data)]
    ms = [jnp.where(masks[rev][1], p[lo] * dec, 0.0) for (rev, _, _), p, dec in zip(inst, prod, decay)]
    a_mat = [p[hi] * dec for p, dec in zip(prod, decay)]
    ts = _unit_tri_inverse(ms, eye, same_sub, pair_mm)
    egs = [jnp.exp(gcol) for (_, _, _, _, gcol, _, _) in data]
    uw = []
    for t, kb_, eg, (_, _, v, beta, _, _, _) in zip(ts, kbeta, egs, data):
        vb_ = (v * beta).astype(BF16)
        kg = (kb_ * eg).astype(BF16)
        rhs = jnp.concatenate([jnp.concatenate([vb_[lo], kg[lo], zeros, zeros], axis=1),
                               jnp.concatenate([zeros, zeros, vb_[hi], kg[hi]], axis=1)], axis=0)
        uw.append(_dot(t.astype(BF16), rhs))
    qdec = [q * eg for eg, (q, _, _, _, _, _, _) in zip(egs, data)]
    kdec_t = [(k * jnp.exp(jnp.where(upper_rows, glast[1], glast[0]) - gcol)).T
              for (_, k, _, _, gcol, _, glast) in data]
    tall = [jnp.concatenate([a_, kt], axis=0) for a_, kt in zip(a_mat, kdec_t)]
    pre = {}
    for (rev, hd, p), x, qd, tl, (_, _, _, _, _, _, glast) in zip(inst, uw, qdec, tall, data):
        for half, rs in enumerate((lo, hi)):
            u = x[:, 2 * half * HEAD_DIM:(2 * half + 1) * HEAD_DIM]
            w_ = x[:, (2 * half + 1) * HEAD_DIM:(2 * half + 2) * HEAD_DIM]
            wq = jnp.concatenate([w_, qd[rs]], axis=0).astype(BF16)
            keep = first_tall if half == 0 else ~first_tall
            pre[rev, hd, 2 * p + half] = (u, wq, jnp.where(keep, tl, 0.0).astype(BF16), jnp.exp(glast[half]))

    order = {False: list(range(nchunk)), True: list(range(nchunk - 1, -1, -1))}
    states = [state_ref[n] for n in range(len(chains))]
    for t in range(nchunk):
        cur = [pre[rev, hd, order[rev][t]] for rev, hd in chains]
        ws = [_dot(wq, s_.astype(BF16)) for (_, wq, _, _), s_ in zip(cur, states)]
        v_new = [(u - x[lo]).astype(BF16) for (u, _, _, _), x in zip(cur, ws)]
        upd = [_dot(tl, jnp.concatenate([vn, vn], axis=0)) for (_, _, tl, _), vn in zip(cur, v_new)]
        states = [s_ * g_ + x[c_len:] for (_, _, _, g_), s_, x in zip(cur, states, upd)]
        for (rev, hd), x, y in zip(chains, ws, upd):
            c = order[rev][t]
            refs[rev][5][c * c_len:(c + 1) * c_len, hd * HEAD_DIM:(hd + 1) * HEAD_DIM] = x[hi] + y[lo]
    for n, s_ in enumerate(states):
        state_ref[n] = s_


def _deltanet(q, k, v, gates, gates_t, *, rows, heads):
    s = q.shape[0]
    nb = s // rows
    fwd = lambda h, b: (b, h)
    bwd = lambda h, b: (nb - 1 - b, h)
    head = lambda im: pl.BlockSpec((rows, heads * HEAD_DIM), im)
    gate = lambda im: pl.BlockSpec((rows, LANES), lambda h, b: (im(h, b)[0], 0))
    gate_t = lambda im: pl.BlockSpec((GATE_ROWS, rows), lambda h, b: (0, im(h, b)[0]))
    return pl.pallas_call(
        _deltanet_kernel,
        grid=(DN_HEADS // heads, nb),
        in_specs=[head(fwd), head(fwd), head(fwd), gate(fwd), gate_t(fwd),
                  head(bwd), head(bwd), head(bwd), gate(bwd), gate_t(bwd)],
        out_specs=[head(fwd), head(bwd)],
        out_shape=[jax.ShapeDtypeStruct(q.shape, F32), jax.ShapeDtypeStruct(q.shape, F32)],
        scratch_shapes=[pltpu.VMEM((2 * heads, HEAD_DIM, HEAD_DIM), F32)],
        compiler_params=_params("parallel", "arbitrary"),
        name="deltanet",
    )(q, k, v, gates, gates_t, q, k, v, gates, gates_t)


def _t5_bucket(rel):
    nb = NUM_BUCKETS // 2
    max_exact = nb // 2
    n = jnp.abs(rel)
    large = max_exact + (jnp.log(jnp.maximum(n, max_exact).astype(F32) / max_exact)
                         / math.log(MAX_DISTANCE / max_exact) * (nb - max_exact)).astype(jnp.int32)
    large = jnp.minimum(large, nb - 1)
    return jnp.where(rel > 0, nb, 0) + jnp.where(n < max_exact, n, large)


def _swa_kernel(q_ref, kp_ref, kc_ref, kn_ref, vp_ref, vc_ref, vn_ref, bucket_ref, rb_ref, sink_ref,
                o_ref, bias_ref, *, seq):
    n = pl.program_id(0)
    w = WINDOW
    qblk = q_ref.shape[0] // w
    group = SWA_HEADS // SWA_KV_HEADS
    ri = lax.broadcasted_iota(jnp.int32, (w, 3 * w), 0)
    ci = lax.broadcasted_iota(jnp.int32, (w, 3 * w), 1)

    @pl.when(n == 0)
    def _():
        bucket = bucket_ref[...]
        in_band = jnp.abs(ci - w - ri) <= w
        for hd in range(SWA_HEADS):
            acc = jnp.zeros((w, 3 * w), F32)
            for b in range(NUM_BUCKETS):
                acc = jnp.where(bucket == b, rb_ref[b, hd], acc)
            bias_ref[hd] = jnp.where(in_band, acc, -jnp.inf)

    key_pos = (n * qblk - 1) * w + ci
    kext, vext = [], []
    for kvh in range(SWA_KV_HEADS):
        kcols = slice(kvh * HEAD_DIM, (kvh + 1) * HEAD_DIM)
        kext.append(jnp.concatenate([kp_ref[:, kcols], kc_ref[:, kcols], kn_ref[:, kcols]], axis=0).astype(BF16))
        vext.append(jnp.concatenate([vp_ref[:, kcols], vc_ref[:, kcols], vn_ref[:, kcols]], axis=0).astype(BF16))
    heads = range(SWA_HEADS)
    sinks = [sink_ref[hd] for hd in heads]
    for t in range(qblk):
        rows = slice(t * w, (t + 1) * w)
        band = slice(t * w, (t + 3) * w)
        s = [_dot_nt(q_ref[rows, hd * HEAD_DIM:(hd + 1) * HEAD_DIM].astype(BF16), kext[hd // group][band])
             * (HEAD_DIM ** -0.5) + bias_ref[hd] for hd in heads]
        if t == 0:
            s = [jnp.where(key_pos >= 0, x, -jnp.inf) for x in s]
        if t == qblk - 1:
            s = [jnp.where(key_pos + t * w < seq, x, -jnp.inf) for x in s]
        mx = [jnp.maximum(jnp.max(x, axis=1, keepdims=True), sk) for x, sk in zip(s, sinks)]
        p = [jnp.exp(x - m) for x, m in zip(s, mx)]
        den = [jnp.sum(x, axis=1, keepdims=True) + jnp.exp(sk - m) for x, sk, m in zip(p, sinks, mx)]
        o = [_dot(x.astype(BF16), vext[hd // group][band]) / dn for hd, x, dn in zip(heads, p, den)]
        for hd, x in zip(heads, o):
            o_ref[rows, hd * HEAD_DIM:(hd + 1) * HEAD_DIM] = x.astype(o_ref.dtype)


def _swa(proj, bucket, rel_bias, sink, *, qblk):
    s = proj.shape[0]
    w = WINDOW
    rows = qblk * w
    nb = s // w
    qw = SWA_HEADS * HEAD_DIM
    kvw = SWA_KV_HEADS * HEAD_DIM
    main = lambda off: pl.BlockSpec((rows, kvw), lambda n: (n, off // kvw))
    halo = lambda off, lo: pl.BlockSpec(
        (w, kvw), lambda n: (jnp.clip(n * qblk - 1 if lo else (n + 1) * qblk, 0, nb - 1), off // kvw))
    return pl.pallas_call(
        functools.partial(_swa_kernel, seq=s),
        grid=(s // rows,),
        in_specs=[pl.BlockSpec((rows, qw), lambda n: (n, QSW_OFF // qw)),
                  halo(KSW_OFF, True), main(KSW_OFF), halo(KSW_OFF, False),
                  halo(VSW_OFF, True), main(VSW_OFF), halo(VSW_OFF, False),
                  pl.BlockSpec((w, 3 * w), lambda n: (0, 0)),
                  pl.BlockSpec(memory_space=pltpu.SMEM),
                  pl.BlockSpec(memory_space=pltpu.SMEM)],
        out_specs=pl.BlockSpec((rows, qw), lambda n: (n, 0)),
        out_shape=jax.ShapeDtypeStruct((s, qw), BF16),
        scratch_shapes=[pltpu.VMEM((SWA_HEADS, w, 3 * w), F32)],
        compiler_params=_params("arbitrary"),
        name="swa",
    )(proj, proj, proj, proj, proj, proj, proj, bucket, rel_bias, sink)


ROW_BATCH = 128


def _out_proj_kernel(of_ref, ob_ref, z_ref, sw_ref, g_ref, w_ref, x_ref, o_ref):
    for r in range(0, o_ref.shape[0], ROW_BATCH):
        rows = slice(r, r + ROW_BATCH)
        parts = []
        for hd in range(DN_HEADS):
            cols = slice(hd * HEAD_DIM, (hd + 1) * HEAD_DIM)
            z = z_ref[rows, cols]
            y = _rms(of_ref[rows, cols] + ob_ref[rows, cols], g_ref[...]) * (z * jax.nn.sigmoid(z))
            parts.append(y.astype(BF16))
        parts.append(sw_ref[rows, :])
        o_ref[rows, :] = x_ref[rows, :] + _dot(jnp.concatenate(parts, axis=1), w_ref[...])


def _out_proj(o_f, o_b, proj, y_sw, dn_g, w_out, x, *, tm):
    s, d = x.shape
    dn = DN_HEADS * HEAD_DIM
    mix = w_out.shape[0]
    return pl.pallas_call(
        _out_proj_kernel,
        grid=(s // tm,),
        in_specs=[pl.BlockSpec((tm, dn), lambda i: (i, 0)),
                  pl.BlockSpec((tm, dn), lambda i: (i, 0)),
                  pl.BlockSpec((tm, dn), lambda i: (i, Z_OFF // dn)),
                  pl.BlockSpec((tm, mix - dn), lambda i: (i, 0)),
                  pl.BlockSpec((1, HEAD_DIM), lambda i: (0, 0)),
                  pl.BlockSpec((mix, d), lambda i: (0, 0)),
                  pl.BlockSpec((tm, d), lambda i: (i, 0))],
        out_specs=pl.BlockSpec((tm, d), lambda i: (i, 0)),
        out_shape=jax.ShapeDtypeStruct((s, d), F32),
        compiler_params=_params("parallel"),
        name="out_proj",
    )(o_f, o_b, proj, y_sw, dn_g.reshape(1, HEAD_DIM), w_out, x)


def _mem_attn_kernel(h_ref, gx_ref, wq_ref, k_ref, v_ref, wo_ref, gf_ref, h2_ref, f_ref):
    h = h_ref[...]
    q = _dot(_rms(h, gx_ref[...]).astype(BF16), wq_ref[...])
    heads = []
    for hd in range(MEM_HEADS):
        cols = slice(hd * HEAD_DIM, (hd + 1) * HEAD_DIM)
        s = _dot_nt(q[:, cols].astype(BF16), k_ref[:, cols]) * (HEAD_DIM ** -0.5)
        p = jnp.exp(s - jnp.max(s, axis=1, keepdims=True))
        den = jnp.sum(p, axis=1, keepdims=True)
        heads.append((_dot(p.astype(BF16), v_ref[:, cols]) / den).astype(BF16))
    h2 = h + _dot(jnp.concatenate(heads, axis=1), wo_ref[...])
    h2_ref[...] = h2
    f_ref[...] = _rms(h2, gf_ref[...]).astype(f_ref.dtype)


def _mem_attn(h, gx, wq, k, v, wo, gf, *, tm):
    s, d = h.shape
    full = lambda a: pl.BlockSpec(a.shape, lambda i: (0, 0))
    gx, gf = gx.reshape(1, d), gf.reshape(1, d)
    return pl.pallas_call(
        _mem_attn_kernel,
        grid=(s // tm,),
        in_specs=[pl.BlockSpec((tm, d), lambda i: (i, 0)), full(gx), full(wq), full(k), full(v), full(wo),
                  full(gf)],
        out_specs=[pl.BlockSpec((tm, d), lambda i: (i, 0)), pl.BlockSpec((tm, d), lambda i: (i, 0))],
        out_shape=[jax.ShapeDtypeStruct((s, d), F32), jax.ShapeDtypeStruct((s, d), BF16)],
        compiler_params=_params("parallel"),
        name="mem_attn",
    )(h, gx, wq, k, v, wo, gf)


def _glu_kernel(f_ref, wg_ref, wu_ref, o_ref):
    f = f_ref[...]
    a = _dot(f, wg_ref[...].astype(BF16))
    o_ref[...] = (a * jax.nn.sigmoid(a) * _dot(f, wu_ref[...].astype(BF16))).astype(o_ref.dtype)


def _ffn_glu(f, wg, wu, *, tm, tn):
    s, d = f.shape
    dff = wg.shape[1]
    return pl.pallas_call(
        _glu_kernel,
        grid=(s // tm, dff // tn),
        in_specs=[pl.BlockSpec((tm, d), lambda i, j: (i, 0)),
                  pl.BlockSpec((d, tn), lambda i, j: (0, j)),
                  pl.BlockSpec((d, tn), lambda i, j: (0, j))],
        out_specs=pl.BlockSpec((tm, tn), lambda i, j: (i, j)),
        out_shape=jax.ShapeDtypeStruct((s, dff), BF16),
        compiler_params=_params("parallel", "parallel"),
        name="ffn_glu",
    )(f, wg, wu)


def _down_kernel(a_ref, w_ref, h_ref, g_ref, o_ref, acc_ref):
    k = pl.program_id(1)

    @pl.when(k == 0)
    def _():
        acc_ref[...] = h_ref[...]

    acc_ref[...] += _dot(a_ref[...], w_ref[...])

    @pl.when(k == pl.num_programs(1) - 1)
    def _():
        o_ref[...] = _rms(acc_ref[...], g_ref[...])


def _ffn_down(act, wd, h, g, *, tm, tk):
    s, dff = act.shape
    d = wd.shape[1]
    return pl.pallas_call(
        _down_kernel,
        grid=(s // tm, dff // tk),
        in_specs=[pl.BlockSpec((tm, tk), lambda i, k: (i, k)),
                  pl.BlockSpec((tk, d), lambda i, k: (k, 0)),
                  pl.BlockSpec((tm, d), lambda i, k: (i, 0)),
                  pl.BlockSpec((1, d), lambda i, k: (0, 0))],
        out_specs=pl.BlockSpec((tm, d), lambda i, k: (i, 0)),
        out_shape=jax.ShapeDtypeStruct((s, d), F32),
        scratch_shapes=[pltpu.VMEM((tm, d), F32)],
        compiler_params=_params("parallel", "arbitrary"),
        name="ffn_down",
    )(act, wd, h, g.reshape(1, d))


def _gate_row(fwd, bwd):
    row = jnp.zeros((1, LANES), F32)
    row = row.at[0, GATE_G_F:GATE_G_F + DN_HEADS].set(fwd.astype(F32))
    return row.at[0, GATE_G_B:GATE_G_B + DN_HEADS].set(bwd.astype(F32))


def _pick(n, *cands):
    for c in cands:
        if n % c == 0:
            return c
    return n


def kernel(x, mem, norm_mix_g, w_in, conv_w, a_log_f, a_log_b, dt_bias_f, dt_bias_b, dn_norm_g, attn_sink, rel_bias, w_out, norm_x_g, norm_mem_g, w_q_mem, w_kv_mem, w_o_mem, norm_ffn_g, w_gate, w_up, w_down, norm_final_g):
    batch, s, d = x.shape
    assert batch == 1 and mem.shape[0] == 1 and w_in.shape[0] == 1, "single sequence, single layer"
    w = WINDOW
    rel = (jnp.arange(3 * w)[None, :] - w) - jnp.arange(w)[:, None]
    bucket = _t5_bucket(rel).astype(jnp.int32)
    mem_dim = MEM_HEADS * HEAD_DIM

    h = x.reshape(s, d)
    mem2 = mem.reshape(mem.shape[1], d)
    tm_big = _pick(s, 1024, 512, 256, 128)
    tm_mid = _pick(s, 512, 256, 128)
    w_r = _w_in_layout(w_in[0].T, cols=256)
    proj = _rms_matmul(h, norm_mix_g[0], w_r, tm=tm_big, tn=_pick(PROJ_WIDTH, 1920, 1152, 640, 128), name="in_proj",
                       w_transposed=True)

    q, k, v, gates, gates_t = _dn_prep(proj, conv_w[0], _gate_row(a_log_f[0], a_log_b[0]),
                                       _gate_row(dt_bias_f[0], dt_bias_b[0]), rows=_pick(s, 256, 128))
    o_f, o_b = _deltanet(q, k, v, gates, gates_t, rows=_pick(s, 256, 128), heads=8)
    y_sw = _swa(proj, bucket, rel_bias.astype(F32), attn_sink[0].astype(F32), qblk=_pick(s // w, 4, 2, 1))
    h = _out_proj(o_f, o_b, proj, y_sw, dn_norm_g[0], w_out[0].astype(BF16), h, tm=tm_mid)

    kv = _rms_matmul(mem2, norm_mem_g[0], w_kv_mem[0].astype(BF16), tm=mem2.shape[0], tn=2 * mem_dim,
                     name="mem_kv").astype(BF16)
    h, f = _mem_attn(h, norm_x_g[0], w_q_mem[0].astype(BF16), kv[:, :mem_dim], kv[:, mem_dim:],
                     w_o_mem[0].astype(BF16), norm_ffn_g[0], tm=tm_mid)

    act = _ffn_glu(f, w_gate[0], w_up[0], tm=_pick(s, 2048, 1024, 512, 256, 128), tn=512)
    out = _ffn_down(act, w_down[0].astype(BF16), h, norm_final_g, tm=tm_mid, tk=2816)
    return out.reshape(batch, s, d)
```

```python
import functools
import math

import jax
import jax.numpy as jnp
from jax import lax
from jax.experimental import pallas as pl
from jax.experimental.pallas import tpu as pltpu

F32 = jnp.float32
BF16 = jnp.bfloat16

RMS_EPS = 1e-6
L2_EPS = 1e-6
HEAD_DIM = 128
DN_HEADS = 8
DN_CHUNK = 64
DN_SUB = 16
CONV_WIDTH = 5
CONV_HALO = 8
SWA_HEADS = 8
SWA_KV_HEADS = 2
WINDOW = 128
NUM_BUCKETS = 32
MAX_DISTANCE = 128
MEM_HEADS = 4
LANES = 128
ROW_BATCH = 128

DN_QKV = 3 * DN_HEADS * HEAD_DIM
Z_OFF = DN_QKV
QSW_OFF = Z_OFF + DN_HEADS * HEAD_DIM
KSW_OFF = QSW_OFF + SWA_HEADS * HEAD_DIM
VSW_OFF = KSW_OFF + SWA_KV_HEADS * HEAD_DIM
GATE_OFF = VSW_OFF + SWA_KV_HEADS * HEAD_DIM
PROJ_WIDTH = GATE_OFF + LANES
GATE_BETA_F, GATE_BETA_B, GATE_G_F, GATE_G_B = 0, DN_HEADS, 2 * DN_HEADS, 3 * DN_HEADS
GATE_ROWS = 4 * DN_HEADS

VMEM_LIMIT_V7X = 56 * 1024 * 1024


def _params(*sem):
    return pltpu.CompilerParams(dimension_semantics=sem, vmem_limit_bytes=VMEM_LIMIT_V7X)


def _dot(a, b):
    return jnp.dot(a, b, preferred_element_type=F32)


def _dot_nt(a, b):
    return lax.dot_general(a, b, (((1,), (1,)), ((), ())), preferred_element_type=F32)


def _dot_tn(a, b):
    return lax.dot_general(a, b, (((0,), (0,)), ((), ())), preferred_element_type=F32)


def _rms(x, g):
    return x * lax.rsqrt(jnp.mean(x * x, axis=-1, keepdims=True) + RMS_EPS) * g


def _w_in_kernel(w_ref, o_ref):
    gate_lo = QSW_OFF
    o_ref[:gate_lo, :] = w_ref[:gate_lo, :].astype(o_ref.dtype)
    o_ref[gate_lo:GATE_OFF, :] = w_ref[gate_lo + GATE_ROWS:, :].astype(o_ref.dtype)
    o_ref[GATE_OFF:GATE_OFF + GATE_ROWS, :] = w_ref[gate_lo:gate_lo + GATE_ROWS, :].astype(o_ref.dtype)
    o_ref[GATE_OFF + GATE_ROWS:, :] = jnp.zeros((LANES - GATE_ROWS, o_ref.shape[1]), o_ref.dtype)


def _w_in_layout(w_t, *, cols):
    n, k = w_t.shape
    assert n == PROJ_WIDTH - (LANES - GATE_ROWS)
    return pl.pallas_call(
        _w_in_kernel,
        grid=(k // cols,),
        in_specs=[pl.BlockSpec((n, cols), lambda i: (0, i))],
        out_specs=pl.BlockSpec((PROJ_WIDTH, cols), lambda i: (0, i)),
        out_shape=jax.ShapeDtypeStruct((PROJ_WIDTH, k), BF16),
        compiler_params=_params("parallel"),
        name="w_in_layout",
    )(w_t)


def _rms_matmul_kernel(x_ref, g_ref, w_ref, o_ref, n_ref, *, w_transposed):
    dot = _dot_nt if w_transposed else _dot

    @pl.when(pl.program_id(1) == 0)
    def _():
        half = max(x_ref.shape[0] // 2, ROW_BATCH)
        for r in range(0, x_ref.shape[0], half):
            rows = slice(r, r + half)
            n = _rms(x_ref[rows, :], g_ref[...]).astype(n_ref.dtype)
            n_ref[rows, :] = n
            o_ref[rows, :] = dot(n, w_ref[...]).astype(o_ref.dtype)

    @pl.when(pl.program_id(1) != 0)
    def _():
        o_ref[...] = dot(n_ref[...], w_ref[...]).astype(o_ref.dtype)


def _rms_matmul(x, g, w, *, tm, tn, name, w_transposed=False):
    m, k = x.shape
    n = w.shape[0] if w_transposed else w.shape[1]
    w_spec = pl.BlockSpec((tn, k), lambda i, j: (j, 0)) if w_transposed else pl.BlockSpec((k, tn), lambda i, j: (0, j))
    return pl.pallas_call(
        functools.partial(_rms_matmul_kernel, w_transposed=w_transposed),
        grid=(m // tm, n // tn),
        in_specs=[pl.BlockSpec((tm, k), lambda i, j: (i, 0)),
                  pl.BlockSpec((1, k), lambda i, j: (0, 0)),
                  w_spec],
        out_specs=pl.BlockSpec((tm, tn), lambda i, j: (i, j)),
        out_shape=jax.ShapeDtypeStruct((m, n), F32),
        scratch_shapes=[pltpu.VMEM((tm, k), BF16)],
        compiler_params=_params("parallel", "arbitrary"),
        name=name,
    )(x, g.reshape(1, k), w)


def _split3(x):
    hi = x.astype(BF16)
    r = x - hi.astype(F32)
    mid = r.astype(BF16)
    lo = (r - mid.astype(F32)).astype(BF16)
    return hi, mid, lo


def _prep_kernel(main_ref, prev_ref, next_ref, gate_ref, cw_ref, alog_ref, dt_ref,
                 q_ref, k_ref, v_ref, go_ref, gt_ref, ext_ref):
    i = pl.program_id(0)
    rows = main_ref.shape[0]
    halo = CONV_HALO
    pad = (CONV_WIDTH - 1) // 2

    ext_ref[0:halo, :] = jnp.where(i > 0, prev_ref[...], 0.0)
    ext_ref[halo:halo + rows, :] = main_ref[...]
    ext_ref[halo + rows:, :] = jnp.where(i < pl.num_programs(0) - 1, next_ref[...], 0.0)

    for s in range(3 * DN_HEADS):
        cols = slice(s * HEAD_DIM, (s + 1) * HEAD_DIM)
        xe = ext_ref[:, cols]
        acc = cw_ref[pad:pad + 1, cols] * xe[halo:halo + rows]
        for j in range(CONV_WIDTH):
            if j != pad:
                shifted = pltpu.roll(xe, shift=(pad - j) % xe.shape[0], axis=0)[halo:halo + rows]
                acc = acc + cw_ref[j:j + 1, cols] * shifted
        y = acc * jax.nn.sigmoid(acc)
        if s < 2 * DN_HEADS:
            y = y * lax.rsqrt(jnp.sum(y * y, axis=-1, keepdims=True) + L2_EPS)
        if s < DN_HEADS:
            q_ref[:, cols] = y * (HEAD_DIM ** -0.5)
        elif s < 2 * DN_HEADS:
            k_ref[:, slice((s - DN_HEADS) * HEAD_DIM, (s - DN_HEADS + 1) * HEAD_DIM)] = y
        else:
            v_ref[:, slice((s - 2 * DN_HEADS) * HEAD_DIM, (s - 2 * DN_HEADS + 1) * HEAD_DIM)] = y

    t = gate_ref[...]
    beta = jax.nn.sigmoid(t)
    a = t + dt_ref[...]
    softplus = jnp.maximum(a, 0.0) + jnp.log1p(jnp.exp(-jnp.abs(a)))
    g = -jnp.exp(alog_ref[...]) * softplus

    ri = lax.broadcasted_iota(jnp.int32, (rows, rows), 0)
    ci = lax.broadcasted_iota(jnp.int32, (rows, rows), 1)
    shift = DN_CHUNK.bit_length() - 1
    same_chunk = (ri >> shift) == (ci >> shift)
    lower = jnp.where(same_chunk & (ci <= ri), 1.0, 0.0).astype(BF16)
    upper = jnp.where(same_chunk & (ci >= ri), 1.0, 0.0).astype(BF16)
    parts = _split3(g)
    gc_f = _dot(lower, parts[0]) + _dot(lower, parts[1]) + _dot(lower, parts[2])
    gc_b = _dot(upper, parts[0]) + _dot(upper, parts[1]) + _dot(upper, parts[2])

    col = lax.broadcasted_iota(jnp.int32, t.shape, 1)
    out = jnp.where(col < GATE_G_F, beta, jnp.where(col < GATE_G_B, gc_f, gc_b))
    go_ref[...] = out
    gt_ref[...] = out.T[0:GATE_ROWS, :]


def _dn_prep(proj, conv_w, alog_row, dt_row, *, rows):
    s = proj.shape[0]
    nblk = s // rows
    hb = rows // CONV_HALO
    last_halo = s // CONV_HALO - 1
    head_cols = DN_HEADS * HEAD_DIM
    return pl.pallas_call(
        _prep_kernel,
        grid=(nblk,),
        in_specs=[pl.BlockSpec((rows, DN_QKV), lambda i: (i, 0)),
                  pl.BlockSpec((CONV_HALO, DN_QKV), lambda i: (jnp.maximum(i * hb - 1, 0), 0)),
                  pl.BlockSpec((CONV_HALO, DN_QKV), lambda i: (jnp.minimum((i + 1) * hb, last_halo), 0)),
                  pl.BlockSpec((rows, LANES), lambda i: (i, GATE_OFF // LANES)),
                  pl.BlockSpec((CONV_WIDTH, DN_QKV), lambda i: (0, 0)),
                  pl.BlockSpec((1, LANES), lambda i: (0, 0)),
                  pl.BlockSpec((1, LANES), lambda i: (0, 0))],
        out_specs=[pl.BlockSpec((rows, head_cols), lambda i: (i, 0)),
                   pl.BlockSpec((rows, head_cols), lambda i: (i, 0)),
                   pl.BlockSpec((rows, head_cols), lambda i: (i, 0)),
                   pl.BlockSpec((rows, LANES), lambda i: (i, 0)),
                   pl.BlockSpec((GATE_ROWS, rows), lambda i: (0, i))],
        out_shape=[jax.ShapeDtypeStruct((s, head_cols), F32),
                   jax.ShapeDtypeStruct((s, head_cols), F32),
                   jax.ShapeDtypeStruct((s, head_cols), F32),
                   jax.ShapeDtypeStruct((s, LANES), F32),
                   jax.ShapeDtypeStruct((GATE_ROWS, s), F32)],
        scratch_shapes=[pltpu.VMEM((rows + 2 * CONV_HALO, DN_QKV), F32)],
        compiler_params=_params("parallel"),
        name="dn_prep",
    )(proj, proj, proj, proj, conv_w, alog_row, dt_row)


def _unit_tri_inverse(ms, eye, same_sub, _mm):
    mds = [jnp.where(same_sub, m, 0.0) for m in ms]
    es = [m - md for m, md in zip(ms, mds)]
    m2 = [_mm(md, md) for md in mds]
    xs = [eye - md for md in mds]
    m4 = [_mm(a, a) for a in m2]
    xs = [x + _mm(x, a) for x, a in zip(xs, m2)]
    m8 = [_mm(a, a) for a in m4]
    xs = [x + _mm(x, a) for x, a in zip(xs, m4)]
    xs = [x + _mm(x, a) for x, a in zip(xs, m8)]
    ns = [_mm(x, e) for x, e in zip(xs, es)]
    n2 = [_mm(n, n) for n in ns]
    ys = [(b - n) - _mm(n, b) for n, b in zip(ns, n2)]
    return [x + _mm(y, x) for x, y in zip(xs, ys)]


def _deltanet_kernel(qf, kf, vf, gf, gtf, qb, kb, vb, gb, gtb, of_ref, ob_ref, state_ref):
    c_len = DN_CHUNK
    nchunk = qf.shape[0] // c_len
    heads = qf.shape[1] // HEAD_DIM
    head0 = pl.program_id(0) * heads

    @pl.when(pl.program_id(1) == 0)
    def _():
        state_ref[...] = jnp.zeros_like(state_ref)

    p_len = 2 * c_len
    ri = lax.broadcasted_iota(jnp.int32, (c_len, p_len), 0)
    cl = lax.broadcasted_iota(jnp.int32, (c_len, p_len), 1)
    ci = cl & (c_len - 1)
    first = cl < c_len
    eye = jnp.where(ri == ci, 1.0, 0.0)
    sub_shift = DN_SUB.bit_length() - 1
    same_sub = (ri >> sub_shift) == (ci >> sub_shift)
    lane = lax.broadcasted_iota(jnp.int32, (p_len, LANES), 1)
    upper_rows = lax.broadcasted_iota(jnp.int32, (p_len, 1), 0) >= c_len
    first_tall = lax.broadcasted_iota(jnp.int32, (c_len + HEAD_DIM, p_len), 1) < c_len
    zeros = jnp.zeros((c_len, HEAD_DIM), BF16)
    masks = {False: (ci <= ri, ci < ri), True: (ci >= ri, ci > ri)}
    refs = {False: (qf, kf, vf, gf, gtf, of_ref), True: (qb, kb, vb, gb, gtb, ob_ref)}

    def pair_mm(x, p):
        blockdiag = jnp.concatenate([jnp.where(first, p, 0.0), jnp.where(first, 0.0, p)], axis=0)
        return _dot(x.astype(BF16), blockdiag.astype(BF16))

    chains = [(rev, hd) for rev in (False, True) for hd in range(heads)]
    inst = [(rev, hd, p) for rev, hd in chains for p in range(nchunk // 2)]

    g_rows = {}
    for rev, hd in chains:
        g_col = head0 + hd + (GATE_G_B if rev else GATE_G_F)
        g_rows[rev, hd] = refs[rev][4][pl.ds(g_col, 1), :]

    def load(rev, hd, p):
        q_ref, k_ref, v_ref, g_ref = refs[rev][:4]
        rows = slice(p * p_len, (p + 1) * p_len)
        cols = slice(hd * HEAD_DIM, (hd + 1) * HEAD_DIM)
        gates = g_ref[rows, :]
        beta_col = head0 + hd + (GATE_BETA_B if rev else GATE_BETA_F)
        g_col = head0 + hd + (GATE_G_B if rev else GATE_G_F)
        beta = jnp.sum(jnp.where(lane == beta_col, gates, 0.0), axis=1, keepdims=True)
        gcol = jnp.sum(jnp.where(lane == g_col, gates, 0.0), axis=1, keepdims=True)
        grow = g_rows[rev, hd][:, rows]
        if rev:
            glast = (grow[:, 0:1], grow[:, c_len:c_len + 1])
        else:
            glast = (grow[:, c_len - 1:c_len], grow[:, p_len - 1:p_len])
        return q_ref[rows, cols], k_ref[rows, cols], v_ref[rows, cols], beta, gcol, grow, glast

    lo, hi = slice(0, c_len), slice(c_len, p_len)

    data = [load(*i) for i in inst]
    kbeta = [k * beta for (_, k, _, beta, _, _, _) in data]
    prod = []
    for kb_, (q, k, _, _, _, _, _) in zip(kbeta, data):
        lhs = jnp.concatenate([jnp.concatenate([kb_[lo], kb_[hi]], axis=1),
                               jnp.concatenate([q[lo], q[hi]], axis=1)], axis=0).astype(BF16)
        kb16 = k.astype(BF16)
        rhs = jnp.concatenate([jnp.concatenate([kb16[lo], zeros], axis=1),
                               jnp.concatenate([zeros, kb16[hi]], axis=1)], axis=0)
        prod.append(_dot_nt(lhs, rhs))
    decay = [jnp.exp(jnp.where(masks[rev][0], jnp.where(first, gcol[lo], gcol[hi]) - grow, -jnp.inf))
             for (rev, _, _), (_, _, _, _, gcol, grow, _) in zip(inst, data)]
    ms = [jnp.where(masks[rev][1], p[lo] * dec, 0.0) for (rev, _, _), p, dec in zip(inst, prod, decay)]
    a_mat = [p[hi] * dec for p, dec in zip(prod, decay)]
    ts = _unit_tri_inverse(ms, eye, same_sub, pair_mm)
    egs = [jnp.exp(gcol) for (_, _, _, _, gcol, _, _) in data]
    uw = []
    for t, kb_, eg, (_, _, v, beta, _, _, _) in zip(ts, kbeta, egs, data):
        vb_ = (v * beta).astype(BF16)
        kg = (kb_ * eg).astype(BF16)
        rhs = jnp.concatenate([jnp.concatenate([vb_[lo], kg[lo], zeros, zeros], axis=1),
                               jnp.concatenate([zeros, zeros, vb_[hi], kg[hi]], axis=1)], axis=0)
        uw.append(_dot(t.astype(BF16), rhs))
    qdec = [q * eg for eg, (q, _, _, _, _, _, _) in zip(egs, data)]
    kdec_t = [(k * jnp.exp(jnp.where(upper_rows, glast[1], glast[0]) - gcol)).T
              for (_, k, _, _, gcol, _, glast) in data]
    tall = [jnp.concatenate([a_, kt], axis=0) for a_, kt in zip(a_mat, kdec_t)]
    pre = {}
    for (rev, hd, p), x, qd, tl, (_, _, _, _, _, _, glast) in zip(inst, uw, qdec, tall, data):
        for half, rs in enumerate((lo, hi)):
            u = x[:, 2 * half * HEAD_DIM:(2 * half + 1) * HEAD_DIM]
            w_ = x[:, (2 * half + 1) * HEAD_DIM:(2 * half + 2) * HEAD_DIM]
            wq = jnp.concatenate([w_, qd[rs]], axis=0).astype(BF16)
            keep = first_tall if half == 0 else ~first_tall
            pre[rev, hd, 2 * p + half] = (u, wq, jnp.where(keep, tl, 0.0).astype(BF16), jnp.exp(glast[half]))

    order = {False: list(range(nchunk)), True: list(range(nchunk - 1, -1, -1))}
    states = [state_ref[n] for n in range(len(chains))]
    for t in range(nchunk):
        cur = [pre[rev, hd, order[rev][t]] for rev, hd in chains]
        ws = [_dot(wq, s_.astype(BF16)) for (_, wq, _, _), s_ in zip(cur, states)]
        v_new = [(u - x[lo]).astype(BF16) for (u, _, _, _), x in zip(cur, ws)]
        upd = [_dot(tl, jnp.concatenate([vn, vn], axis=0)) for (_, _, tl, _), vn in zip(cur, v_new)]
        states = [s_ * g_ + x[c_len:] for (_, _, _, g_), s_, x in zip(cur, states, upd)]
        for (rev, hd), x, y in zip(chains, ws, upd):
            c = order[rev][t]
            refs[rev][5][c * c_len:(c + 1) * c_len, hd * HEAD_DIM:(hd + 1) * HEAD_DIM] = x[hi] + y[lo]
    for n, s_ in enumerate(states):
        state_ref[n] = s_


def _deltanet(q, k, v, gates, gates_t, *, rows, heads):
    s = q.shape[0]
    nb = s // rows
    fwd = lambda h, b: (b, h)
    bwd = lambda h, b: (nb - 1 - b, h)
    head = lambda im: pl.BlockSpec((rows, heads * HEAD_DIM), im)
    gate = lambda im: pl.BlockSpec((rows, LANES), lambda h, b: (im(h, b)[0], 0))
    gate_t = lambda im: pl.BlockSpec((GATE_ROWS, rows), lambda h, b: (0, im(h, b)[0]))
    return pl.pallas_call(
        _deltanet_kernel,
        grid=(DN_HEADS // heads, nb),
        in_specs=[head(fwd), head(fwd), head(fwd), gate(fwd), gate_t(fwd),
                  head(bwd), head(bwd), head(bwd), gate(bwd), gate_t(bwd)],
        out_specs=[head(fwd), head(bwd)],
        out_shape=[jax.ShapeDtypeStruct(q.shape, F32), jax.ShapeDtypeStruct(q.shape, F32)],
        scratch_shapes=[pltpu.VMEM((2 * heads, HEAD_DIM, HEAD_DIM), F32)],
        compiler_params=_params("parallel", "arbitrary"),
        name="deltanet",
    )(q, k, v, gates, gates_t, q, k, v, gates, gates_t)


def _t5_bucket(rel):
    nb = NUM_BUCKETS // 2
    max_exact = nb // 2
    n = jnp.abs(rel)
    large = max_exact + (jnp.log(jnp.maximum(n, max_exact).astype(F32) / max_exact)
                         / math.log(MAX_DISTANCE / max_exact) * (nb - max_exact)).astype(jnp.int32)
    large = jnp.minimum(large, nb - 1)
    return jnp.where(rel > 0, nb, 0) + jnp.where(n < max_exact, n, large)


def _swa_kernel(q_ref, kp_ref, kc_ref, kn_ref, vp_ref, vc_ref, vn_ref, bucket_ref, rb_ref, sink_ref,
                o_ref, bias_ref, *, seq):
    n = pl.program_id(0)
    w = WINDOW
    qblk = q_ref.shape[0] // w
    group = SWA_HEADS // SWA_KV_HEADS
    ri = lax.broadcasted_iota(jnp.int32, (w, 3 * w), 0)
    ci = lax.broadcasted_iota(jnp.int32, (w, 3 * w), 1)

    @pl.when(n == 0)
    def _():
        bucket = bucket_ref[...]
        in_band = jnp.abs(ci - w - ri) <= w
        for hd in range(SWA_HEADS):
            acc = jnp.zeros((w, 3 * w), F32)
            for b in range(NUM_BUCKETS):
                acc = jnp.where(bucket == b, rb_ref[b, hd], acc)
            bias_ref[hd] = jnp.where(in_band, acc, -jnp.inf)

    key_pos = (n * qblk - 1) * w + ci
    kext, vext = [], []
    for kvh in range(SWA_KV_HEADS):
        kcols = slice(kvh * HEAD_DIM, (kvh + 1) * HEAD_DIM)
        kext.append(jnp.concatenate([kp_ref[:, kcols], kc_ref[:, kcols], kn_ref[:, kcols]], axis=0).astype(BF16))
        vext.append(jnp.concatenate([vp_ref[:, kcols], vc_ref[:, kcols], vn_ref[:, kcols]], axis=0).astype(BF16))
    heads = range(SWA_HEADS)
    sinks = [sink_ref[hd] for hd in heads]
    for t in range(qblk):
        rows = slice(t * w, (t + 1) * w)
        band = slice(t * w, (t + 3) * w)
        s = [_dot_nt(q_ref[rows, hd * HEAD_DIM:(hd + 1) * HEAD_DIM].astype(BF16), kext[hd // group][band])
             * (HEAD_DIM ** -0.5) + bias_ref[hd] for hd in heads]
        if t == 0:
            s = [jnp.where(key_pos >= 0, x, -jnp.inf) for x in s]
        if t == qblk - 1:
            s = [jnp.where(key_pos + t * w < seq, x, -jnp.inf) for x in s]
        mx = [jnp.maximum(jnp.max(x, axis=1, keepdims=True), sk) for x, sk in zip(s, sinks)]
        p = [jnp.exp(x - m) for x, m in zip(s, mx)]
        den = [jnp.sum(x, axis=1, keepdims=True) + jnp.exp(sk - m) for x, sk, m in zip(p, sinks, mx)]
        o = [_dot(x.astype(BF16), vext[hd // group][band]) / dn for hd, x, dn in zip(heads, p, den)]
        for hd, x in zip(heads, o):
            o_ref[rows, hd * HEAD_DIM:(hd + 1) * HEAD_DIM] = x.astype(o_ref.dtype)


def _swa(proj, bucket, rel_bias, sink, *, qblk):
    s = proj.shape[0]
    w = WINDOW
    rows = qblk * w
    nb = s // w
    qw = SWA_HEADS * HEAD_DIM
    kvw = SWA_KV_HEADS * HEAD_DIM
    main = lambda off: pl.BlockSpec((rows, kvw), lambda n: (n, off // kvw))
    halo = lambda off, lo: pl.BlockSpec(
        (w, kvw), lambda n: (jnp.clip(n * qblk - 1 if lo else (n + 1) * qblk, 0, nb - 1), off // kvw))
    return pl.pallas_call(
        functools.partial(_swa_kernel, seq=s),
        grid=(s // rows,),
        in_specs=[pl.BlockSpec((rows, qw), lambda n: (n, QSW_OFF // qw)),
                  halo(KSW_OFF, True), main(KSW_OFF), halo(KSW_OFF, False),
                  halo(VSW_OFF, True), main(VSW_OFF), halo(VSW_OFF, False),
                  pl.BlockSpec((w, 3 * w), lambda n: (0, 0)),
                  pl.BlockSpec(memory_space=pltpu.SMEM),
                  pl.BlockSpec(memory_space=pltpu.SMEM)],
        out_specs=pl.BlockSpec((rows, qw), lambda n: (n, 0)),
        out_shape=jax.ShapeDtypeStruct((s, qw), BF16),
        scratch_shapes=[pltpu.VMEM((SWA_HEADS, w, 3 * w), F32)],
        compiler_params=_params("arbitrary"),
        name="swa",
    )(proj, proj, proj, proj, proj, proj, proj, bucket, rel_bias, sink)


def _out_proj_kernel(of_ref, ob_ref, z_ref, sw_ref, g_ref, w_ref, x_ref, o_ref):
    for r in range(0, o_ref.shape[0], ROW_BATCH):
        rows = slice(r, r + ROW_BATCH)
        parts = []
        for hd in range(DN_HEADS):
            cols = slice(hd * HEAD_DIM, (hd + 1) * HEAD_DIM)
            z = z_ref[rows, cols]
            y = _rms(of_ref[rows, cols] + ob_ref[rows, cols], g_ref[...]) * (z * jax.nn.sigmoid(z))
            parts.append(y.astype(BF16))
        parts.append(sw_ref[rows, :])
        o_ref[rows, :] = x_ref[rows, :] + _dot(jnp.concatenate(parts, axis=1), w_ref[...])


def _out_proj(o_f, o_b, proj, y_sw, dn_g, w_out, x, *, tm):
    s, d = x.shape
    dn = DN_HEADS * HEAD_DIM
    mix = w_out.shape[0]
    return pl.pallas_call(
        _out_proj_kernel,
        grid=(s // tm,),
        in_specs=[pl.BlockSpec((tm, dn), lambda i: (i, 0)),
                  pl.BlockSpec((tm, dn), lambda i: (i, 0)),
                  pl.BlockSpec((tm, dn), lambda i: (i, Z_OFF // dn)),
                  pl.BlockSpec((tm, mix - dn), lambda i: (i, 0)),
                  pl.BlockSpec((1, HEAD_DIM), lambda i: (0, 0)),
                  pl.BlockSpec((mix, d), lambda i: (0, 0)),
                  pl.BlockSpec((tm, d), lambda i: (i, 0))],
        out_specs=pl.BlockSpec((tm, d), lambda i: (i, 0)),
        out_shape=jax.ShapeDtypeStruct((s, d), F32),
        compiler_params=_params("parallel"),
        name="out_proj",
    )(o_f, o_b, proj, y_sw, dn_g.reshape(1, HEAD_DIM), w_out, x)


def _mem_attn_kernel(h_ref, gx_ref, wq_ref, k_ref, v_ref, wo_ref, gf_ref, h2_ref, f_ref):
    h = h_ref[...]
    q = _dot(_rms(h, gx_ref[...]).astype(BF16), wq_ref[...])
    heads = []
    for hd in range(MEM_HEADS):
        cols = slice(hd * HEAD_DIM, (hd + 1) * HEAD_DIM)
        s = _dot_nt(q[:, cols].astype(BF16), k_ref[:, cols]) * (HEAD_DIM ** -0.5)
        p = jnp.exp(s - jnp.max(s, axis=1, keepdims=True))
        den = jnp.sum(p, axis=1, keepdims=True)
        heads.append((_dot(p.astype(BF16), v_ref[:, cols]) / den).astype(BF16))
    h2 = h + _dot(jnp.concatenate(heads, axis=1), wo_ref[...])
    h2_ref[...] = h2
    f_ref[...] = _rms(h2, gf_ref[...]).astype(f_ref.dtype)


def _mem_attn(h, gx, wq, k, v, wo, gf, *, tm):
    s, d = h.shape
    full = lambda a: pl.BlockSpec(a.shape, lambda i: (0, 0))
    gx, gf = gx.reshape(1, d), gf.reshape(1, d)
    return pl.pallas_call(
        _mem_attn_kernel,
        grid=(s // tm,),
        in_specs=[pl.BlockSpec((tm, d), lambda i: (i, 0)), full(gx), full(wq), full(k), full(v), full(wo),
                  full(gf)],
        out_specs=[pl.BlockSpec((tm, d), lambda i: (i, 0)), pl.BlockSpec((tm, d), lambda i: (i, 0))],
        out_shape=[jax.ShapeDtypeStruct((s, d), F32), jax.ShapeDtypeStruct((s, d), BF16)],
        compiler_params=_params("parallel"),
        name="mem_attn",
    )(h, gx, wq, k, v, wo, gf)


def _mix_mem_kernel(of_ref, ob_ref, z_ref, sw_ref, dng_ref, wout_ref, x_ref, gx_ref, wq_ref, k_ref, v_ref, wo_ref,
                    gf_ref, h2_ref, f_ref):
    batches = [slice(r, r + ROW_BATCH) for r in range(0, h2_ref.shape[0], ROW_BATCH)]
    scale = HEAD_DIM ** -0.5
    lhs = []
    for rows in batches:
        parts = []
        for hd in range(DN_HEADS):
            cols = slice(hd * HEAD_DIM, (hd + 1) * HEAD_DIM)
            z = z_ref[rows, cols]
            y = _rms(of_ref[rows, cols] + ob_ref[rows, cols], dng_ref[...]) * (z * jax.nn.sigmoid(z))
            parts.append(y.astype(BF16))
        parts.append(sw_ref[rows, :])
        lhs.append(jnp.concatenate(parts, axis=1))
    h1 = [x_ref[rows, :] + _dot(a, wout_ref[...]) for rows, a in zip(batches, lhs)]
    q = [_dot(_rms(h, gx_ref[...]).astype(BF16), wq_ref[...]) for h in h1]
    att = []
    for qb in q:
        heads = []
        for hd in range(MEM_HEADS):
            cols = slice(hd * HEAD_DIM, (hd + 1) * HEAD_DIM)
            s = _dot_nt(qb[:, cols].astype(BF16), k_ref[:, cols]) * scale
            p = jnp.exp(s - jnp.max(s, axis=1, keepdims=True))
            den = jnp.sum(p, axis=1, keepdims=True)
            heads.append((_dot(p.astype(BF16), v_ref[:, cols]) / den).astype(BF16))
        att.append(jnp.concatenate(heads, axis=1))
    h2 = [h + _dot(a, wo_ref[...]) for h, a in zip(h1, att)]
    for rows, h in zip(batches, h2):
        h2_ref[rows, :] = h
        f_ref[rows, :] = _rms(h, gf_ref[...]).astype(f_ref.dtype)


def _mix_mem(o_f, o_b, proj, y_sw, dn_g, w_out, x, gx, wq, k, v, wo, gf, *, tm):
    s, d = x.shape
    dn = DN_HEADS * HEAD_DIM
    mix = w_out.shape[0]
    row = lambda width, col=0: pl.BlockSpec((tm, width), lambda i: (i, col))
    const = lambda a: pl.BlockSpec(a.shape, lambda i: (0, 0), pipeline_mode=pl.Buffered(1))
    dn_g, gx, gf = dn_g.reshape(1, HEAD_DIM), gx.reshape(1, d), gf.reshape(1, d)
    return pl.pallas_call(
        _mix_mem_kernel,
        grid=(s // tm,),
        in_specs=[row(dn), row(dn), row(dn, Z_OFF // dn), row(mix - dn), const(dn_g), const(w_out), row(d),
                  const(gx), const(wq), const(k), const(v), const(wo), const(gf)],
        out_specs=[row(d), row(d)],
        out_shape=[jax.ShapeDtypeStruct((s, d), F32), jax.ShapeDtypeStruct((s, d), BF16)],
        compiler_params=_params("parallel"),
        name="mix_mem",
    )(o_f, o_b, proj, y_sw, dn_g, w_out, x, gx, wq, k, v, wo, gf)


def _glu_kernel(f_ref, wg_ref, wu_ref, o_ref):
    f = f_ref[...]
    a = _dot(f, wg_ref[...].astype(BF16))
    o_ref[...] = (a * jax.nn.sigmoid(a) * _dot(f, wu_ref[...].astype(BF16))).astype(o_ref.dtype)


def _ffn_glu(f, wg, wu, *, tm, tn):
    s, d = f.shape
    dff = wg.shape[1]
    return pl.pallas_call(
        _glu_kernel,
        grid=(s // tm, dff // tn),
        in_specs=[pl.BlockSpec((tm, d), lambda i, j: (i, 0)),
                  pl.BlockSpec((d, tn), lambda i, j: (0, j)),
                  pl.BlockSpec((d, tn), lambda i, j: (0, j))],
        out_specs=pl.BlockSpec((tm, tn), lambda i, j: (i, j)),
        out_shape=jax.ShapeDtypeStruct((s, dff), BF16),
        compiler_params=_params("parallel", "parallel"),
        name="ffn_glu",
    )(f, wg, wu)


def _down_kernel(a_ref, w_ref, h_ref, g_ref, o_ref, acc_ref):
    k = pl.program_id(1)

    @pl.when(k == 0)
    def _():
        acc_ref[...] = h_ref[...]

    acc_ref[...] += _dot(a_ref[...], w_ref[...])

    @pl.when(k == pl.num_programs(1) - 1)
    def _():
        o_ref[...] = _rms(acc_ref[...], g_ref[...])


def _ffn_down(act, wd, h, g, *, tm, tk):
    s, dff = act.shape
    d = wd.shape[1]
    return pl.pallas_call(
        _down_kernel,
        grid=(s // tm, dff // tk),
        in_specs=[pl.BlockSpec((tm, tk), lambda i, k: (i, k)),
                  pl.BlockSpec((tk, d), lambda i, k: (k, 0)),
                  pl.BlockSpec((tm, d), lambda i, k: (i, 0)),
                  pl.BlockSpec((1, d), lambda i, k: (0, 0))],
        out_specs=pl.BlockSpec((tm, d), lambda i, k: (i, 0)),
        out_shape=jax.ShapeDtypeStruct((s, d), F32),
        scratch_shapes=[pltpu.VMEM((tm, d), F32)],
        compiler_params=_params("parallel", "arbitrary"),
        name="ffn_down",
    )(act, wd, h, g.reshape(1, d))


def _gate_row(fwd, bwd):
    row = jnp.zeros((1, LANES), F32)
    row = row.at[0, GATE_G_F:GATE_G_F + DN_HEADS].set(fwd.astype(F32))
    return row.at[0, GATE_G_B:GATE_G_B + DN_HEADS].set(bwd.astype(F32))


def _pick(n, *cands):
    for c in cands:
        if n % c == 0:
            return c
    return n


def kernel(x, mem, norm_mix_g, w_in, conv_w, a_log_f, a_log_b, dt_bias_f, dt_bias_b, dn_norm_g, attn_sink, rel_bias, w_out, norm_x_g, norm_mem_g, w_q_mem, w_kv_mem, w_o_mem, norm_ffn_g, w_gate, w_up, w_down, norm_final_g):
    batch, s, d = x.shape
    assert batch == 1 and mem.shape[0] == 1 and w_in.shape[0] == 1, "single sequence, single layer"
    w = WINDOW
    rel = (jnp.arange(3 * w)[None, :] - w) - jnp.arange(w)[:, None]
    bucket = _t5_bucket(rel).astype(jnp.int32)
    mem_dim = MEM_HEADS * HEAD_DIM

    h = x.reshape(s, d)
    mem2 = mem.reshape(mem.shape[1], d)
    tm_big = _pick(s, 1024, 512, 256, 128)
    tm_mid = _pick(s, 512, 256, 128)
    w_r = _w_in_layout(w_in[0].T, cols=256)
    proj = _rms_matmul(h, norm_mix_g[0], w_r, tm=tm_big, tn=_pick(PROJ_WIDTH, 1920, 1152, 640, 128), name="in_proj",
                       w_transposed=True)

    q, k, v, gates, gates_t = _dn_prep(proj, conv_w[0], _gate_row(a_log_f[0], a_log_b[0]),
                                       _gate_row(dt_bias_f[0], dt_bias_b[0]), rows=_pick(s, 256, 128))
    o_f, o_b = _deltanet(q, k, v, gates, gates_t, rows=_pick(s, 256, 128), heads=8)
    y_sw = _swa(proj, bucket, rel_bias.astype(F32), attn_sink[0].astype(F32), qblk=_pick(s // w, 4, 2, 1))
    kv = _rms_matmul(mem2, norm_mem_g[0], w_kv_mem[0].astype(BF16), tm=mem2.shape[0], tn=2 * mem_dim,
                     name="mem_kv").astype(BF16)
    h, f = _mix_mem(o_f, o_b, proj, y_sw, dn_norm_g[0], w_out[0].astype(BF16), h, norm_x_g[0],
                    w_q_mem[0].astype(BF16), kv[:, :mem_dim], kv[:, mem_dim:], w_o_mem[0].astype(BF16),
                    norm_ffn_g[0], tm=_pick(s, 256, 128))

    act = _ffn_glu(f, w_gate[0], w_up[0], tm=tm_big, tn=512)
    out = _ffn_down(act, w_down[0].astype(BF16), h, norm_final_g, tm=tm_mid, tk=2816)
    return out.reshape(batch, s, d)
```

```python
import functools
import math

import jax
import jax.numpy as jnp
from jax import lax
from jax.experimental import pallas as pl
from jax.experimental.pallas import tpu as pltpu

F32 = jnp.float32
BF16 = jnp.bfloat16

RMS_EPS = 1e-6
L2_EPS = 1e-6
HEAD_DIM = 128
DN_HEADS = 8
DN_CHUNK = 64
DN_SUB = 16
CONV_WIDTH = 5
CONV_HALO = 8
SWA_HEADS = 8
SWA_KV_HEADS = 2
WINDOW = 128
NUM_BUCKETS = 32
MAX_DISTANCE = 128
MEM_HEADS = 4
LANES = 128
ROW_BATCH = 128

DN_QKV = 3 * DN_HEADS * HEAD_DIM
Z_OFF = DN_QKV
QSW_OFF = Z_OFF + DN_HEADS * HEAD_DIM
KSW_OFF = QSW_OFF + SWA_HEADS * HEAD_DIM
VSW_OFF = KSW_OFF + SWA_KV_HEADS * HEAD_DIM
GATE_OFF = VSW_OFF + SWA_KV_HEADS * HEAD_DIM
PROJ_WIDTH = GATE_OFF + LANES
GATE_BETA_F, GATE_BETA_B, GATE_G_F, GATE_G_B = 0, DN_HEADS, 2 * DN_HEADS, 3 * DN_HEADS
GATE_ROWS = 4 * DN_HEADS

VMEM_LIMIT_V7X = 56 * 1024 * 1024


def _params(*sem):
    return pltpu.CompilerParams(dimension_semantics=sem, vmem_limit_bytes=VMEM_LIMIT_V7X)


def _dot(a, b):
    return jnp.dot(a, b, preferred_element_type=F32)


def _dot_nt(a, b):
    return lax.dot_general(a, b, (((1,), (1,)), ((), ())), preferred_element_type=F32)


def _dot_tn(a, b):
    return lax.dot_general(a, b, (((0,), (0,)), ((), ())), preferred_element_type=F32)


def _rms(x, g):
    return x * lax.rsqrt(jnp.mean(x * x, axis=-1, keepdims=True) + RMS_EPS) * g


def _w_in_kernel(w_ref, o_ref):
    gate_lo = QSW_OFF
    o_ref[:gate_lo, :] = w_ref[:gate_lo, :].astype(o_ref.dtype)
    o_ref[gate_lo:GATE_OFF, :] = w_ref[gate_lo + GATE_ROWS:, :].astype(o_ref.dtype)
    o_ref[GATE_OFF:GATE_OFF + GATE_ROWS, :] = w_ref[gate_lo:gate_lo + GATE_ROWS, :].astype(o_ref.dtype)
    o_ref[GATE_OFF + GATE_ROWS:, :] = jnp.zeros((LANES - GATE_ROWS, o_ref.shape[1]), o_ref.dtype)


def _w_in_layout(w_t, *, cols):
    n, k = w_t.shape
    assert n == PROJ_WIDTH - (LANES - GATE_ROWS)
    return pl.pallas_call(
        _w_in_kernel,
        grid=(k // cols,),
        in_specs=[pl.BlockSpec((n, cols), lambda i: (0, i))],
        out_specs=pl.BlockSpec((PROJ_WIDTH, cols), lambda i: (0, i)),
        out_shape=jax.ShapeDtypeStruct((PROJ_WIDTH, k), BF16),
        compiler_params=_params("parallel"),
        name="w_in_layout",
    )(w_t)


def _rms_matmul_kernel(x_ref, g_ref, w_ref, o_ref, n_ref, *, w_transposed):
    dot = _dot_nt if w_transposed else _dot

    @pl.when(pl.program_id(1) == 0)
    def _():
        half = max(x_ref.shape[0] // 2, ROW_BATCH)
        for r in range(0, x_ref.shape[0], half):
            rows = slice(r, r + half)
            n = _rms(x_ref[rows, :], g_ref[...]).astype(n_ref.dtype)
            n_ref[rows, :] = n
            o_ref[rows, :] = dot(n, w_ref[...]).astype(o_ref.dtype)

    @pl.when(pl.program_id(1) != 0)
    def _():
        o_ref[...] = dot(n_ref[...], w_ref[...]).astype(o_ref.dtype)


def _rms_matmul(x, g, w, *, tm, tn, name, w_transposed=False):
    m, k = x.shape
    n = w.shape[0] if w_transposed else w.shape[1]
    w_spec = pl.BlockSpec((tn, k), lambda i, j: (j, 0)) if w_transposed else pl.BlockSpec((k, tn), lambda i, j: (0, j))
    return pl.pallas_call(
        functools.partial(_rms_matmul_kernel, w_transposed=w_transposed),
        grid=(m // tm, n // tn),
        in_specs=[pl.BlockSpec((tm, k), lambda i, j: (i, 0)),
                  pl.BlockSpec((1, k), lambda i, j: (0, 0)),
                  w_spec],
        out_specs=pl.BlockSpec((tm, tn), lambda i, j: (i, j)),
        out_shape=jax.ShapeDtypeStruct((m, n), F32),
        scratch_shapes=[pltpu.VMEM((tm, k), BF16)],
        compiler_params=_params("parallel", "arbitrary"),
        name=name,
    )(x, g.reshape(1, k), w)


def _split3(x):
    hi = x.astype(BF16)
    r = x - hi.astype(F32)
    mid = r.astype(BF16)
    lo = (r - mid.astype(F32)).astype(BF16)
    return hi, mid, lo


def _prep_kernel(main_ref, prev_ref, next_ref, gate_ref, cw_ref, alog_ref, dt_ref,
                 q_ref, k_ref, v_ref, go_ref, gt_ref, ext_ref):
    i = pl.program_id(0)
    rows = main_ref.shape[0]
    halo = CONV_HALO
    pad = (CONV_WIDTH - 1) // 2

    ext_ref[0:halo, :] = jnp.where(i > 0, prev_ref[...], 0.0)
    ext_ref[halo:halo + rows, :] = main_ref[...]
    ext_ref[halo + rows:, :] = jnp.where(i < pl.num_programs(0) - 1, next_ref[...], 0.0)

    for s in range(3 * DN_HEADS):
        cols = slice(s * HEAD_DIM, (s + 1) * HEAD_DIM)
        xe = ext_ref[:, cols]
        acc = cw_ref[pad:pad + 1, cols] * xe[halo:halo + rows]
        for j in range(CONV_WIDTH):
            if j != pad:
                shifted = pltpu.roll(xe, shift=(pad - j) % xe.shape[0], axis=0)[halo:halo + rows]
                acc = acc + cw_ref[j:j + 1, cols] * shifted
        y = acc * jax.nn.sigmoid(acc)
        if s < 2 * DN_HEADS:
            y = y * lax.rsqrt(jnp.sum(y * y, axis=-1, keepdims=True) + L2_EPS)
        if s < DN_HEADS:
            q_ref[:, cols] = y * (HEAD_DIM ** -0.5)
        elif s < 2 * DN_HEADS:
            k_ref[:, slice((s - DN_HEADS) * HEAD_DIM, (s - DN_HEADS + 1) * HEAD_DIM)] = y
        else:
            v_ref[:, slice((s - 2 * DN_HEADS) * HEAD_DIM, (s - 2 * DN_HEADS + 1) * HEAD_DIM)] = y

    t = gate_ref[...]
    beta = jax.nn.sigmoid(t)
    a = t + dt_ref[...]
    softplus = jnp.maximum(a, 0.0) + jnp.log1p(jnp.exp(-jnp.abs(a)))
    g = -jnp.exp(alog_ref[...]) * softplus

    ri = lax.broadcasted_iota(jnp.int32, (rows, rows), 0)
    ci = lax.broadcasted_iota(jnp.int32, (rows, rows), 1)
    shift = DN_CHUNK.bit_length() - 1
    same_chunk = (ri >> shift) == (ci >> shift)
    lower = jnp.where(same_chunk & (ci <= ri), 1.0, 0.0).astype(BF16)
    upper = jnp.where(same_chunk & (ci >= ri), 1.0, 0.0).astype(BF16)
    parts = _split3(g)
    gc_f = _dot(lower, parts[0]) + _dot(lower, parts[1]) + _dot(lower, parts[2])
    gc_b = _dot(upper, parts[0]) + _dot(upper, parts[1]) + _dot(upper, parts[2])

    col = lax.broadcasted_iota(jnp.int32, t.shape, 1)
    out = jnp.where(col < GATE_G_F, beta, jnp.where(col < GATE_G_B, gc_f, gc_b))
    go_ref[...] = out
    gt_ref[...] = out.T[0:GATE_ROWS, :]


def _dn_prep(proj, conv_w, alog_row, dt_row, *, rows):
    s = proj.shape[0]
    nblk = s // rows
    hb = rows // CONV_HALO
    last_halo = s // CONV_HALO - 1
    head_cols = DN_HEADS * HEAD_DIM
    return pl.pallas_call(
        _prep_kernel,
        grid=(nblk,),
        in_specs=[pl.BlockSpec((rows, DN_QKV), lambda i: (i, 0)),
                  pl.BlockSpec((CONV_HALO, DN_QKV), lambda i: (jnp.maximum(i * hb - 1, 0), 0)),
                  pl.BlockSpec((CONV_HALO, DN_QKV), lambda i: (jnp.minimum((i + 1) * hb, last_halo), 0)),
                  pl.BlockSpec((rows, LANES), lambda i: (i, GATE_OFF // LANES)),
                  pl.BlockSpec((CONV_WIDTH, DN_QKV), lambda i: (0, 0)),
                  pl.BlockSpec((1, LANES), lambda i: (0, 0)),
                  pl.BlockSpec((1, LANES), lambda i: (0, 0))],
        out_specs=[pl.BlockSpec((rows, head_cols), lambda i: (i, 0)),
                   pl.BlockSpec((rows, head_cols), lambda i: (i, 0)),
                   pl.BlockSpec((rows, head_cols), lambda i: (i, 0)),
                   pl.BlockSpec((rows, LANES), lambda i: (i, 0)),
                   pl.BlockSpec((GATE_ROWS, rows), lambda i: (0, i))],
        out_shape=[jax.ShapeDtypeStruct((s, head_cols), F32),
                   jax.ShapeDtypeStruct((s, head_cols), F32),
                   jax.ShapeDtypeStruct((s, head_cols), F32),
                   jax.ShapeDtypeStruct((s, LANES), F32),
                   jax.ShapeDtypeStruct((GATE_ROWS, s), F32)],
        scratch_shapes=[pltpu.VMEM((rows + 2 * CONV_HALO, DN_QKV), F32)],
        compiler_params=_params("parallel"),
        name="dn_prep",
    )(proj, proj, proj, proj, conv_w, alog_row, dt_row)


def _unit_tri_inverse(ms, eye, same_sub, _mm):
    c = eye.shape[0]
    mds = [jnp.where(same_sub, m, 0.0) for m in ms]
    es = [m - md for m, md in zip(ms, mds)]
    pw = [_mm(md, md) for md in mds]
    xs = [eye - md for md in mds]
    for _ in range(DN_SUB.bit_length() - 3):
        both = [_mm(jnp.concatenate([a, x], axis=0), a) for a, x in zip(pw, xs)]
        pw = [b[:c] for b in both]
        xs = [x + b[c:] for x, b in zip(xs, both)]
    xs = [x + _mm(x, a) for x, a in zip(xs, pw)]
    ns = [_mm(x, e) for x, e in zip(xs, es)]
    n2 = [_mm(n, n) for n in ns]
    ys = [_mm(n, (eye - n) + b) for n, b in zip(ns, n2)]
    return [x - _mm(y, x) for x, y in zip(xs, ys)]


def _deltanet_kernel(qf, kf, vf, gf, gtf, qb, kb, vb, gb, gtb, of_ref, ob_ref, state_ref):
    c_len = DN_CHUNK
    nchunk = qf.shape[0] // c_len
    heads = qf.shape[1] // HEAD_DIM
    head0 = pl.program_id(0) * heads

    @pl.when(pl.program_id(1) == 0)
    def _():
        state_ref[...] = jnp.zeros_like(state_ref)

    p_len = 2 * c_len
    ri = lax.broadcasted_iota(jnp.int32, (c_len, p_len), 0)
    cl = lax.broadcasted_iota(jnp.int32, (c_len, p_len), 1)
    ci = cl & (c_len - 1)
    first = cl < c_len
    eye = jnp.where(ri == ci, 1.0, 0.0)
    sub_shift = DN_SUB.bit_length() - 1
    same_sub = (ri >> sub_shift) == (ci >> sub_shift)
    lane = lax.broadcasted_iota(jnp.int32, (p_len, LANES), 1)
    upper_rows = lax.broadcasted_iota(jnp.int32, (p_len, 1), 0) >= c_len
    first_tall = lax.broadcasted_iota(jnp.int32, (c_len + HEAD_DIM, p_len), 1) < c_len
    masks = {False: (ci <= ri, ci < ri), True: (ci >= ri, ci > ri)}
    refs = {False: (qf, kf, vf, gf, gtf, of_ref), True: (qb, kb, vb, gb, gtb, ob_ref)}

    def pair_mm(x, p):
        blockdiag = jnp.concatenate([jnp.where(first, p, 0.0), jnp.where(first, 0.0, p)], axis=0)
        return _dot(x.astype(BF16), blockdiag.astype(BF16))

    chains = [(rev, hd) for rev in (False, True) for hd in range(heads)]
    inst = [(rev, hd, p) for rev, hd in chains for p in range(nchunk // 2)]

    g_rows = {}
    for rev, hd in chains:
        g_col = head0 + hd + (GATE_G_B if rev else GATE_G_F)
        g_rows[rev, hd] = refs[rev][4][pl.ds(g_col, 1), :]

    def load(rev, hd, p):
        q_ref, k_ref, v_ref, g_ref = refs[rev][:4]
        rows = slice(p * p_len, (p + 1) * p_len)
        cols = slice(hd * HEAD_DIM, (hd + 1) * HEAD_DIM)
        gates = g_ref[rows, :]
        beta_col = head0 + hd + (GATE_BETA_B if rev else GATE_BETA_F)
        g_col = head0 + hd + (GATE_G_B if rev else GATE_G_F)
        beta = jnp.sum(jnp.where(lane == beta_col, gates, 0.0), axis=1, keepdims=True)
        gcol = jnp.sum(jnp.where(lane == g_col, gates, 0.0), axis=1, keepdims=True)
        grow = g_rows[rev, hd][:, rows]
        if rev:
            glast = (grow[:, 0:1], grow[:, c_len:c_len + 1])
        else:
            glast = (grow[:, c_len - 1:c_len], grow[:, p_len - 1:p_len])
        return q_ref[rows, cols], k_ref[rows, cols], v_ref[rows, cols], beta, gcol, grow, glast

    lo, hi = slice(0, c_len), slice(c_len, p_len)

    data = [load(*i) for i in inst]
    kbeta = [k * beta for (_, k, _, beta, _, _, _) in data]
    zeros = jnp.zeros((c_len, HEAD_DIM), BF16)
    prod = []
    for kb_, (q, k, _, _, _, _, _) in zip(kbeta, data):
        lhs = jnp.concatenate([jnp.concatenate([kb_[lo], kb_[hi]], axis=1),
                               jnp.concatenate([q[lo], q[hi]], axis=1)], axis=0).astype(BF16)
        kb16 = k.astype(BF16)
        rhs = jnp.concatenate([jnp.concatenate([kb16[lo], zeros], axis=1),
                               jnp.concatenate([zeros, kb16[hi]], axis=1)], axis=0)
        prod.append(_dot_nt(lhs, rhs))
    decay = [jnp.exp(jnp.where(masks[rev][0], jnp.where(first, gcol[lo], gcol[hi]) - grow, -jnp.inf))
             for (rev, _, _), (_, _, _, _, gcol, grow, _) in zip(inst, data)]
    ms = [jnp.where(masks[rev][1], p[lo] * dec, 0.0) for (rev, _, _), p, dec in zip(inst, prod, decay)]
    a_mat = [p[hi] * dec for p, dec in zip(prod, decay)]
    ts = _unit_tri_inverse(ms, eye, same_sub, pair_mm)
    egs = [jnp.exp(gcol) for (_, _, _, _, gcol, _, _) in data]
    uw = []
    for t, kb_, eg, (_, _, v, beta, _, _, _) in zip(ts, kbeta, egs, data):
        lhs = jnp.concatenate([jnp.where(first, t, 0.0), jnp.where(first, 0.0, t)], axis=0).astype(BF16)
        rhs = jnp.concatenate([(v * beta).astype(BF16), (kb_ * eg).astype(BF16)], axis=1)
        uw.append(_dot(lhs, rhs))
    qdec = [q * eg for eg, (q, _, _, _, _, _, _) in zip(egs, data)]
    kdec_t = [(k * jnp.exp(jnp.where(upper_rows, glast[1], glast[0]) - gcol)).T
              for (_, k, _, _, gcol, _, glast) in data]
    tall = [jnp.concatenate([a_, kt], axis=0) for a_, kt in zip(a_mat, kdec_t)]
    pre = {}
    for (rev, hd, p), x, qd, tl, (_, _, _, _, _, _, glast) in zip(inst, uw, qdec, tall, data):
        for half, rs in enumerate((lo, hi)):
            u = x[rs, :HEAD_DIM]
            w_ = x[rs, HEAD_DIM:]
            wq = jnp.concatenate([w_, qd[rs]], axis=0).astype(BF16)
            keep = first_tall if half == 0 else ~first_tall
            pre[rev, hd, 2 * p + half] = (u, wq, jnp.where(keep, tl, 0.0).astype(BF16), jnp.exp(glast[half]))

    order = {False: list(range(nchunk)), True: list(range(nchunk - 1, -1, -1))}
    states = [state_ref[n] for n in range(len(chains))]
    for t in range(nchunk):
        cur = [pre[rev, hd, order[rev][t]] for rev, hd in chains]
        ws = [_dot(wq, s_.astype(BF16)) for (_, wq, _, _), s_ in zip(cur, states)]
        v_new = [(u - x[lo]).astype(BF16) for (u, _, _, _), x in zip(cur, ws)]
        upd = [_dot(tl, jnp.concatenate([vn, vn], axis=0)) for (_, _, tl, _), vn in zip(cur, v_new)]
        states = [s_ * g_ + x[c_len:] for (_, _, _, g_), s_, x in zip(cur, states, upd)]
        for (rev, hd), x, y in zip(chains, ws, upd):
            c = order[rev][t]
            refs[rev][5][c * c_len:(c + 1) * c_len, hd * HEAD_DIM:(hd + 1) * HEAD_DIM] = x[hi] + y[lo]
    for n, s_ in enumerate(states):
        state_ref[n] = s_


def _deltanet(q, k, v, gates, gates_t, *, rows, heads):
    s = q.shape[0]
    nb = s // rows
    fwd = lambda h, b: (b, h)
    bwd = lambda h, b: (nb - 1 - b, h)
    head = lambda im: pl.BlockSpec((rows, heads * HEAD_DIM), im)
    gate = lambda im: pl.BlockSpec((rows, LANES), lambda h, b: (im(h, b)[0], 0))
    gate_t = lambda im: pl.BlockSpec((GATE_ROWS, rows), lambda h, b: (0, im(h, b)[0]))
    return pl.pallas_call(
        _deltanet_kernel,
        grid=(DN_HEADS // heads, nb),
        in_specs=[head(fwd), head(fwd), head(fwd), gate(fwd), gate_t(fwd),
                  head(bwd), head(bwd), head(bwd), gate(bwd), gate_t(bwd)],
        out_specs=[head(fwd), head(bwd)],
        out_shape=[jax.ShapeDtypeStruct(q.shape, F32), jax.ShapeDtypeStruct(q.shape, F32)],
        scratch_shapes=[pltpu.VMEM((2 * heads, HEAD_DIM, HEAD_DIM), F32)],
        compiler_params=_params("parallel", "arbitrary"),
        name="deltanet",
    )(q, k, v, gates, gates_t, q, k, v, gates, gates_t)


def _t5_bucket(rel):
    nb = NUM_BUCKETS // 2
    max_exact = nb // 2
    n = jnp.abs(rel)
    large = max_exact + (jnp.log(jnp.maximum(n, max_exact).astype(F32) / max_exact)
                         / math.log(MAX_DISTANCE / max_exact) * (nb - max_exact)).astype(jnp.int32)
    large = jnp.minimum(large, nb - 1)
    return jnp.where(rel > 0, nb, 0) + jnp.where(n < max_exact, n, large)


def _swa_kernel(q_ref, kp_ref, kc_ref, kn_ref, vp_ref, vc_ref, vn_ref, bucket_ref, rb_ref, sink_ref,
                o_ref, bias_ref, *, seq):
    n = pl.program_id(0)
    w = WINDOW
    qblk = q_ref.shape[0] // w
    group = SWA_HEADS // SWA_KV_HEADS
    ri = lax.broadcasted_iota(jnp.int32, (w, 3 * w), 0)
    ci = lax.broadcasted_iota(jnp.int32, (w, 3 * w), 1)

    @pl.when(n == 0)
    def _():
        bucket = bucket_ref[...]
        in_band = jnp.abs(ci - w - ri) <= w
        for hd in range(SWA_HEADS):
            acc = jnp.zeros((w, 3 * w), F32)
            for b in range(NUM_BUCKETS):
                acc = jnp.where(bucket == b, rb_ref[b, hd], acc)
            bias_ref[hd] = jnp.where(in_band, acc, -jnp.inf)

    key_pos = (n * qblk - 1) * w + ci
    kext, vext = [], []
    for kvh in range(SWA_KV_HEADS):
        kcols = slice(kvh * HEAD_DIM, (kvh + 1) * HEAD_DIM)
        kext.append(jnp.concatenate([kp_ref[:, kcols], kc_ref[:, kcols], kn_ref[:, kcols]], axis=0).astype(BF16))
        vext.append(jnp.concatenate([vp_ref[:, kcols], vc_ref[:, kcols], vn_ref[:, kcols]], axis=0).astype(BF16))
    heads = range(SWA_HEADS)
    sinks = [sink_ref[hd] for hd in heads]
    for t in range(qblk):
        rows = slice(t * w, (t + 1) * w)
        band = slice(t * w, (t + 3) * w)
        s = [_dot_nt(q_ref[rows, hd * HEAD_DIM:(hd + 1) * HEAD_DIM].astype(BF16), kext[hd // group][band])
             * (HEAD_DIM ** -0.5) + bias_ref[hd] for hd in heads]
        if t == 0:
            s = [jnp.where(key_pos >= 0, x, -jnp.inf) for x in s]
        if t == qblk - 1:
            s = [jnp.where(key_pos + t * w < seq, x, -jnp.inf) for x in s]
        mx = [jnp.maximum(jnp.max(x, axis=1, keepdims=True), sk) for x, sk in zip(s, sinks)]
        p = [jnp.exp(x - m) for x, m in zip(s, mx)]
        den = [jnp.sum(x, axis=1, keepdims=True) + jnp.exp(sk - m) for x, sk, m in zip(p, sinks, mx)]
        o = [_dot(x.astype(BF16), vext[hd // group][band]) / dn for hd, x, dn in zip(heads, p, den)]
        for hd, x in zip(heads, o):
            o_ref[rows, hd * HEAD_DIM:(hd + 1) * HEAD_DIM] = x.astype(o_ref.dtype)


def _swa(proj, bucket, rel_bias, sink, *, qblk):
    s = proj.shape[0]
    w = WINDOW
    rows = qblk * w
    nb = s // w
    qw = SWA_HEADS * HEAD_DIM
    kvw = SWA_KV_HEADS * HEAD_DIM
    main = lambda off: pl.BlockSpec((rows, kvw), lambda n: (n, off // kvw))
    halo = lambda off, lo: pl.BlockSpec(
        (w, kvw), lambda n: (jnp.clip(n * qblk - 1 if lo else (n + 1) * qblk, 0, nb - 1), off // kvw))
    return pl.pallas_call(
        functools.partial(_swa_kernel, seq=s),
        grid=(s // rows,),
        in_specs=[pl.BlockSpec((rows, qw), lambda n: (n, QSW_OFF // qw)),
                  halo(KSW_OFF, True), main(KSW_OFF), halo(KSW_OFF, False),
                  halo(VSW_OFF, True), main(VSW_OFF), halo(VSW_OFF, False),
                  pl.BlockSpec((w, 3 * w), lambda n: (0, 0)),
                  pl.BlockSpec(memory_space=pltpu.SMEM),
                  pl.BlockSpec(memory_space=pltpu.SMEM)],
        out_specs=pl.BlockSpec((rows, qw), lambda n: (n, 0)),
        out_shape=jax.ShapeDtypeStruct((s, qw), BF16),
        scratch_shapes=[pltpu.VMEM((SWA_HEADS, w, 3 * w), F32)],
        compiler_params=_params("arbitrary"),
        name="swa",
    )(proj, proj, proj, proj, proj, proj, proj, bucket, rel_bias, sink)


def _out_proj_kernel(of_ref, ob_ref, z_ref, sw_ref, g_ref, w_ref, x_ref, o_ref):
    for r in range(0, o_ref.shape[0], ROW_BATCH):
        rows = slice(r, r + ROW_BATCH)
        parts = []
        for hd in range(DN_HEADS):
            cols = slice(hd * HEAD_DIM, (hd + 1) * HEAD_DIM)
            z = z_ref[rows, cols]
            y = _rms(of_ref[rows, cols] + ob_ref[rows, cols], g_ref[...]) * (z * jax.nn.sigmoid(z))
            parts.append(y.astype(BF16))
        parts.append(sw_ref[rows, :])
        o_ref[rows, :] = x_ref[rows, :] + _dot(jnp.concatenate(parts, axis=1), w_ref[...])


def _out_proj(o_f, o_b, proj, y_sw, dn_g, w_out, x, *, tm):
    s, d = x.shape
    dn = DN_HEADS * HEAD_DIM
    mix = w_out.shape[0]
    return pl.pallas_call(
        _out_proj_kernel,
        grid=(s // tm,),
        in_specs=[pl.BlockSpec((tm, dn), lambda i: (i, 0)),
                  pl.BlockSpec((tm, dn), lambda i: (i, 0)),
                  pl.BlockSpec((tm, dn), lambda i: (i, Z_OFF // dn)),
                  pl.BlockSpec((tm, mix - dn), lambda i: (i, 0)),
                  pl.BlockSpec((1, HEAD_DIM), lambda i: (0, 0)),
                  pl.BlockSpec((mix, d), lambda i: (0, 0)),
                  pl.BlockSpec((tm, d), lambda i: (i, 0))],
        out_specs=pl.BlockSpec((tm, d), lambda i: (i, 0)),
        out_shape=jax.ShapeDtypeStruct((s, d), F32),
        compiler_params=_params("parallel"),
        name="out_proj",
    )(o_f, o_b, proj, y_sw, dn_g.reshape(1, HEAD_DIM), w_out, x)


def _mem_attn_kernel(h_ref, gx_ref, wq_ref, k_ref, v_ref, wo_ref, gf_ref, h2_ref, f_ref):
    h = h_ref[...]
    q = _dot(_rms(h, gx_ref[...]).astype(BF16), wq_ref[...])
    heads = []
    for hd in range(MEM_HEADS):
        cols = slice(hd * HEAD_DIM, (hd + 1) * HEAD_DIM)
        s = _dot_nt(q[:, cols].astype(BF16), k_ref[:, cols]) * (HEAD_DIM ** -0.5)
        p = jnp.exp(s - jnp.max(s, axis=1, keepdims=True))
        den = jnp.sum(p, axis=1, keepdims=True)
        heads.append((_dot(p.astype(BF16), v_ref[:, cols]) / den).astype(BF16))
    h2 = h + _dot(jnp.concatenate(heads, axis=1), wo_ref[...])
    h2_ref[...] = h2
    f_ref[...] = _rms(h2, gf_ref[...]).astype(f_ref.dtype)


def _mem_attn(h, gx, wq, k, v, wo, gf, *, tm):
    s, d = h.shape
    full = lambda a: pl.BlockSpec(a.shape, lambda i: (0, 0))
    gx, gf = gx.reshape(1, d), gf.reshape(1, d)
    return pl.pallas_call(
        _mem_attn_kernel,
        grid=(s // tm,),
        in_specs=[pl.BlockSpec((tm, d), lambda i: (i, 0)), full(gx), full(wq), full(k), full(v), full(wo),
                  full(gf)],
        out_specs=[pl.BlockSpec((tm, d), lambda i: (i, 0)), pl.BlockSpec((tm, d), lambda i: (i, 0))],
        out_shape=[jax.ShapeDtypeStruct((s, d), F32), jax.ShapeDtypeStruct((s, d), BF16)],
        compiler_params=_params("parallel"),
        name="mem_attn",
    )(h, gx, wq, k, v, wo, gf)


def _mix_mem_kernel(of_ref, ob_ref, z_ref, sw_ref, dng_ref, wout_ref, x_ref, gx_ref, wq_ref, k_ref, v_ref, wo_ref,
                    gf_ref, h2_ref, f_ref):
    batches = [slice(r, r + ROW_BATCH) for r in range(0, h2_ref.shape[0], ROW_BATCH)]
    scale = HEAD_DIM ** -0.5
    lhs = []
    for rows in batches:
        parts = []
        for hd in range(DN_HEADS):
            cols = slice(hd * HEAD_DIM, (hd + 1) * HEAD_DIM)
            z = z_ref[rows, cols]
            y = _rms(of_ref[rows, cols] + ob_ref[rows, cols], dng_ref[...]) * (z * jax.nn.sigmoid(z))
            parts.append(y.astype(BF16))
        parts.append(sw_ref[rows, :])
        lhs.append(jnp.concatenate(parts, axis=1))
    h1 = [x_ref[rows, :] + _dot(a, wout_ref[...]) for rows, a in zip(batches, lhs)]
    q = [_dot(_rms(h, gx_ref[...]).astype(BF16), wq_ref[...]) for h in h1]
    att = []
    for qb in q:
        heads = []
        for hd in range(MEM_HEADS):
            cols = slice(hd * HEAD_DIM, (hd + 1) * HEAD_DIM)
            s = _dot_nt(qb[:, cols].astype(BF16), k_ref[:, cols]) * scale
            p = jnp.exp(s - jnp.max(s, axis=1, keepdims=True))
            den = jnp.sum(p, axis=1, keepdims=True)
            heads.append((_dot(p.astype(BF16), v_ref[:, cols]) / den).astype(BF16))
        att.append(jnp.concatenate(heads, axis=1))
    h2 = [h + _dot(a, wo_ref[...]) for h, a in zip(h1, att)]
    for rows, h in zip(batches, h2):
        h2_ref[rows, :] = h
        f_ref[rows, :] = _rms(h, gf_ref[...]).astype(f_ref.dtype)


def _mix_mem(o_f, o_b, proj, y_sw, dn_g, w_out, x, gx, wq, k, v, wo, gf, *, tm):
    s, d = x.shape
    dn = DN_HEADS * HEAD_DIM
    mix = w_out.shape[0]
    row = lambda width, col=0: pl.BlockSpec((tm, width), lambda i: (i, col))
    const = lambda a: pl.BlockSpec(a.shape, lambda i: (0, 0), pipeline_mode=pl.Buffered(1))
    dn_g, gx, gf = dn_g.reshape(1, HEAD_DIM), gx.reshape(1, d), gf.reshape(1, d)
    return pl.pallas_call(
        _mix_mem_kernel,
        grid=(s // tm,),
        in_specs=[row(dn), row(dn), row(dn, Z_OFF // dn), row(mix - dn), const(dn_g), const(w_out), row(d),
                  const(gx), const(wq), const(k), const(v), const(wo), const(gf)],
        out_specs=[row(d), row(d)],
        out_shape=[jax.ShapeDtypeStruct((s, d), F32), jax.ShapeDtypeStruct((s, d), BF16)],
        compiler_params=_params("parallel"),
        name="mix_mem",
    )(o_f, o_b, proj, y_sw, dn_g, w_out, x, gx, wq, k, v, wo, gf)


def _glu_kernel(f_ref, wg_ref, wu_ref, o_ref):
    f = f_ref[...]
    a = _dot(f, wg_ref[...].astype(BF16))
    o_ref[...] = (a * jax.nn.sigmoid(a) * _dot(f, wu_ref[...].astype(BF16))).astype(o_ref.dtype)


def _ffn_glu(f, wg, wu, *, tm, tn):
    s, d = f.shape
    dff = wg.shape[1]
    return pl.pallas_call(
        _glu_kernel,
        grid=(s // tm, dff // tn),
        in_specs=[pl.BlockSpec((tm, d), lambda i, j: (i, 0)),
                  pl.BlockSpec((d, tn), lambda i, j: (0, j)),
                  pl.BlockSpec((d, tn), lambda i, j: (0, j))],
        out_specs=pl.BlockSpec((tm, tn), lambda i, j: (i, j)),
        out_shape=jax.ShapeDtypeStruct((s, dff), BF16),
        compiler_params=_params("parallel", "parallel"),
        name="ffn_glu",
    )(f, wg, wu)


def _down_kernel(a_ref, w_ref, h_ref, g_ref, o_ref, acc_ref):
    k = pl.program_id(1)

    @pl.when(k == 0)
    def _():
        acc_ref[...] = h_ref[...]

    acc_ref[...] += _dot(a_ref[...], w_ref[...])

    @pl.when(k == pl.num_programs(1) - 1)
    def _():
        o_ref[...] = _rms(acc_ref[...], g_ref[...])


def _ffn_down(act, wd, h, g, *, tm, tk):
    s, dff = act.shape
    d = wd.shape[1]
    return pl.pallas_call(
        _down_kernel,
        grid=(s // tm, dff // tk),
        in_specs=[pl.BlockSpec((tm, tk), lambda i, k: (i, k)),
                  pl.BlockSpec((tk, d), lambda i, k: (k, 0)),
                  pl.BlockSpec((tm, d), lambda i, k: (i, 0)),
                  pl.BlockSpec((1, d), lambda i, k: (0, 0))],
        out_specs=pl.BlockSpec((tm, d), lambda i, k: (i, 0)),
        out_shape=jax.ShapeDtypeStruct((s, d), F32),
        scratch_shapes=[pltpu.VMEM((tm, d), F32)],
        compiler_params=_params("parallel", "arbitrary"),
        name="ffn_down",
    )(act, wd, h, g.reshape(1, d))


def _gate_row(fwd, bwd):
    row = jnp.zeros((1, LANES), F32)
    row = row.at[0, GATE_G_F:GATE_G_F + DN_HEADS].set(fwd.astype(F32))
    return row.at[0, GATE_G_B:GATE_G_B + DN_HEADS].set(bwd.astype(F32))


def _pick(n, *cands):
    for c in cands:
        if n % c == 0:
            return c
    return n


def kernel(x, mem, norm_mix_g, w_in, conv_w, a_log_f, a_log_b, dt_bias_f, dt_bias_b, dn_norm_g, attn_sink, rel_bias, w_out, norm_x_g, norm_mem_g, w_q_mem, w_kv_mem, w_o_mem, norm_ffn_g, w_gate, w_up, w_down, norm_final_g):
    batch, s, d = x.shape
    assert batch == 1 and mem.shape[0] == 1 and w_in.shape[0] == 1, "single sequence, single layer"
    w = WINDOW
    rel = (jnp.arange(3 * w)[None, :] - w) - jnp.arange(w)[:, None]
    bucket = _t5_bucket(rel).astype(jnp.int32)
    mem_dim = MEM_HEADS * HEAD_DIM

    h = x.reshape(s, d)
    mem2 = mem.reshape(mem.shape[1], d)
    tm_big = _pick(s, 1024, 512, 256, 128)
    tm_mid = _pick(s, 512, 256, 128)
    w_r = _w_in_layout(w_in[0].T, cols=256)
    proj = _rms_matmul(h, norm_mix_g[0], w_r, tm=tm_big, tn=_pick(PROJ_WIDTH, 1920, 1152, 640, 128), name="in_proj",
                       w_transposed=True)

    q, k, v, gates, gates_t = _dn_prep(proj, conv_w[0], _gate_row(a_log_f[0], a_log_b[0]),
                                       _gate_row(dt_bias_f[0], dt_bias_b[0]), rows=_pick(s, 256, 128))
    o_f, o_b = _deltanet(q, k, v, gates, gates_t, rows=_pick(s, 256, 128), heads=8)
    y_sw = _swa(proj, bucket, rel_bias.astype(F32), attn_sink[0].astype(F32), qblk=_pick(s // w, 4, 2, 1))
    kv = _rms_matmul(mem2, norm_mem_g[0], w_kv_mem[0].astype(BF16), tm=mem2.shape[0], tn=2 * mem_dim,
                     name="mem_kv").astype(BF16)
    h, f = _mix_mem(o_f, o_b, proj, y_sw, dn_norm_g[0], w_out[0].astype(BF16), h, norm_x_g[0],
                    w_q_mem[0].astype(BF16), kv[:, :mem_dim], kv[:, mem_dim:], w_o_mem[0].astype(BF16),
                    norm_ffn_g[0], tm=_pick(s, 256, 128))

    act = _ffn_glu(f, w_gate[0], w_up[0], tm=tm_big, tn=512)
    out = _ffn_down(act, w_down[0].astype(BF16), h, norm_final_g, tm=tm_mid, tk=2816)
    return out.reshape(batch, s, d)
```

```python
import functools
import math

import jax
import jax.numpy as jnp
from jax import lax
from jax.experimental import pallas as pl
from jax.experimental.pallas import tpu as pltpu

F32 = jnp.float32
BF16 = jnp.bfloat16

RMS_EPS = 1e-6
L2_EPS = 1e-6
HEAD_DIM = 128
DN_HEADS = 8
DN_CHUNK = 64
DN_SUB = 16
CONV_WIDTH = 5
CONV_HALO = 8
SWA_HEADS = 8
SWA_KV_HEADS = 2
WINDOW = 128
NUM_BUCKETS = 32
MAX_DISTANCE = 128
MEM_HEADS = 4
LOG2E = 1.4426950408889634
LANES = 128
ROW_BATCH = 128

DN_QKV = 3 * DN_HEADS * HEAD_DIM
Z_OFF = DN_QKV
QSW_OFF = Z_OFF + DN_HEADS * HEAD_DIM
KSW_OFF = QSW_OFF + SWA_HEADS * HEAD_DIM
VSW_OFF = KSW_OFF + SWA_KV_HEADS * HEAD_DIM
GATE_OFF = VSW_OFF + SWA_KV_HEADS * HEAD_DIM
PROJ_WIDTH = GATE_OFF + LANES
GATE_BETA_F, GATE_BETA_B, GATE_G_F, GATE_G_B = 0, DN_HEADS, 2 * DN_HEADS, 3 * DN_HEADS
GATE_ROWS = 4 * DN_HEADS

VMEM_LIMIT_V7X = 56 * 1024 * 1024


def _params(*sem):
    return pltpu.CompilerParams(dimension_semantics=sem, vmem_limit_bytes=VMEM_LIMIT_V7X)


def _steps(dim, tile):
    assert dim % tile == 0, f"tile {tile} does not divide {dim}"
    return dim // tile


def _dot(a, b):
    return jnp.dot(a, b, preferred_element_type=F32)


def _dot_nt(a, b):
    return lax.dot_general(a, b, (((1,), (1,)), ((), ())), preferred_element_type=F32)


def _dot_tn(a, b):
    return lax.dot_general(a, b, (((0,), (0,)), ((), ())), preferred_element_type=F32)


def _rms(x, g):
    return x * lax.rsqrt(jnp.mean(x * x, axis=-1, keepdims=True) + RMS_EPS) * g


def _w_in_kernel(w_ref, o_ref):
    gate_lo = QSW_OFF
    o_ref[:gate_lo, :] = w_ref[:gate_lo, :].astype(o_ref.dtype)
    o_ref[gate_lo:GATE_OFF, :] = w_ref[gate_lo + GATE_ROWS:, :].astype(o_ref.dtype)
    o_ref[GATE_OFF:GATE_OFF + GATE_ROWS, :] = w_ref[gate_lo:gate_lo + GATE_ROWS, :].astype(o_ref.dtype)
    o_ref[GATE_OFF + GATE_ROWS:, :] = jnp.zeros((LANES - GATE_ROWS, o_ref.shape[1]), o_ref.dtype)


def _w_in_layout(w_t, *, cols):
    n, k = w_t.shape
    assert n == PROJ_WIDTH - (LANES - GATE_ROWS)
    return pl.pallas_call(
        _w_in_kernel,
        grid=(_steps(k, cols),),
        in_specs=[pl.BlockSpec((n, cols), lambda i: (0, i))],
        out_specs=pl.BlockSpec((PROJ_WIDTH, cols), lambda i: (0, i)),
        out_shape=jax.ShapeDtypeStruct((PROJ_WIDTH, k), BF16),
        compiler_params=_params("parallel"),
        name="w_in_layout",
    )(w_t)


def _rms_matmul_kernel(x_ref, g_ref, w_ref, o_ref, n_ref, *, w_transposed):
    dot = _dot_nt if w_transposed else _dot

    @pl.when(pl.program_id(1) == 0)
    def _():
        half = max(x_ref.shape[0] // 2, ROW_BATCH)
        for r in range(0, x_ref.shape[0], half):
            rows = slice(r, r + half)
            n = _rms(x_ref[rows, :], g_ref[...]).astype(n_ref.dtype)
            n_ref[rows, :] = n
            o_ref[rows, :] = dot(n, w_ref[...]).astype(o_ref.dtype)

    @pl.when(pl.program_id(1) != 0)
    def _():
        o_ref[...] = dot(n_ref[...], w_ref[...]).astype(o_ref.dtype)


def _rms_matmul(x, g, w, *, tm, tn, name, w_transposed=False):
    m, k = x.shape
    n = w.shape[0] if w_transposed else w.shape[1]
    w_spec = pl.BlockSpec((tn, k), lambda i, j: (j, 0)) if w_transposed else pl.BlockSpec((k, tn), lambda i, j: (0, j))
    return pl.pallas_call(
        functools.partial(_rms_matmul_kernel, w_transposed=w_transposed),
        grid=(_steps(m, tm), _steps(n, tn)),
        in_specs=[pl.BlockSpec((tm, k), lambda i, j: (i, 0)),
                  pl.BlockSpec((1, k), lambda i, j: (0, 0)),
                  w_spec],
        out_specs=pl.BlockSpec((tm, tn), lambda i, j: (i, j)),
        out_shape=jax.ShapeDtypeStruct((m, n), F32),
        scratch_shapes=[pltpu.VMEM((tm, k), BF16)],
        compiler_params=_params("parallel", "arbitrary"),
        name=name,
    )(x, g.reshape(1, k), w)


def _split3(x):
    hi = x.astype(BF16)
    r = x - hi.astype(F32)
    mid = r.astype(BF16)
    lo = (r - mid.astype(F32)).astype(BF16)
    return hi, mid, lo


def _prep_kernel(main_ref, prev_ref, next_ref, gate_ref, cw_ref, alog_ref, dt_ref,
                 q_ref, k_ref, v_ref, go_ref, gt_ref, ext_ref):
    i = pl.program_id(0)
    rows = main_ref.shape[0]
    halo = CONV_HALO
    pad = (CONV_WIDTH - 1) // 2

    ext_ref[0:halo, :] = jnp.where(i > 0, prev_ref[...], 0.0)
    ext_ref[halo:halo + rows, :] = main_ref[...]
    ext_ref[halo + rows:, :] = jnp.where(i < pl.num_programs(0) - 1, next_ref[...], 0.0)

    for s in range(3 * DN_HEADS):
        cols = slice(s * HEAD_DIM, (s + 1) * HEAD_DIM)
        xe = ext_ref[:, cols]
        acc = cw_ref[pad:pad + 1, cols] * xe[halo:halo + rows]
        for j in range(CONV_WIDTH):
            if j != pad:
                shifted = pltpu.roll(xe, shift=(pad - j) % xe.shape[0], axis=0)[halo:halo + rows]
                acc = acc + cw_ref[j:j + 1, cols] * shifted
        y = acc * jax.nn.sigmoid(acc)
        if s < 2 * DN_HEADS:
            inv = lax.rsqrt(jnp.sum(y * y, axis=-1, keepdims=True) + L2_EPS)
            y = y * (inv * (HEAD_DIM ** -0.5) if s < DN_HEADS else inv)
        if s < DN_HEADS:
            q_ref[:, cols] = y
        elif s < 2 * DN_HEADS:
            k_ref[:, slice((s - DN_HEADS) * HEAD_DIM, (s - DN_HEADS + 1) * HEAD_DIM)] = y
        else:
            v_ref[:, slice((s - 2 * DN_HEADS) * HEAD_DIM, (s - 2 * DN_HEADS + 1) * HEAD_DIM)] = y

    t = gate_ref[...]
    beta = jax.nn.sigmoid(t)
    a = t + dt_ref[...]
    softplus = jnp.maximum(a, 0.0) + jnp.log1p(jnp.exp(-jnp.abs(a)))
    g = -jnp.exp(alog_ref[...]) * softplus

    ri = lax.broadcasted_iota(jnp.int32, (rows, rows), 0)
    ci = lax.broadcasted_iota(jnp.int32, (rows, rows), 1)
    shift = DN_CHUNK.bit_length() - 1
    same_chunk = (ri >> shift) == (ci >> shift)
    lower = jnp.where(same_chunk & (ci <= ri), 1.0, 0.0).astype(BF16)
    upper = jnp.where(same_chunk & (ci >= ri), 1.0, 0.0).astype(BF16)
    parts = _split3(g)
    gc_f = _dot(lower, parts[0]) + _dot(lower, parts[1]) + _dot(lower, parts[2])
    gc_b = _dot(upper, parts[0]) + _dot(upper, parts[1]) + _dot(upper, parts[2])

    col = lax.broadcasted_iota(jnp.int32, t.shape, 1)
    out = jnp.where(col < GATE_G_F, beta, jnp.where(col < GATE_G_B, gc_f, gc_b))
    go_ref[...] = out
    gt_ref[...] = out.T[0:GATE_ROWS, :]


def _dn_prep(proj, conv_w, alog_row, dt_row, *, rows):
    s = proj.shape[0]
    nblk = _steps(s, rows)
    hb = rows // CONV_HALO
    last_halo = s // CONV_HALO - 1
    head_cols = DN_HEADS * HEAD_DIM
    return pl.pallas_call(
        _prep_kernel,
        grid=(nblk,),
        in_specs=[pl.BlockSpec((rows, DN_QKV), lambda i: (i, 0)),
                  pl.BlockSpec((CONV_HALO, DN_QKV), lambda i: (jnp.maximum(i * hb - 1, 0), 0)),
                  pl.BlockSpec((CONV_HALO, DN_QKV), lambda i: (jnp.minimum((i + 1) * hb, last_halo), 0)),
                  pl.BlockSpec((rows, LANES), lambda i: (i, GATE_OFF // LANES)),
                  pl.BlockSpec((CONV_WIDTH, DN_QKV), lambda i: (0, 0)),
                  pl.BlockSpec((1, LANES), lambda i: (0, 0)),
                  pl.BlockSpec((1, LANES), lambda i: (0, 0))],
        out_specs=[pl.BlockSpec((rows, head_cols), lambda i: (i, 0)),
                   pl.BlockSpec((rows, head_cols), lambda i: (i, 0)),
                   pl.BlockSpec((rows, head_cols), lambda i: (i, 0)),
                   pl.BlockSpec((rows, LANES), lambda i: (i, 0)),
                   pl.BlockSpec((GATE_ROWS, rows), lambda i: (0, i))],
        out_shape=[jax.ShapeDtypeStruct((s, head_cols), F32),
                   jax.ShapeDtypeStruct((s, head_cols), F32),
                   jax.ShapeDtypeStruct((s, head_cols), F32),
                   jax.ShapeDtypeStruct((s, LANES), F32),
                   jax.ShapeDtypeStruct((GATE_ROWS, s), F32)],
        scratch_shapes=[pltpu.VMEM((rows + 2 * CONV_HALO, DN_QKV), F32)],
        compiler_params=_params("parallel"),
        name="dn_prep",
    )(proj, proj, proj, proj, conv_w, alog_row, dt_row)


def _unit_tri_inverse(ms, eye, same_sub, _mm):
    c = eye.shape[0]
    mds = [jnp.where(same_sub, m, 0.0) for m in ms]
    es = [m - md for m, md in zip(ms, mds)]
    pw = [_mm(md, md) for md in mds]
    xs = [eye - md for md in mds]
    for _ in range(DN_SUB.bit_length() - 3):
        both = [_mm(jnp.concatenate([a, x], axis=0), a) for a, x in zip(pw, xs)]
        pw = [b[:c] for b in both]
        xs = [x + b[c:] for x, b in zip(xs, both)]
    xs = [x + _mm(x, a) for x, a in zip(xs, pw)]
    ns = [_mm(x, e) for x, e in zip(xs, es)]
    n2 = [_mm(n, n) for n in ns]
    ys = [_mm(n, (eye - n) + b) for n, b in zip(ns, n2)]
    return [x - _mm(y, x) for x, y in zip(xs, ys)]


def _deltanet_kernel(qf, kf, vf, gf, gtf, qb, kb, vb, gb, gtb, of_ref, ob_ref, state_ref):
    c_len = DN_CHUNK
    nchunk = qf.shape[0] // c_len
    heads = qf.shape[1] // HEAD_DIM
    head0 = pl.program_id(0) * heads

    @pl.when(pl.program_id(1) == 0)
    def _():
        state_ref[...] = jnp.zeros_like(state_ref)

    p_len = 2 * c_len
    ri = lax.broadcasted_iota(jnp.int32, (c_len, p_len), 0)
    cl = lax.broadcasted_iota(jnp.int32, (c_len, p_len), 1)
    ci = cl & (c_len - 1)
    first = cl < c_len
    eye = jnp.where(ri == ci, 1.0, 0.0)
    sub_shift = DN_SUB.bit_length() - 1
    same_sub = (ri >> sub_shift) == (ci >> sub_shift)
    lane = lax.broadcasted_iota(jnp.int32, (p_len, LANES), 1)
    upper_rows = lax.broadcasted_iota(jnp.int32, (p_len, 1), 0) >= c_len
    first_tall = lax.broadcasted_iota(jnp.int32, (c_len + HEAD_DIM, p_len), 1) < c_len
    masks = {False: (ci <= ri, ci < ri), True: (ci >= ri, ci > ri)}
    refs = {False: (qf, kf, vf, gf, gtf, of_ref), True: (qb, kb, vb, gb, gtb, ob_ref)}

    def pair_mm(x, p):
        blockdiag = jnp.concatenate([jnp.where(first, p, 0.0), jnp.where(first, 0.0, p)], axis=0)
        return _dot(x.astype(BF16), blockdiag.astype(BF16))

    chains = [(rev, hd) for rev in (False, True) for hd in range(heads)]

    g_rows = {}
    for rev, hd in chains:
        g_col = head0 + hd + (GATE_G_B if rev else GATE_G_F)
        g_rows[rev, hd] = refs[rev][4][pl.ds(g_col, 1), :]

    def load(rev, hd, p):
        q_ref, k_ref, v_ref, g_ref = refs[rev][:4]
        rows = slice(p * p_len, (p + 1) * p_len)
        cols = slice(hd * HEAD_DIM, (hd + 1) * HEAD_DIM)
        gates = g_ref[rows, :]
        beta_col = head0 + hd + (GATE_BETA_B if rev else GATE_BETA_F)
        g_col = head0 + hd + (GATE_G_B if rev else GATE_G_F)
        beta = jnp.sum(jnp.where(lane == beta_col, gates, 0.0), axis=1, keepdims=True)
        gcol = jnp.sum(jnp.where(lane == g_col, gates, 0.0), axis=1, keepdims=True)
        grow = g_rows[rev, hd][:, rows]
        if rev:
            glast = (grow[:, 0:1], grow[:, c_len:c_len + 1])
        else:
            glast = (grow[:, c_len - 1:c_len], grow[:, p_len - 1:p_len])
        return q_ref[rows, cols], k_ref[rows, cols], v_ref[rows, cols], beta, gcol, grow, glast

    lo, hi = slice(0, c_len), slice(c_len, p_len)

    zeros = jnp.zeros((c_len, HEAD_DIM), BF16)

    def phase_a(inst):
        data = [load(*i) for i in inst]
        kbeta = [k * beta for (_, k, _, beta, _, _, _) in data]
        prod = []
        for kb_, (q, k, _, _, _, _, _) in zip(kbeta, data):
            lhs = jnp.concatenate([jnp.concatenate([kb_[lo], kb_[hi]], axis=1),
                                   jnp.concatenate([q[lo], q[hi]], axis=1)], axis=0).astype(BF16)
            kb16 = k.astype(BF16)
            rhs = jnp.concatenate([jnp.concatenate([kb16[lo], zeros], axis=1),
                                   jnp.concatenate([zeros, kb16[hi]], axis=1)], axis=0)
            prod.append(_dot_nt(lhs, rhs))
        decay = [jnp.exp(jnp.where(masks[rev][0], jnp.where(first, gcol[lo], gcol[hi]) - grow, -jnp.inf))
                 for (rev, _, _), (_, _, _, _, gcol, grow, _) in zip(inst, data)]
        ms = [jnp.where(masks[rev][1], p[lo] * dec, 0.0) for (rev, _, _), p, dec in zip(inst, prod, decay)]
        a_mat = [p[hi] * dec for p, dec in zip(prod, decay)]
        ts = _unit_tri_inverse(ms, eye, same_sub, pair_mm)
        egs = [jnp.exp(gcol) for (_, _, _, _, gcol, _, _) in data]
        uw = []
        for t, kb_, eg, (_, _, v, beta, _, _, _) in zip(ts, kbeta, egs, data):
            lhs = jnp.concatenate([jnp.where(first, t, 0.0), jnp.where(first, 0.0, t)], axis=0).astype(BF16)
            rhs = jnp.concatenate([(v * beta).astype(BF16), (kb_ * eg).astype(BF16)], axis=1)
            uw.append(_dot(lhs, rhs))
        qdec = [q * eg for eg, (q, _, _, _, _, _, _) in zip(egs, data)]
        kdec_t = [(k * jnp.exp(jnp.where(upper_rows, glast[1], glast[0]) - gcol)).T
                  for (_, k, _, _, gcol, _, glast) in data]
        tall = [jnp.concatenate([a_, kt], axis=0) for a_, kt in zip(a_mat, kdec_t)]
        pre = {}
        for (rev, hd, p), x, qd, tl, (_, _, _, _, _, _, glast) in zip(inst, uw, qdec, tall, data):
            for half, rs in enumerate((lo, hi)):
                wq = jnp.concatenate([x[rs, HEAD_DIM:], qd[rs]], axis=0).astype(BF16)
                keep = first_tall if half == 0 else ~first_tall
                pre[rev, hd, 2 * p + half] = (x[rs, :HEAD_DIM], wq, jnp.where(keep, tl, 0.0).astype(BF16),
                                              jnp.exp(glast[half]))
        return pre

    def phase_b(states, pre, t):
        chunk = {False: t, True: nchunk - 1 - t}
        cur = [pre[rev, hd, chunk[rev]] for rev, hd in chains]
        ws = [_dot(wq, s_.astype(BF16)) for (_, wq, _, _), s_ in zip(cur, states)]
        v_new = [(u - x[lo]).astype(BF16) for (u, _, _, _), x in zip(cur, ws)]
        upd = [_dot(tl, jnp.concatenate([vn, vn], axis=0)) for (_, _, tl, _), vn in zip(cur, v_new)]
        for (rev, hd), x, y in zip(chains, ws, upd):
            c = chunk[rev]
            refs[rev][5][c * c_len:(c + 1) * c_len, hd * HEAD_DIM:(hd + 1) * HEAD_DIM] = x[hi] + y[lo]
        return [s_ * g_ + x[c_len:] for (_, _, _, g_), s_, x in zip(cur, states, upd)]

    npair = nchunk // 2
    states = [state_ref[n] for n in range(len(chains))]
    for s in range(npair):
        pre = phase_a([(rev, hd, npair - 1 - s if rev else s) for rev, hd in chains])
        for t in (2 * s, 2 * s + 1):
            states = phase_b(states, pre, t)
    for n, s_ in enumerate(states):
        state_ref[n] = s_


def _deltanet(q, k, v, gates, gates_t, *, rows, heads):
    s = q.shape[0]
    nb = _steps(s, rows)
    assert rows % (2 * DN_CHUNK) == 0
    fwd = lambda h, b: (b, h)
    bwd = lambda h, b: (nb - 1 - b, h)
    head = lambda im: pl.BlockSpec((rows, heads * HEAD_DIM), im)
    gate = lambda im: pl.BlockSpec((rows, LANES), lambda h, b: (im(h, b)[0], 0))
    gate_t = lambda im: pl.BlockSpec((GATE_ROWS, rows), lambda h, b: (0, im(h, b)[0]))
    return pl.pallas_call(
        _deltanet_kernel,
        grid=(_steps(DN_HEADS, heads), nb),
        in_specs=[head(fwd), head(fwd), head(fwd), gate(fwd), gate_t(fwd),
                  head(bwd), head(bwd), head(bwd), gate(bwd), gate_t(bwd)],
        out_specs=[head(fwd), head(bwd)],
        out_shape=[jax.ShapeDtypeStruct(q.shape, F32), jax.ShapeDtypeStruct(q.shape, F32)],
        scratch_shapes=[pltpu.VMEM((2 * heads, HEAD_DIM, HEAD_DIM), F32)],
        compiler_params=_params("parallel", "arbitrary"),
        name="deltanet",
    )(q, k, v, gates, gates_t, q, k, v, gates, gates_t)


def _t5_bucket(rel):
    nb = NUM_BUCKETS // 2
    max_exact = nb // 2
    n = jnp.abs(rel)
    large = max_exact + (jnp.log(jnp.maximum(n, max_exact).astype(F32) / max_exact)
                         / math.log(MAX_DISTANCE / max_exact) * (nb - max_exact)).astype(jnp.int32)
    large = jnp.minimum(large, nb - 1)
    return jnp.where(rel > 0, nb, 0) + jnp.where(n < max_exact, n, large)


def _swa_kernel(q_ref, kp_ref, kc_ref, kn_ref, vp_ref, vc_ref, vn_ref, bucket_ref, rb_ref, sink_ref,
                o_ref, bias_ref, *, seq):
    n = pl.program_id(0)
    w = WINDOW
    qblk = q_ref.shape[0] // w
    group = SWA_HEADS // SWA_KV_HEADS
    ri = lax.broadcasted_iota(jnp.int32, (w, 3 * w), 0)
    ci = lax.broadcasted_iota(jnp.int32, (w, 3 * w), 1)

    @pl.when(n == 0)
    def _():
        bucket = bucket_ref[...]
        in_band = jnp.abs(ci - w - ri) <= w
        for hd in range(SWA_HEADS):
            acc = jnp.zeros((w, 3 * w), F32)
            for b in range(NUM_BUCKETS):
                acc = jnp.where(bucket == b, rb_ref[b, hd], acc)
            bias_ref[hd] = jnp.where(in_band, acc * LOG2E, -jnp.inf)

    key_pos = (n * qblk - 1) * w + ci
    kext, vext = [], []
    for kvh in range(SWA_KV_HEADS):
        kcols = slice(kvh * HEAD_DIM, (kvh + 1) * HEAD_DIM)
        kext.append(jnp.concatenate([kp_ref[:, kcols], kc_ref[:, kcols], kn_ref[:, kcols]], axis=0).astype(BF16))
        vext.append(jnp.concatenate([vp_ref[:, kcols], vc_ref[:, kcols], vn_ref[:, kcols]], axis=0).astype(BF16))
    heads = range(SWA_HEADS)
    sinks = [sink_ref[hd] * LOG2E for hd in heads]
    for t in range(qblk):
        rows = slice(t * w, (t + 1) * w)
        band = slice(t * w, (t + 3) * w)
        s = [_dot_nt(q_ref[rows, hd * HEAD_DIM:(hd + 1) * HEAD_DIM].astype(BF16), kext[hd // group][band])
             * (HEAD_DIM ** -0.5 * LOG2E) + bias_ref[hd] for hd in heads]
        if t == 0:
            s = [jnp.where(key_pos >= 0, x, -jnp.inf) for x in s]
        if t == qblk - 1:
            s = [jnp.where(key_pos + t * w < seq, x, -jnp.inf) for x in s]
        mx = [jnp.maximum(jnp.max(x, axis=1, keepdims=True), sk) for x, sk in zip(s, sinks)]
        p = [jnp.exp2(x - m) for x, m in zip(s, mx)]
        den = [jnp.sum(x, axis=1, keepdims=True) + jnp.exp2(sk - m) for x, sk, m in zip(p, sinks, mx)]
        o = [_dot(x.astype(BF16), vext[hd // group][band]) / dn for hd, x, dn in zip(heads, p, den)]
        for hd, x in zip(heads, o):
            o_ref[rows, hd * HEAD_DIM:(hd + 1) * HEAD_DIM] = x.astype(o_ref.dtype)


def _swa(proj, bucket, rel_bias, sink, *, qblk):
    s = proj.shape[0]
    w = WINDOW
    rows = qblk * w
    nb = s // w
    qw = SWA_HEADS * HEAD_DIM
    kvw = SWA_KV_HEADS * HEAD_DIM
    main = lambda off: pl.BlockSpec((rows, kvw), lambda n: (n, off // kvw))
    halo = lambda off, lo: pl.BlockSpec(
        (w, kvw), lambda n: (jnp.clip(n * qblk - 1 if lo else (n + 1) * qblk, 0, nb - 1), off // kvw))
    return pl.pallas_call(
        functools.partial(_swa_kernel, seq=s),
        grid=(_steps(s, rows),),
        in_specs=[pl.BlockSpec((rows, qw), lambda n: (n, QSW_OFF // qw)),
                  halo(KSW_OFF, True), main(KSW_OFF), halo(KSW_OFF, False),
                  halo(VSW_OFF, True), main(VSW_OFF), halo(VSW_OFF, False),
                  pl.BlockSpec((w, 3 * w), lambda n: (0, 0)),
                  pl.BlockSpec(memory_space=pltpu.SMEM),
                  pl.BlockSpec(memory_space=pltpu.SMEM)],
        out_specs=pl.BlockSpec((rows, qw), lambda n: (n, 0)),
        out_shape=jax.ShapeDtypeStruct((s, qw), BF16),
        scratch_shapes=[pltpu.VMEM((SWA_HEADS, w, 3 * w), F32)],
        compiler_params=_params("arbitrary"),
        name="swa",
    )(proj, proj, proj, proj, proj, proj, proj, bucket, rel_bias, sink)


def _mix_mem_kernel(of_ref, ob_ref, z_ref, sw_ref, dng_ref, wout_ref, x_ref, gx_ref, wq_ref, k_ref, v_ref, wo_ref,
                    gf_ref, h2_ref, f_ref):
    batches = [slice(r, r + ROW_BATCH) for r in range(0, h2_ref.shape[0], ROW_BATCH)]
    scale = HEAD_DIM ** -0.5
    lhs = []
    for rows in batches:
        parts = []
        for hd in range(DN_HEADS):
            cols = slice(hd * HEAD_DIM, (hd + 1) * HEAD_DIM)
            z = z_ref[rows, cols]
            y = _rms(of_ref[rows, cols] + ob_ref[rows, cols], dng_ref[...]) * (z * jax.nn.sigmoid(z))
            parts.append(y.astype(BF16))
        parts.append(sw_ref[rows, :])
        lhs.append(jnp.concatenate(parts, axis=1))
    h1 = [x_ref[rows, :] + _dot(a, wout_ref[...]) for rows, a in zip(batches, lhs)]
    q = [_dot(_rms(h, gx_ref[...]).astype(BF16), wq_ref[...]) for h in h1]
    att = []
    for qb in q:
        heads = []
        for hd in range(MEM_HEADS):
            cols = slice(hd * HEAD_DIM, (hd + 1) * HEAD_DIM)
            s = _dot_nt(qb[:, cols].astype(BF16), k_ref[:, cols]) * scale
            p = jnp.exp(s - jnp.max(s, axis=1, keepdims=True))
            den = jnp.sum(p, axis=1, keepdims=True)
            heads.append((_dot(p.astype(BF16), v_ref[:, cols]) / den).astype(BF16))
        att.append(jnp.concatenate(heads, axis=1))
    h2 = [h + _dot(a, wo_ref[...]) for h, a in zip(h1, att)]
    for rows, h in zip(batches, h2):
        h2_ref[rows, :] = h
        f_ref[rows, :] = _rms(h, gf_ref[...]).astype(f_ref.dtype)


def _mix_mem(o_f, o_b, proj, y_sw, dn_g, w_out, x, gx, wq, k, v, wo, gf, *, tm):
    s, d = x.shape
    dn = DN_HEADS * HEAD_DIM
    mix = w_out.shape[0]
    row = lambda width, col=0: pl.BlockSpec((tm, width), lambda i: (i, col))
    const = lambda a: pl.BlockSpec(a.shape, lambda i: (0, 0), pipeline_mode=pl.Buffered(1))
    dn_g, gx, gf = dn_g.reshape(1, HEAD_DIM), gx.reshape(1, d), gf.reshape(1, d)
    return pl.pallas_call(
        _mix_mem_kernel,
        grid=(_steps(s, tm),),
        in_specs=[row(dn), row(dn), row(dn, Z_OFF // dn), row(mix - dn), const(dn_g), const(w_out), row(d),
                  const(gx), const(wq), const(k), const(v), const(wo), const(gf)],
        out_specs=[row(d), row(d)],
        out_shape=[jax.ShapeDtypeStruct((s, d), F32), jax.ShapeDtypeStruct((s, d), BF16)],
        compiler_params=_params("parallel"),
        name="mix_mem",
    )(o_f, o_b, proj, y_sw, dn_g, w_out, x, gx, wq, k, v, wo, gf)


def _glu_kernel(f_ref, wg_ref, wu_ref, o_ref):
    f = f_ref[...]
    a = _dot(f, wg_ref[...].astype(BF16))
    o_ref[...] = (a * jax.nn.sigmoid(a) * _dot(f, wu_ref[...].astype(BF16))).astype(o_ref.dtype)


def _ffn_glu(f, wg, wu, *, tm, tn):
    s, d = f.shape
    dff = wg.shape[1]
    return pl.pallas_call(
        _glu_kernel,
        grid=(_steps(s, tm), _steps(dff, tn)),
        in_specs=[pl.BlockSpec((tm, d), lambda i, j: (i, 0)),
                  pl.BlockSpec((d, tn), lambda i, j: (0, j)),
                  pl.BlockSpec((d, tn), lambda i, j: (0, j))],
        out_specs=pl.BlockSpec((tm, tn), lambda i, j: (i, j)),
        out_shape=jax.ShapeDtypeStruct((s, dff), BF16),
        compiler_params=_params("parallel", "parallel"),
        name="ffn_glu",
    )(f, wg, wu)


def _down_kernel(a_ref, w_ref, h_ref, g_ref, o_ref, acc_ref):
    k = pl.program_id(1)

    @pl.when(k == 0)
    def _():
        acc_ref[...] = h_ref[...]

    acc_ref[...] += _dot(a_ref[...], w_ref[...])

    @pl.when(k == pl.num_programs(1) - 1)
    def _():
        o_ref[...] = _rms(acc_ref[...], g_ref[...])


def _ffn_down(act, wd, h, g, *, tm, tk):
    s, dff = act.shape
    d = wd.shape[1]
    return pl.pallas_call(
        _down_kernel,
        grid=(_steps(s, tm), _steps(dff, tk)),
        in_specs=[pl.BlockSpec((tm, tk), lambda i, k: (i, k)),
                  pl.BlockSpec((tk, d), lambda i, k: (k, 0)),
                  pl.BlockSpec((tm, d), lambda i, k: (i, 0)),
                  pl.BlockSpec((1, d), lambda i, k: (0, 0))],
        out_specs=pl.BlockSpec((tm, d), lambda i, k: (i, 0)),
        out_shape=jax.ShapeDtypeStruct((s, d), F32),
        scratch_shapes=[pltpu.VMEM((tm, d), F32)],
        compiler_params=_params("parallel", "arbitrary"),
        name="ffn_down",
    )(act, wd, h, g.reshape(1, d))


def _gate_row(fwd, bwd):
    row = jnp.zeros((1, LANES), F32)
    row = row.at[0, GATE_G_F:GATE_G_F + DN_HEADS].set(fwd.astype(F32))
    return row.at[0, GATE_G_B:GATE_G_B + DN_HEADS].set(bwd.astype(F32))


def _pick(n, *cands):
    for c in cands:
        if n % c == 0:
            return c
    return n


def kernel(x, mem, norm_mix_g, w_in, conv_w, a_log_f, a_log_b, dt_bias_f, dt_bias_b, dn_norm_g, attn_sink, rel_bias, w_out, norm_x_g, norm_mem_g, w_q_mem, w_kv_mem, w_o_mem, norm_ffn_g, w_gate, w_up, w_down, norm_final_g):
    batch, s, d = x.shape
    assert batch == 1 and mem.shape[0] == 1 and w_in.shape[0] == 1, "single sequence, single layer"
    w = WINDOW
    rel = (jnp.arange(3 * w)[None, :] - w) - jnp.arange(w)[:, None]
    bucket = _t5_bucket(rel).astype(jnp.int32)
    mem_dim = MEM_HEADS * HEAD_DIM

    h = x.reshape(s, d)
    mem2 = mem.reshape(mem.shape[1], d)
    tm_big = _pick(s, 1024, 512, 256, 128)
    tm_mid = _pick(s, 512, 256, 128)
    w_r = _w_in_layout(w_in[0].T, cols=256)
    proj = _rms_matmul(h, norm_mix_g[0], w_r, tm=tm_big, tn=_pick(PROJ_WIDTH, 1920, 1152, 640, 128), name="in_proj",
                       w_transposed=True)

    q, k, v, gates, gates_t = _dn_prep(proj, conv_w[0], _gate_row(a_log_f[0], a_log_b[0]),
                                       _gate_row(dt_bias_f[0], dt_bias_b[0]), rows=_pick(s, 256, 128))
    o_f, o_b = _deltanet(q, k, v, gates, gates_t, rows=_pick(s, 512, 256, 128), heads=8)
    y_sw = _swa(proj, bucket, rel_bias.astype(F32), attn_sink[0].astype(F32), qblk=_pick(s // w, 4, 2, 1))
    kv = _rms_matmul(mem2, norm_mem_g[0], w_kv_mem[0].astype(BF16), tm=mem2.shape[0], tn=2 * mem_dim,
                     name="mem_kv").astype(BF16)
    h, f = _mix_mem(o_f, o_b, proj, y_sw, dn_norm_g[0], w_out[0].astype(BF16), h, norm_x_g[0],
                    w_q_mem[0].astype(BF16), kv[:, :mem_dim], kv[:, mem_dim:], w_o_mem[0].astype(BF16),
                    norm_ffn_g[0], tm=_pick(s, 256, 128))

    act = _ffn_glu(f, w_gate[0], w_up[0], tm=tm_big, tn=512)
    out = _ffn_down(act, w_down[0].astype(BF16), h, norm_final_g, tm=tm_mid, tk=2816)
    return out.reshape(batch, s, d)
```

```python
import functools
import math

import jax
import jax.numpy as jnp
from jax import lax
from jax.experimental import pallas as pl
from jax.experimental.pallas import tpu as pltpu

F32 = jnp.float32
BF16 = jnp.bfloat16

RMS_EPS = 1e-6
L2_EPS = 1e-6
HEAD_DIM = 128
DN_HEADS = 8
DN_CHUNK = 64
DN_SUB = 16
CONV_WIDTH = 5
CONV_HALO = 8
SWA_HEADS = 8
SWA_KV_HEADS = 2
WINDOW = 128
NUM_BUCKETS = 32
MAX_DISTANCE = 128
MEM_HEADS = 4
LOG2E = 1.4426950408889634
LANES = 128
ROW_BATCH = 128

DN_QKV = 3 * DN_HEADS * HEAD_DIM
Z_OFF = DN_QKV
QSW_OFF = Z_OFF + DN_HEADS * HEAD_DIM
KSW_OFF = QSW_OFF + SWA_HEADS * HEAD_DIM
VSW_OFF = KSW_OFF + SWA_KV_HEADS * HEAD_DIM
GATE_OFF = VSW_OFF + SWA_KV_HEADS * HEAD_DIM
PROJ_WIDTH = GATE_OFF + LANES
GATE_BETA_F, GATE_BETA_B, GATE_G_F, GATE_G_B = 0, DN_HEADS, 2 * DN_HEADS, 3 * DN_HEADS
GATE_ROWS = 4 * DN_HEADS

VMEM_LIMIT_V7X = 56 * 1024 * 1024


def _params(*sem):
    return pltpu.CompilerParams(dimension_semantics=sem, vmem_limit_bytes=VMEM_LIMIT_V7X)


def _steps(dim, tile):
    assert dim % tile == 0, f"tile {tile} does not divide {dim}"
    return dim // tile


def _dot(a, b):
    return jnp.dot(a, b, preferred_element_type=F32)


def _dot_nt(a, b):
    return lax.dot_general(a, b, (((1,), (1,)), ((), ())), preferred_element_type=F32)


def _dot_tn(a, b):
    return lax.dot_general(a, b, (((0,), (0,)), ((), ())), preferred_element_type=F32)


def _rms(x, g):
    return x * lax.rsqrt(jnp.mean(x * x, axis=-1, keepdims=True) + RMS_EPS) * g


def _w_in_kernel(w_ref, o_ref):
    gate_lo = QSW_OFF
    o_ref[:gate_lo, :] = w_ref[:gate_lo, :].astype(o_ref.dtype)
    o_ref[gate_lo:GATE_OFF, :] = w_ref[gate_lo + GATE_ROWS:, :].astype(o_ref.dtype)
    o_ref[GATE_OFF:GATE_OFF + GATE_ROWS, :] = w_ref[gate_lo:gate_lo + GATE_ROWS, :].astype(o_ref.dtype)
    o_ref[GATE_OFF + GATE_ROWS:, :] = jnp.zeros((LANES - GATE_ROWS, o_ref.shape[1]), o_ref.dtype)


def _w_in_layout(w_t, *, cols):
    n, k = w_t.shape
    assert n == PROJ_WIDTH - (LANES - GATE_ROWS)
    return pl.pallas_call(
        _w_in_kernel,
        grid=(_steps(k, cols),),
        in_specs=[pl.BlockSpec((n, cols), lambda i: (0, i))],
        out_specs=pl.BlockSpec((PROJ_WIDTH, cols), lambda i: (0, i)),
        out_shape=jax.ShapeDtypeStruct((PROJ_WIDTH, k), BF16),
        compiler_params=_params("parallel"),
        name="w_in_layout",
    )(w_t)


def _rms_matmul_kernel(x_ref, g_ref, w_ref, o_ref, n_ref, *, w_transposed):
    dot = _dot_nt if w_transposed else _dot

    @pl.when(pl.program_id(1) == 0)
    def _():
        half = max(x_ref.shape[0] // 2, ROW_BATCH)
        for r in range(0, x_ref.shape[0], half):
            rows = slice(r, r + half)
            n = _rms(x_ref[rows, :], g_ref[...]).astype(n_ref.dtype)
            n_ref[rows, :] = n
            o_ref[rows, :] = dot(n, w_ref[...]).astype(o_ref.dtype)

    @pl.when(pl.program_id(1) != 0)
    def _():
        o_ref[...] = dot(n_ref[...], w_ref[...]).astype(o_ref.dtype)


def _rms_matmul(x, g, w, *, tm, tn, name, w_transposed=False):
    m, k = x.shape
    n = w.shape[0] if w_transposed else w.shape[1]
    w_spec = pl.BlockSpec((tn, k), lambda i, j: (j, 0)) if w_transposed else pl.BlockSpec((k, tn), lambda i, j: (0, j))
    return pl.pallas_call(
        functools.partial(_rms_matmul_kernel, w_transposed=w_transposed),
        grid=(_steps(m, tm), _steps(n, tn)),
        in_specs=[pl.BlockSpec((tm, k), lambda i, j: (i, 0)),
                  pl.BlockSpec((1, k), lambda i, j: (0, 0)),
                  w_spec],
        out_specs=pl.BlockSpec((tm, tn), lambda i, j: (i, j)),
        out_shape=jax.ShapeDtypeStruct((m, n), F32),
        scratch_shapes=[pltpu.VMEM((tm, k), BF16)],
        compiler_params=_params("parallel", "arbitrary"),
        name=name,
    )(x, g.reshape(1, k), w)


def _split3(x):
    hi = x.astype(BF16)
    r = x - hi.astype(F32)
    mid = r.astype(BF16)
    lo = (r - mid.astype(F32)).astype(BF16)
    return hi, mid, lo


def _prep_kernel(main_ref, prev_ref, next_ref, gate_ref, cw_ref, alog_ref, dt_ref,
                 q_ref, k_ref, v_ref, go_ref, gt_ref, ext_ref):
    i = pl.program_id(0)
    rows = main_ref.shape[0]
    halo = CONV_HALO
    pad = (CONV_WIDTH - 1) // 2

    ext_ref[0:halo, :] = jnp.where(i > 0, prev_ref[...], 0.0)
    ext_ref[halo:halo + rows, :] = main_ref[...]
    ext_ref[halo + rows:, :] = jnp.where(i < pl.num_programs(0) - 1, next_ref[...], 0.0)

    for s in range(3 * DN_HEADS):
        cols = slice(s * HEAD_DIM, (s + 1) * HEAD_DIM)
        xe = ext_ref[:, cols]
        acc = cw_ref[pad:pad + 1, cols] * xe[halo:halo + rows]
        for j in range(CONV_WIDTH):
            if j != pad:
                shifted = pltpu.roll(xe, shift=(pad - j) % xe.shape[0], axis=0)[halo:halo + rows]
                acc = acc + cw_ref[j:j + 1, cols] * shifted
        y = acc * jax.nn.sigmoid(acc)
        if s < 2 * DN_HEADS:
            inv = lax.rsqrt(jnp.sum(y * y, axis=-1, keepdims=True) + L2_EPS)
            y = y * (inv * (HEAD_DIM ** -0.5) if s < DN_HEADS else inv)
        if s < DN_HEADS:
            q_ref[:, cols] = y
        elif s < 2 * DN_HEADS:
            k_ref[:, slice((s - DN_HEADS) * HEAD_DIM, (s - DN_HEADS + 1) * HEAD_DIM)] = y
        else:
            v_ref[:, slice((s - 2 * DN_HEADS) * HEAD_DIM, (s - 2 * DN_HEADS + 1) * HEAD_DIM)] = y

    t = gate_ref[...]
    beta = jax.nn.sigmoid(t)
    a = t + dt_ref[...]
    softplus = jnp.maximum(a, 0.0) + jnp.log1p(jnp.exp(-jnp.abs(a)))
    g = -jnp.exp(alog_ref[...]) * softplus

    ri = lax.broadcasted_iota(jnp.int32, (rows, rows), 0)
    ci = lax.broadcasted_iota(jnp.int32, (rows, rows), 1)
    shift = DN_CHUNK.bit_length() - 1
    same_chunk = (ri >> shift) == (ci >> shift)
    lower = jnp.where(same_chunk & (ci <= ri), 1.0, 0.0).astype(BF16)
    upper = jnp.where(same_chunk & (ci >= ri), 1.0, 0.0).astype(BF16)
    parts = _split3(g)
    gc_f = _dot(lower, parts[0]) + _dot(lower, parts[1]) + _dot(lower, parts[2])
    gc_b = _dot(upper, parts[0]) + _dot(upper, parts[1]) + _dot(upper, parts[2])

    col = lax.broadcasted_iota(jnp.int32, t.shape, 1)
    out = jnp.where(col < GATE_G_F, beta, jnp.where(col < GATE_G_B, gc_f, gc_b))
    go_ref[...] = out
    gt_ref[...] = out.T[0:GATE_ROWS, :]


def _dn_prep(proj, conv_w, alog_row, dt_row, *, rows):
    s = proj.shape[0]
    nblk = _steps(s, rows)
    hb = rows // CONV_HALO
    last_halo = s // CONV_HALO - 1
    head_cols = DN_HEADS * HEAD_DIM
    return pl.pallas_call(
        _prep_kernel,
        grid=(nblk,),
        in_specs=[pl.BlockSpec((rows, DN_QKV), lambda i: (i, 0)),
                  pl.BlockSpec((CONV_HALO, DN_QKV), lambda i: (jnp.maximum(i * hb - 1, 0), 0)),
                  pl.BlockSpec((CONV_HALO, DN_QKV), lambda i: (jnp.minimum((i + 1) * hb, last_halo), 0)),
                  pl.BlockSpec((rows, LANES), lambda i: (i, GATE_OFF // LANES)),
                  pl.BlockSpec((CONV_WIDTH, DN_QKV), lambda i: (0, 0)),
                  pl.BlockSpec((1, LANES), lambda i: (0, 0)),
                  pl.BlockSpec((1, LANES), lambda i: (0, 0))],
        out_specs=[pl.BlockSpec((rows, head_cols), lambda i: (i, 0)),
                   pl.BlockSpec((rows, head_cols), lambda i: (i, 0)),
                   pl.BlockSpec((rows, head_cols), lambda i: (i, 0)),
                   pl.BlockSpec((rows, LANES), lambda i: (i, 0)),
                   pl.BlockSpec((GATE_ROWS, rows), lambda i: (0, i))],
        out_shape=[jax.ShapeDtypeStruct((s, head_cols), F32),
                   jax.ShapeDtypeStruct((s, head_cols), F32),
                   jax.ShapeDtypeStruct((s, head_cols), F32),
                   jax.ShapeDtypeStruct((s, LANES), F32),
                   jax.ShapeDtypeStruct((GATE_ROWS, s), F32)],
        scratch_shapes=[pltpu.VMEM((rows + 2 * CONV_HALO, DN_QKV), F32)],
        compiler_params=_params("parallel"),
        name="dn_prep",
    )(proj, proj, proj, proj, conv_w, alog_row, dt_row)


def _unit_tri_inverse(ms, eye, same_sub, _mm):
    c = eye.shape[0]
    mds = [jnp.where(same_sub, m, 0.0) for m in ms]
    es = [m - md for m, md in zip(ms, mds)]
    pw = [_mm(md, md) for md in mds]
    xs = [eye - md for md in mds]
    for _ in range(DN_SUB.bit_length() - 3):
        both = [_mm(jnp.concatenate([a, x], axis=0), a) for a, x in zip(pw, xs)]
        pw = [b[:c] for b in both]
        xs = [x + b[c:] for x, b in zip(xs, both)]
    xs = [x + _mm(x, a) for x, a in zip(xs, pw)]
    ns = [_mm(x, e) for x, e in zip(xs, es)]
    n2 = [_mm(n, n) for n in ns]
    ys = [_mm(n, (eye - n) + b) for n, b in zip(ns, n2)]
    return [x - _mm(y, x) for x, y in zip(xs, ys)]


def _deltanet_kernel(qf, kf, vf, gf, gtf, qb, kb, vb, gb, gtb, of_ref, ob_ref, state_ref):
    c_len = DN_CHUNK
    nchunk = qf.shape[0] // c_len
    heads = qf.shape[1] // HEAD_DIM
    head0 = pl.program_id(0) * heads

    @pl.when(pl.program_id(1) == 0)
    def _():
        state_ref[...] = jnp.zeros_like(state_ref)

    p_len = 2 * c_len
    ri = lax.broadcasted_iota(jnp.int32, (c_len, p_len), 0)
    cl = lax.broadcasted_iota(jnp.int32, (c_len, p_len), 1)
    ci = cl & (c_len - 1)
    first = cl < c_len
    eye = jnp.where(ri == ci, 1.0, 0.0)
    sub_shift = DN_SUB.bit_length() - 1
    same_sub = (ri >> sub_shift) == (ci >> sub_shift)
    lane = lax.broadcasted_iota(jnp.int32, (p_len, LANES), 1)
    upper_rows = lax.broadcasted_iota(jnp.int32, (p_len, 1), 0) >= c_len
    first_tall = lax.broadcasted_iota(jnp.int32, (c_len + HEAD_DIM, p_len), 1) < c_len
    masks = {False: (ci <= ri, ci < ri), True: (ci >= ri, ci > ri)}
    refs = {False: (qf, kf, vf, gf, gtf, of_ref), True: (qb, kb, vb, gb, gtb, ob_ref)}

    def pair_mm(x, p):
        blockdiag = jnp.concatenate([jnp.where(first, p, 0.0), jnp.where(first, 0.0, p)], axis=0)
        return _dot(x.astype(BF16), blockdiag.astype(BF16))

    chains = [(rev, hd) for rev in (False, True) for hd in range(heads)]

    g_rows = {}
    for rev, hd in chains:
        g_col = head0 + hd + (GATE_G_B if rev else GATE_G_F)
        g_rows[rev, hd] = refs[rev][4][pl.ds(g_col, 1), :]

    def load(rev, hd, p):
        q_ref, k_ref, v_ref, g_ref = refs[rev][:4]
        rows = slice(p * p_len, (p + 1) * p_len)
        cols = slice(hd * HEAD_DIM, (hd + 1) * HEAD_DIM)
        gates = g_ref[rows, :]
        beta_col = head0 + hd + (GATE_BETA_B if rev else GATE_BETA_F)
        g_col = head0 + hd + (GATE_G_B if rev else GATE_G_F)
        beta = jnp.sum(jnp.where(lane == beta_col, gates, 0.0), axis=1, keepdims=True)
        gcol = jnp.sum(jnp.where(lane == g_col, gates, 0.0), axis=1, keepdims=True)
        grow = g_rows[rev, hd][:, rows]
        if rev:
            glast = (grow[:, 0:1], grow[:, c_len:c_len + 1])
        else:
            glast = (grow[:, c_len - 1:c_len], grow[:, p_len - 1:p_len])
        return q_ref[rows, cols], k_ref[rows, cols], v_ref[rows, cols], beta, gcol, grow, glast

    lo, hi = slice(0, c_len), slice(c_len, p_len)

    zeros = jnp.zeros((c_len, HEAD_DIM), BF16)

    def phase_a(inst):
        data = [load(*i) for i in inst]
        kbeta = [k * beta for (_, k, _, beta, _, _, _) in data]
        prod = []
        for kb_, (q, k, _, _, _, _, _) in zip(kbeta, data):
            lhs = jnp.concatenate([jnp.concatenate([kb_[lo], kb_[hi]], axis=1),
                                   jnp.concatenate([q[lo], q[hi]], axis=1)], axis=0).astype(BF16)
            kb16 = k.astype(BF16)
            rhs = jnp.concatenate([jnp.concatenate([kb16[lo], zeros], axis=1),
                                   jnp.concatenate([zeros, kb16[hi]], axis=1)], axis=0)
            prod.append(_dot_nt(lhs, rhs))
        decay = [jnp.exp(jnp.where(masks[rev][0], jnp.where(first, gcol[lo], gcol[hi]) - grow, -jnp.inf))
                 for (rev, _, _), (_, _, _, _, gcol, grow, _) in zip(inst, data)]
        ms = [jnp.where(masks[rev][1], p[lo] * dec, 0.0) for (rev, _, _), p, dec in zip(inst, prod, decay)]
        a_mat = [p[hi] * dec for p, dec in zip(prod, decay)]
        ts = _unit_tri_inverse(ms, eye, same_sub, pair_mm)
        egs = [jnp.exp(gcol) for (_, _, _, _, gcol, _, _) in data]
        uw = []
        for t, kb_, eg, (_, _, v, beta, _, _, _) in zip(ts, kbeta, egs, data):
            lhs = jnp.concatenate([jnp.where(first, t, 0.0), jnp.where(first, 0.0, t)], axis=0).astype(BF16)
            rhs = jnp.concatenate([(v * beta).astype(BF16), (kb_ * eg).astype(BF16)], axis=1)
            uw.append(_dot(lhs, rhs))
        qdec = [q * eg for eg, (q, _, _, _, _, _, _) in zip(egs, data)]
        kdec_t = [(k * jnp.exp(jnp.where(upper_rows, glast[1], glast[0]) - gcol)).T
                  for (_, k, _, _, gcol, _, glast) in data]
        tall = [jnp.concatenate([a_, kt], axis=0) for a_, kt in zip(a_mat, kdec_t)]
        pre = {}
        for (rev, hd, p), x, qd, tl, (_, _, _, _, _, _, glast) in zip(inst, uw, qdec, tall, data):
            for half, rs in enumerate((lo, hi)):
                wq = jnp.concatenate([x[rs, HEAD_DIM:], qd[rs]], axis=0).astype(BF16)
                keep = first_tall if half == 0 else ~first_tall
                pre[rev, hd, 2 * p + half] = (x[rs, :HEAD_DIM], wq, jnp.where(keep, tl, 0.0).astype(BF16),
                                              jnp.exp(glast[half]))
        return pre

    def phase_b(states, pre, t):
        chunk = {False: t, True: nchunk - 1 - t}
        cur = [pre[rev, hd, chunk[rev]] for rev, hd in chains]
        ws = [_dot(wq, s_.astype(BF16)) for (_, wq, _, _), s_ in zip(cur, states)]
        v_new = [(u - x[lo]).astype(BF16) for (u, _, _, _), x in zip(cur, ws)]
        upd = [_dot(tl, jnp.concatenate([vn, vn], axis=0)) for (_, _, tl, _), vn in zip(cur, v_new)]
        for (rev, hd), x, y in zip(chains, ws, upd):
            c = chunk[rev]
            refs[rev][5][c * c_len:(c + 1) * c_len, hd * HEAD_DIM:(hd + 1) * HEAD_DIM] = x[hi] + y[lo]
        return [s_ * g_ + x[c_len:] for (_, _, _, g_), s_, x in zip(cur, states, upd)]

    npair = nchunk // 2
    states = [state_ref[n] for n in range(len(chains))]
    for s in range(npair):
        pre = phase_a([(rev, hd, npair - 1 - s if rev else s) for rev, hd in chains])
        for t in (2 * s, 2 * s + 1):
            states = phase_b(states, pre, t)
    for n, s_ in enumerate(states):
        state_ref[n] = s_


def _deltanet(q, k, v, gates, gates_t, *, rows, heads):
    s = q.shape[0]
    nb = _steps(s, rows)
    assert rows % (2 * DN_CHUNK) == 0
    fwd = lambda h, b: (b, h)
    bwd = lambda h, b: (nb - 1 - b, h)
    head = lambda im: pl.BlockSpec((rows, heads * HEAD_DIM), im)
    gate = lambda im: pl.BlockSpec((rows, LANES), lambda h, b: (im(h, b)[0], 0))
    gate_t = lambda im: pl.BlockSpec((GATE_ROWS, rows), lambda h, b: (0, im(h, b)[0]))
    return pl.pallas_call(
        _deltanet_kernel,
        grid=(_steps(DN_HEADS, heads), nb),
        in_specs=[head(fwd), head(fwd), head(fwd), gate(fwd), gate_t(fwd),
                  head(bwd), head(bwd), head(bwd), gate(bwd), gate_t(bwd)],
        out_specs=[head(fwd), head(bwd)],
        out_shape=[jax.ShapeDtypeStruct(q.shape, F32), jax.ShapeDtypeStruct(q.shape, F32)],
        scratch_shapes=[pltpu.VMEM((2 * heads, HEAD_DIM, HEAD_DIM), F32)],
        compiler_params=_params("parallel", "arbitrary"),
        name="deltanet",
    )(q, k, v, gates, gates_t, q, k, v, gates, gates_t)


def _t5_bucket(rel):
    nb = NUM_BUCKETS // 2
    max_exact = nb // 2
    n = jnp.abs(rel)
    large = max_exact + (jnp.log(jnp.maximum(n, max_exact).astype(F32) / max_exact)
                         / math.log(MAX_DISTANCE / max_exact) * (nb - max_exact)).astype(jnp.int32)
    large = jnp.minimum(large, nb - 1)
    return jnp.where(rel > 0, nb, 0) + jnp.where(n < max_exact, n, large)


def _swa_kernel(q_ref, kp_ref, kc_ref, kn_ref, vp_ref, vc_ref, vn_ref, bucket_ref, rb_ref, sink_ref,
                o_ref, bias_ref, *, seq):
    n = pl.program_id(0)
    w = WINDOW
    qblk = q_ref.shape[0] // w
    group = SWA_HEADS // SWA_KV_HEADS
    ri = lax.broadcasted_iota(jnp.int32, (w, 3 * w), 0)
    ci = lax.broadcasted_iota(jnp.int32, (w, 3 * w), 1)

    @pl.when(n == 0)
    def _():
        bucket = bucket_ref[...]
        in_band = jnp.abs(ci - w - ri) <= w
        for hd in range(SWA_HEADS):
            acc = jnp.zeros((w, 3 * w), F32)
            for b in range(NUM_BUCKETS):
                acc = jnp.where(bucket == b, rb_ref[b, hd], acc)
            bias_ref[hd] = jnp.where(in_band, acc * LOG2E, -jnp.inf)

    key_pos = (n * qblk - 1) * w + ci
    kext, vext = [], []
    for kvh in range(SWA_KV_HEADS):
        kcols = slice(kvh * HEAD_DIM, (kvh + 1) * HEAD_DIM)
        kext.append(jnp.concatenate([kp_ref[:, kcols], kc_ref[:, kcols], kn_ref[:, kcols]], axis=0).astype(BF16))
        vext.append(jnp.concatenate([vp_ref[:, kcols], vc_ref[:, kcols], vn_ref[:, kcols]], axis=0).astype(BF16))
    heads = range(SWA_HEADS)
    sinks = [sink_ref[hd] * LOG2E for hd in heads]
    for t in range(qblk):
        rows = slice(t * w, (t + 1) * w)
        band = slice(t * w, (t + 3) * w)
        s = [_dot_nt(q_ref[rows, hd * HEAD_DIM:(hd + 1) * HEAD_DIM].astype(BF16), kext[hd // group][band])
             * (HEAD_DIM ** -0.5 * LOG2E) + bias_ref[hd] for hd in heads]
        if t == 0:
            s = [jnp.where(key_pos >= 0, x, -jnp.inf) for x in s]
        if t == qblk - 1:
            s = [jnp.where(key_pos + t * w < seq, x, -jnp.inf) for x in s]
        mx = [jnp.maximum(jnp.max(x, axis=1, keepdims=True), sk) for x, sk in zip(s, sinks)]
        p = [jnp.exp2(x - m) for x, m in zip(s, mx)]
        den = [jnp.sum(x, axis=1, keepdims=True) + jnp.exp2(sk - m) for x, sk, m in zip(p, sinks, mx)]
        o = [_dot(x.astype(BF16), vext[hd // group][band]) / dn for hd, x, dn in zip(heads, p, den)]
        for hd, x in zip(heads, o):
            o_ref[rows, hd * HEAD_DIM:(hd + 1) * HEAD_DIM] = x.astype(o_ref.dtype)


def _swa(proj, bucket, rel_bias, sink, *, qblk):
    s = proj.shape[0]
    w = WINDOW
    rows = qblk * w
    nb = s // w
    qw = SWA_HEADS * HEAD_DIM
    kvw = SWA_KV_HEADS * HEAD_DIM
    main = lambda off: pl.BlockSpec((rows, kvw), lambda n: (n, off // kvw))
    halo = lambda off, lo: pl.BlockSpec(
        (w, kvw), lambda n: (jnp.clip(n * qblk - 1 if lo else (n + 1) * qblk, 0, nb - 1), off // kvw))
    return pl.pallas_call(
        functools.partial(_swa_kernel, seq=s),
        grid=(_steps(s, rows),),
        in_specs=[pl.BlockSpec((rows, qw), lambda n: (n, QSW_OFF // qw)),
                  halo(KSW_OFF, True), main(KSW_OFF), halo(KSW_OFF, False),
                  halo(VSW_OFF, True), main(VSW_OFF), halo(VSW_OFF, False),
                  pl.BlockSpec((w, 3 * w), lambda n: (0, 0)),
                  pl.BlockSpec(memory_space=pltpu.SMEM),
                  pl.BlockSpec(memory_space=pltpu.SMEM)],
        out_specs=pl.BlockSpec((rows, qw), lambda n: (n, 0)),
        out_shape=jax.ShapeDtypeStruct((s, qw), BF16),
        scratch_shapes=[pltpu.VMEM((SWA_HEADS, w, 3 * w), F32)],
        compiler_params=_params("arbitrary"),
        name="swa",
    )(proj, proj, proj, proj, proj, proj, proj, bucket, rel_bias, sink)


def _mix_mem_kernel(of_ref, ob_ref, z_ref, sw_ref, dng_ref, wout_ref, x_ref, gx_ref, wq_ref, k_ref, v_ref, wo_ref,
                    gf_ref, h2_ref, f_ref):
    scale = HEAD_DIM ** -0.5
    for r0 in range(0, h2_ref.shape[0], 2 * ROW_BATCH):
        _mix_mem_rows([slice(r, r + ROW_BATCH) for r in range(r0, r0 + 2 * ROW_BATCH, ROW_BATCH)], scale,
                      of_ref, ob_ref, z_ref, sw_ref, dng_ref, wout_ref, x_ref, gx_ref, wq_ref, k_ref, v_ref, wo_ref,
                      gf_ref, h2_ref, f_ref)


def _mix_mem_rows(batches, scale, of_ref, ob_ref, z_ref, sw_ref, dng_ref, wout_ref, x_ref, gx_ref, wq_ref, k_ref, v_ref,
                  wo_ref, gf_ref, h2_ref, f_ref):
    lhs = []
    for rows in batches:
        parts = []
        for hd in range(DN_HEADS):
            cols = slice(hd * HEAD_DIM, (hd + 1) * HEAD_DIM)
            z = z_ref[rows, cols]
            y = _rms(of_ref[rows, cols] + ob_ref[rows, cols], dng_ref[...]) * (z * jax.nn.sigmoid(z))
            parts.append(y.astype(BF16))
        parts.append(sw_ref[rows, :])
        lhs.append(jnp.concatenate(parts, axis=1))
    h1 = [x_ref[rows, :] + _dot(a, wout_ref[...]) for rows, a in zip(batches, lhs)]
    q = [_dot(_rms(h, gx_ref[...]).astype(BF16), wq_ref[...]) for h in h1]
    att = []
    for qb in q:
        heads = []
        for hd in range(MEM_HEADS):
            cols = slice(hd * HEAD_DIM, (hd + 1) * HEAD_DIM)
            s = _dot_nt(qb[:, cols].astype(BF16), k_ref[:, cols]) * scale
            p = jnp.exp(s - jnp.max(s, axis=1, keepdims=True))
            den = jnp.sum(p, axis=1, keepdims=True)
            heads.append((_dot(p.astype(BF16), v_ref[:, cols]) / den).astype(BF16))
        att.append(jnp.concatenate(heads, axis=1))
    h2 = [h + _dot(a, wo_ref[...]) for h, a in zip(h1, att)]
    for rows, h in zip(batches, h2):
        h2_ref[rows, :] = h
        f_ref[rows, :] = _rms(h, gf_ref[...]).astype(f_ref.dtype)


def _mix_mem(o_f, o_b, proj, y_sw, dn_g, w_out, x, gx, wq, k, v, wo, gf, *, tm):
    s, d = x.shape
    dn = DN_HEADS * HEAD_DIM
    mix = w_out.shape[0]
    row = lambda width, col=0: pl.BlockSpec((tm, width), lambda i: (i, col))
    const = lambda a: pl.BlockSpec(a.shape, lambda i: (0, 0), pipeline_mode=pl.Buffered(1))
    dn_g, gx, gf = dn_g.reshape(1, HEAD_DIM), gx.reshape(1, d), gf.reshape(1, d)
    return pl.pallas_call(
        _mix_mem_kernel,
        grid=(_steps(s, tm),),
        in_specs=[row(dn), row(dn), row(dn, Z_OFF // dn), row(mix - dn), const(dn_g), const(w_out), row(d),
                  const(gx), const(wq), const(k), const(v), const(wo), const(gf)],
        out_specs=[row(d), row(d)],
        out_shape=[jax.ShapeDtypeStruct((s, d), F32), jax.ShapeDtypeStruct((s, d), BF16)],
        compiler_params=_params("parallel"),
        name="mix_mem",
    )(o_f, o_b, proj, y_sw, dn_g, w_out, x, gx, wq, k, v, wo, gf)


def _glu_kernel(f_ref, wg_ref, wu_ref, wd_ref, o_ref, wd16_ref):
    f = f_ref[...]
    a = _dot(f, wg_ref[...].astype(BF16))
    o_ref[...] = (a * jax.nn.sigmoid(a) * _dot(f, wu_ref[...].astype(BF16))).astype(o_ref.dtype)
    wd16_ref[...] = wd_ref[...].astype(wd16_ref.dtype)


def _ffn_glu(f, wg, wu, wd, *, tm, tn):
    s, d = f.shape
    dff = wg.shape[1]
    ni, nj = _steps(s, tm), _steps(dff, tn)
    wd_rows = _steps(wd.shape[0], ni * nj)
    wd_spec = pl.BlockSpec((wd_rows, wd.shape[1]), lambda i, j: (i * nj + j, 0))
    return pl.pallas_call(
        _glu_kernel,
        grid=(ni, nj),
        in_specs=[pl.BlockSpec((tm, d), lambda i, j: (i, 0)),
                  pl.BlockSpec((d, tn), lambda i, j: (0, j)),
                  pl.BlockSpec((d, tn), lambda i, j: (0, j)),
                  wd_spec],
        out_specs=[pl.BlockSpec((tm, tn), lambda i, j: (i, j)), wd_spec],
        out_shape=[jax.ShapeDtypeStruct((s, dff), BF16), jax.ShapeDtypeStruct(wd.shape, BF16)],
        compiler_params=_params("parallel", "parallel"),
        name="ffn_glu",
    )(f, wg, wu, wd)


def _down_kernel(a_ref, w_ref, h_ref, g_ref, o_ref, acc_ref):
    k = pl.program_id(1)

    @pl.when(k == 0)
    def _():
        acc_ref[...] = h_ref[...]

    acc_ref[...] += _dot(a_ref[...], w_ref[...])

    @pl.when(k == pl.num_programs(1) - 1)
    def _():
        o_ref[...] = _rms(acc_ref[...], g_ref[...])


def _ffn_down(act, wd, h, g, *, tm, tk):
    s, dff = act.shape
    d = wd.shape[1]
    return pl.pallas_call(
        _down_kernel,
        grid=(_steps(s, tm), _steps(dff, tk)),
        in_specs=[pl.BlockSpec((tm, tk), lambda i, k: (i, k)),
                  pl.BlockSpec((tk, d), lambda i, k: (k, 0)),
                  pl.BlockSpec((tm, d), lambda i, k: (i, 0)),
                  pl.BlockSpec((1, d), lambda i, k: (0, 0))],
        out_specs=pl.BlockSpec((tm, d), lambda i, k: (i, 0)),
        out_shape=jax.ShapeDtypeStruct((s, d), F32),
        scratch_shapes=[pltpu.VMEM((tm, d), F32)],
        compiler_params=_params("parallel", "arbitrary"),
        name="ffn_down",
    )(act, wd, h, g.reshape(1, d))


def _gate_row(fwd, bwd):
    row = jnp.zeros((1, LANES), F32)
    row = row.at[0, GATE_G_F:GATE_G_F + DN_HEADS].set(fwd.astype(F32))
    return row.at[0, GATE_G_B:GATE_G_B + DN_HEADS].set(bwd.astype(F32))


def _pick(n, *cands):
    for c in cands:
        if n % c == 0:
            return c
    return n


def kernel(x, mem, norm_mix_g, w_in, conv_w, a_log_f, a_log_b, dt_bias_f, dt_bias_b, dn_norm_g, attn_sink, rel_bias, w_out, norm_x_g, norm_mem_g, w_q_mem, w_kv_mem, w_o_mem, norm_ffn_g, w_gate, w_up, w_down, norm_final_g):
    batch, s, d = x.shape
    assert batch == 1 and mem.shape[0] == 1 and w_in.shape[0] == 1, "single sequence, single layer"
    w = WINDOW
    rel = (jnp.arange(3 * w)[None, :] - w) - jnp.arange(w)[:, None]
    bucket = _t5_bucket(rel).astype(jnp.int32)
    mem_dim = MEM_HEADS * HEAD_DIM

    h = x.reshape(s, d)
    mem2 = mem.reshape(mem.shape[1], d)
    tm_big = _pick(s, 1024, 512, 256, 128)
    tm_mid = _pick(s, 512, 256, 128)
    w_r = _w_in_layout(w_in[0].T, cols=256)
    proj = _rms_matmul(h, norm_mix_g[0], w_r, tm=tm_big, tn=_pick(PROJ_WIDTH, 1920, 1152, 640, 128), name="in_proj",
                       w_transposed=True)

    q, k, v, gates, gates_t = _dn_prep(proj, conv_w[0], _gate_row(a_log_f[0], a_log_b[0]),
                                       _gate_row(dt_bias_f[0], dt_bias_b[0]), rows=_pick(s, 256, 128))
    o_f, o_b = _deltanet(q, k, v, gates, gates_t, rows=_pick(s, 512, 256, 128), heads=8)
    y_sw = _swa(proj, bucket, rel_bias.astype(F32), attn_sink[0].astype(F32), qblk=_pick(s // w, 4, 2, 1))
    kv = _rms_matmul(mem2, norm_mem_g[0], w_kv_mem[0].astype(BF16), tm=mem2.shape[0], tn=2 * mem_dim,
                     name="mem_kv").astype(BF16)
    h, f = _mix_mem(o_f, o_b, proj, y_sw, dn_norm_g[0], w_out[0].astype(BF16), h, norm_x_g[0],
                    w_q_mem[0].astype(BF16), kv[:, :mem_dim], kv[:, mem_dim:], w_o_mem[0].astype(BF16),
                    norm_ffn_g[0], tm=_pick(s, 512, 256))

    act, w_down16 = _ffn_glu(f, w_gate[0], w_up[0], w_down[0], tm=tm_big, tn=512)
    out = _ffn_down(act, w_down16, h, norm_final_g, tm=tm_mid, tk=2816)
    return out.reshape(batch, s, d)
```

```python
import functools
import math

import jax
import jax.numpy as jnp
from jax import lax
from jax.experimental import pallas as pl
from jax.experimental.pallas import tpu as pltpu

F32 = jnp.float32
BF16 = jnp.bfloat16

RMS_EPS = 1e-6
L2_EPS = 1e-6
HEAD_DIM = 128
DN_HEADS = 8
DN_CHUNK = 64
DN_SUB = 16
CONV_WIDTH = 5
CONV_HALO = 8
SWA_HEADS = 8
SWA_KV_HEADS = 2
WINDOW = 128
NUM_BUCKETS = 32
MAX_DISTANCE = 128
MEM_HEADS = 4
LOG2E = 1.4426950408889634
LANES = 128
ROW_BATCH = 128

DN_QKV = 3 * DN_HEADS * HEAD_DIM
Z_OFF = DN_QKV
QSW_OFF = Z_OFF + DN_HEADS * HEAD_DIM
KSW_OFF = QSW_OFF + SWA_HEADS * HEAD_DIM
VSW_OFF = KSW_OFF + SWA_KV_HEADS * HEAD_DIM
GATE_OFF = VSW_OFF + SWA_KV_HEADS * HEAD_DIM
PROJ_WIDTH = GATE_OFF + LANES
GATE_BETA_F, GATE_BETA_B, GATE_G_F, GATE_G_B = 0, DN_HEADS, 2 * DN_HEADS, 3 * DN_HEADS
GATE_ROWS = 4 * DN_HEADS

VMEM_LIMIT_V7X = 56 * 1024 * 1024


def _params(*sem):
    return pltpu.CompilerParams(dimension_semantics=sem, vmem_limit_bytes=VMEM_LIMIT_V7X)


def _steps(dim, tile):
    assert dim % tile == 0, f"tile {tile} does not divide {dim}"
    return dim // tile


def _dot(a, b):
    return jnp.dot(a, b, preferred_element_type=F32)


def _dot_nt(a, b):
    return lax.dot_general(a, b, (((1,), (1,)), ((), ())), preferred_element_type=F32)


def _dot_tn(a, b):
    return lax.dot_general(a, b, (((0,), (0,)), ((), ())), preferred_element_type=F32)


def _rms(x, g):
    return x * lax.rsqrt(jnp.mean(x * x, axis=-1, keepdims=True) + RMS_EPS) * g


def _w_in_kernel(w_ref, o_ref):
    gate_lo = QSW_OFF
    o_ref[:gate_lo, :] = w_ref[:gate_lo, :].astype(o_ref.dtype)
    o_ref[gate_lo:GATE_OFF, :] = w_ref[gate_lo + GATE_ROWS:, :].astype(o_ref.dtype)
    o_ref[GATE_OFF:GATE_OFF + GATE_ROWS, :] = w_ref[gate_lo:gate_lo + GATE_ROWS, :].astype(o_ref.dtype)
    o_ref[GATE_OFF + GATE_ROWS:, :] = jnp.zeros((LANES - GATE_ROWS, o_ref.shape[1]), o_ref.dtype)


def _w_in_layout(w_t, *, cols):
    n, k = w_t.shape
    assert n == PROJ_WIDTH - (LANES - GATE_ROWS)
    return pl.pallas_call(
        _w_in_kernel,
        grid=(_steps(k, cols),),
        in_specs=[pl.BlockSpec((n, cols), lambda i: (0, i))],
        out_specs=pl.BlockSpec((PROJ_WIDTH, cols), lambda i: (0, i)),
        out_shape=jax.ShapeDtypeStruct((PROJ_WIDTH, k), BF16),
        compiler_params=_params("parallel"),
        name="w_in_layout",
    )(w_t)


def _rms_matmul_kernel(x_ref, g_ref, w_ref, o_ref, n_ref, *, w_transposed):
    dot = _dot_nt if w_transposed else _dot

    @pl.when(pl.program_id(1) == 0)
    def _():
        half = max(x_ref.shape[0] // 2, ROW_BATCH)
        for r in range(0, x_ref.shape[0], half):
            rows = slice(r, r + half)
            n = _rms(x_ref[rows, :], g_ref[...]).astype(n_ref.dtype)
            n_ref[rows, :] = n
            o_ref[rows, :] = dot(n, w_ref[...]).astype(o_ref.dtype)

    @pl.when(pl.program_id(1) != 0)
    def _():
        o_ref[...] = dot(n_ref[...], w_ref[...]).astype(o_ref.dtype)


def _rms_matmul(x, g, w, *, tm, tn, name, w_transposed=False):
    m, k = x.shape
    n = w.shape[0] if w_transposed else w.shape[1]
    w_spec = pl.BlockSpec((tn, k), lambda i, j: (j, 0)) if w_transposed else pl.BlockSpec((k, tn), lambda i, j: (0, j))
    return pl.pallas_call(
        functools.partial(_rms_matmul_kernel, w_transposed=w_transposed),
        grid=(_steps(m, tm), _steps(n, tn)),
        in_specs=[pl.BlockSpec((tm, k), lambda i, j: (i, 0)),
                  pl.BlockSpec((1, k), lambda i, j: (0, 0)),
                  w_spec],
        out_specs=pl.BlockSpec((tm, tn), lambda i, j: (i, j)),
        out_shape=jax.ShapeDtypeStruct((m, n), F32),
        scratch_shapes=[pltpu.VMEM((tm, k), BF16)],
        compiler_params=_params("parallel", "arbitrary"),
        name=name,
    )(x, g.reshape(1, k), w)


def _split3(x):
    hi = x.astype(BF16)
    r = x - hi.astype(F32)
    mid = r.astype(BF16)
    lo = (r - mid.astype(F32)).astype(BF16)
    return hi, mid, lo


def _prep_kernel(main_ref, prev_ref, next_ref, gate_ref, cw_ref, alog_ref, dt_ref,
                 q_ref, k_ref, v_ref, go_ref, gt_ref, ext_ref):
    i = pl.program_id(0)
    rows = main_ref.shape[0]
    halo = CONV_HALO
    pad = (CONV_WIDTH - 1) // 2

    ext_ref[0:halo, :] = jnp.where(i > 0, prev_ref[...], 0.0)
    ext_ref[halo:halo + rows, :] = main_ref[...]
    ext_ref[halo + rows:, :] = jnp.where(i < pl.num_programs(0) - 1, next_ref[...], 0.0)

    for s in range(3 * DN_HEADS):
        cols = slice(s * HEAD_DIM, (s + 1) * HEAD_DIM)
        xe = ext_ref[:, cols]
        acc = cw_ref[pad:pad + 1, cols] * xe[halo:halo + rows]
        for j in range(CONV_WIDTH):
            if j != pad:
                shifted = pltpu.roll(xe, shift=(pad - j) % xe.shape[0], axis=0)[halo:halo + rows]
                acc = acc + cw_ref[j:j + 1, cols] * shifted
        y = acc * jax.nn.sigmoid(acc)
        if s < 2 * DN_HEADS:
            inv = lax.rsqrt(jnp.sum(y * y, axis=-1, keepdims=True) + L2_EPS)
            y = y * (inv * (HEAD_DIM ** -0.5) if s < DN_HEADS else inv)
        if s < DN_HEADS:
            q_ref[:, cols] = y
        elif s < 2 * DN_HEADS:
            k_ref[:, slice((s - DN_HEADS) * HEAD_DIM, (s - DN_HEADS + 1) * HEAD_DIM)] = y
        else:
            v_ref[:, slice((s - 2 * DN_HEADS) * HEAD_DIM, (s - 2 * DN_HEADS + 1) * HEAD_DIM)] = y

    t = gate_ref[...]
    beta = jax.nn.sigmoid(t)
    a = t + dt_ref[...]
    softplus = jnp.maximum(a, 0.0) + jnp.log1p(jnp.exp(-jnp.abs(a)))
    g = -jnp.exp(alog_ref[...]) * softplus

    ri = lax.broadcasted_iota(jnp.int32, (rows, rows), 0)
    ci = lax.broadcasted_iota(jnp.int32, (rows, rows), 1)
    shift = DN_CHUNK.bit_length() - 1
    same_chunk = (ri >> shift) == (ci >> shift)
    lower = jnp.where(same_chunk & (ci <= ri), 1.0, 0.0).astype(BF16)
    upper = jnp.where(same_chunk & (ci >= ri), 1.0, 0.0).astype(BF16)
    parts = _split3(g)
    gc_f = _dot(lower, parts[0]) + _dot(lower, parts[1]) + _dot(lower, parts[2])
    gc_b = _dot(upper, parts[0]) + _dot(upper, parts[1]) + _dot(upper, parts[2])

    col = lax.broadcasted_iota(jnp.int32, t.shape, 1)
    out = jnp.where(col < GATE_G_F, beta, jnp.where(col < GATE_G_B, gc_f, gc_b))
    go_ref[...] = out
    gt_ref[...] = out.T[0:GATE_ROWS, :]


def _dn_prep(proj, conv_w, alog_row, dt_row, *, rows):
    s = proj.shape[0]
    nblk = _steps(s, rows)
    hb = rows // CONV_HALO
    last_halo = s // CONV_HALO - 1
    head_cols = DN_HEADS * HEAD_DIM
    return pl.pallas_call(
        _prep_kernel,
        grid=(nblk,),
        in_specs=[pl.BlockSpec((rows, DN_QKV), lambda i: (i, 0)),
                  pl.BlockSpec((CONV_HALO, DN_QKV), lambda i: (jnp.maximum(i * hb - 1, 0), 0)),
                  pl.BlockSpec((CONV_HALO, DN_QKV), lambda i: (jnp.minimum((i + 1) * hb, last_halo), 0)),
                  pl.BlockSpec((rows, LANES), lambda i: (i, GATE_OFF // LANES)),
                  pl.BlockSpec((CONV_WIDTH, DN_QKV), lambda i: (0, 0)),
                  pl.BlockSpec((1, LANES), lambda i: (0, 0)),
                  pl.BlockSpec((1, LANES), lambda i: (0, 0))],
        out_specs=[pl.BlockSpec((rows, head_cols), lambda i: (i, 0)),
                   pl.BlockSpec((rows, head_cols), lambda i: (i, 0)),
                   pl.BlockSpec((rows, head_cols), lambda i: (i, 0)),
                   pl.BlockSpec((rows, LANES), lambda i: (i, 0)),
                   pl.BlockSpec((GATE_ROWS, rows), lambda i: (0, i))],
        out_shape=[jax.ShapeDtypeStruct((s, head_cols), F32),
                   jax.ShapeDtypeStruct((s, head_cols), F32),
                   jax.ShapeDtypeStruct((s, head_cols), F32),
                   jax.ShapeDtypeStruct((s, LANES), F32),
                   jax.ShapeDtypeStruct((GATE_ROWS, s), F32)],
        scratch_shapes=[pltpu.VMEM((rows + 2 * CONV_HALO, DN_QKV), F32)],
        compiler_params=_params("parallel"),
        name="dn_prep",
    )(proj, proj, proj, proj, conv_w, alog_row, dt_row)


def _unit_tri_inverse(ms, eye, same_sub, _mm):
    c = eye.shape[0]
    mds = [jnp.where(same_sub, m, 0.0) for m in ms]
    es = [m - md for m, md in zip(ms, mds)]
    pw = [_mm(md, md) for md in mds]
    xs = [eye - md for md in mds]
    for _ in range(DN_SUB.bit_length() - 3):
        both = [_mm(jnp.concatenate([a, x], axis=0), a) for a, x in zip(pw, xs)]
        pw = [b[:c] for b in both]
        xs = [x + b[c:] for x, b in zip(xs, both)]
    xs = [x + _mm(x, a) for x, a in zip(xs, pw)]
    ns = [_mm(x, e) for x, e in zip(xs, es)]
    n2 = [_mm(n, n) for n in ns]
    ys = [_mm(n, (eye - n) + b) for n, b in zip(ns, n2)]
    return [x - _mm(y, x) for x, y in zip(xs, ys)]


def _deltanet_kernel(qf, kf, vf, gf, gtf, qb, kb, vb, gb, gtb, of_ref, ob_ref, state_ref):
    c_len = DN_CHUNK
    nchunk = qf.shape[0] // c_len
    heads = qf.shape[1] // HEAD_DIM
    head0 = pl.program_id(0) * heads

    @pl.when(pl.program_id(1) == 0)
    def _():
        state_ref[...] = jnp.zeros_like(state_ref)

    p_len = 2 * c_len
    ri = lax.broadcasted_iota(jnp.int32, (c_len, p_len), 0)
    cl = lax.broadcasted_iota(jnp.int32, (c_len, p_len), 1)
    ci = cl & (c_len - 1)
    first = cl < c_len
    eye = jnp.where(ri == ci, 1.0, 0.0)
    sub_shift = DN_SUB.bit_length() - 1
    same_sub = (ri >> sub_shift) == (ci >> sub_shift)
    lane = lax.broadcasted_iota(jnp.int32, (p_len, LANES), 1)
    upper_rows = lax.broadcasted_iota(jnp.int32, (p_len, 1), 0) >= c_len
    first_tall = lax.broadcasted_iota(jnp.int32, (c_len + HEAD_DIM, p_len), 1) < c_len
    masks = {False: (ci <= ri, ci < ri), True: (ci >= ri, ci > ri)}
    refs = {False: (qf, kf, vf, gf, gtf, of_ref), True: (qb, kb, vb, gb, gtb, ob_ref)}

    def pair_mm(x, p):
        blockdiag = jnp.concatenate([jnp.where(first, p, 0.0), jnp.where(first, 0.0, p)], axis=0)
        return _dot(x.astype(BF16), blockdiag.astype(BF16))

    chains = [(rev, hd) for rev in (False, True) for hd in range(heads)]

    g_rows = {}
    for rev, hd in chains:
        g_col = head0 + hd + (GATE_G_B if rev else GATE_G_F)
        g_rows[rev, hd] = refs[rev][4][pl.ds(g_col, 1), :]

    def load(rev, hd, p):
        q_ref, k_ref, v_ref, g_ref = refs[rev][:4]
        rows = slice(p * p_len, (p + 1) * p_len)
        cols = slice(hd * HEAD_DIM, (hd + 1) * HEAD_DIM)
        gates = g_ref[rows, :]
        beta_col = head0 + hd + (GATE_BETA_B if rev else GATE_BETA_F)
        g_col = head0 + hd + (GATE_G_B if rev else GATE_G_F)
        beta = jnp.sum(jnp.where(lane == beta_col, gates, 0.0), axis=1, keepdims=True)
        gcol = jnp.sum(jnp.where(lane == g_col, gates, 0.0), axis=1, keepdims=True)
        grow = g_rows[rev, hd][:, rows]
        if rev:
            glast = (grow[:, 0:1], grow[:, c_len:c_len + 1])
        else:
            glast = (grow[:, c_len - 1:c_len], grow[:, p_len - 1:p_len])
        return q_ref[rows, cols], k_ref[rows, cols], v_ref[rows, cols], beta, gcol, grow, glast

    lo, hi = slice(0, c_len), slice(c_len, p_len)

    zeros = jnp.zeros((c_len, HEAD_DIM), BF16)

    def phase_a(inst):
        data = [load(*i) for i in inst]
        kbeta = [k * beta for (_, k, _, beta, _, _, _) in data]
        prod = []
        for kb_, (q, k, _, _, _, _, _) in zip(kbeta, data):
            lhs = jnp.concatenate([jnp.concatenate([kb_[lo], kb_[hi]], axis=1),
                                   jnp.concatenate([q[lo], q[hi]], axis=1)], axis=0).astype(BF16)
            kb16 = k.astype(BF16)
            rhs = jnp.concatenate([jnp.concatenate([kb16[lo], zeros], axis=1),
                                   jnp.concatenate([zeros, kb16[hi]], axis=1)], axis=0)
            prod.append(_dot_nt(lhs, rhs))
        decay = [jnp.exp(jnp.where(masks[rev][0], jnp.where(first, gcol[lo], gcol[hi]) - grow, -jnp.inf))
                 for (rev, _, _), (_, _, _, _, gcol, grow, _) in zip(inst, data)]
        ms = [jnp.where(masks[rev][1], p[lo] * dec, 0.0) for (rev, _, _), p, dec in zip(inst, prod, decay)]
        a_mat = [p[hi] * dec for p, dec in zip(prod, decay)]
        ts = _unit_tri_inverse(ms, eye, same_sub, pair_mm)
        egs = [jnp.exp(gcol) for (_, _, _, _, gcol, _, _) in data]
        uw = []
        for t, kb_, eg, (_, _, v, beta, _, _, _) in zip(ts, kbeta, egs, data):
            lhs = jnp.concatenate([jnp.where(first, t, 0.0), jnp.where(first, 0.0, t)], axis=0).astype(BF16)
            rhs = jnp.concatenate([(v * beta).astype(BF16), (kb_ * eg).astype(BF16)], axis=1)
            uw.append(_dot(lhs, rhs))
        qdec = [q * eg for eg, (q, _, _, _, _, _, _) in zip(egs, data)]
        kdec_t = [(k * jnp.exp(jnp.where(upper_rows, glast[1], glast[0]) - gcol)).T
                  for (_, k, _, _, gcol, _, glast) in data]
        tall = [jnp.concatenate([a_, kt], axis=0) for a_, kt in zip(a_mat, kdec_t)]
        pre = {}
        for (rev, hd, p), x, qd, tl, (_, _, _, _, _, _, glast) in zip(inst, uw, qdec, tall, data):
            for half, rs in enumerate((lo, hi)):
                wq = jnp.concatenate([x[rs, HEAD_DIM:], qd[rs]], axis=0).astype(BF16)
                keep = first_tall if half == 0 else ~first_tall
                pre[rev, hd, 2 * p + half] = (x[rs, :HEAD_DIM], wq, jnp.where(keep, tl, 0.0).astype(BF16),
                                              jnp.exp(glast[half]))
        return pre

    def phase_b(states, pre, t):
        chunk = {False: t, True: nchunk - 1 - t}
        cur = [pre[rev, hd, chunk[rev]] for rev, hd in chains]
        ws = [_dot(wq, s_.astype(BF16)) for (_, wq, _, _), s_ in zip(cur, states)]
        v_new = [(u - x[lo]).astype(BF16) for (u, _, _, _), x in zip(cur, ws)]
        upd = [_dot(tl, jnp.concatenate([vn, vn], axis=0)) for (_, _, tl, _), vn in zip(cur, v_new)]
        for (rev, hd), x, y in zip(chains, ws, upd):
            c = chunk[rev]
            refs[rev][5][c * c_len:(c + 1) * c_len, hd * HEAD_DIM:(hd + 1) * HEAD_DIM] = x[hi] + y[lo]
        return [s_ * g_ + x[c_len:] for (_, _, _, g_), s_, x in zip(cur, states, upd)]

    npair = nchunk // 2
    states = [state_ref[n] for n in range(len(chains))]
    for s in range(npair):
        pre = phase_a([(rev, hd, npair - 1 - s if rev else s) for rev, hd in chains])
        for t in (2 * s, 2 * s + 1):
            states = phase_b(states, pre, t)
    for n, s_ in enumerate(states):
        state_ref[n] = s_


def _deltanet(q, k, v, gates, gates_t, *, rows, heads):
    s = q.shape[0]
    nb = _steps(s, rows)
    assert rows % (2 * DN_CHUNK) == 0
    fwd = lambda h, b: (b, h)
    bwd = lambda h, b: (nb - 1 - b, h)
    head = lambda im: pl.BlockSpec((rows, heads * HEAD_DIM), im)
    gate = lambda im: pl.BlockSpec((rows, LANES), lambda h, b: (im(h, b)[0], 0))
    gate_t = lambda im: pl.BlockSpec((GATE_ROWS, rows), lambda h, b: (0, im(h, b)[0]))
    return pl.pallas_call(
        _deltanet_kernel,
        grid=(_steps(DN_HEADS, heads), nb),
        in_specs=[head(fwd), head(fwd), head(fwd), gate(fwd), gate_t(fwd),
                  head(bwd), head(bwd), head(bwd), gate(bwd), gate_t(bwd)],
        out_specs=[head(fwd), head(bwd)],
        out_shape=[jax.ShapeDtypeStruct(q.shape, F32), jax.ShapeDtypeStruct(q.shape, F32)],
        scratch_shapes=[pltpu.VMEM((2 * heads, HEAD_DIM, HEAD_DIM), F32)],
        compiler_params=_params("parallel", "arbitrary"),
        name="deltanet",
    )(q, k, v, gates, gates_t, q, k, v, gates, gates_t)


def _t5_bucket(rel):
    nb = NUM_BUCKETS // 2
    max_exact = nb // 2
    n = jnp.abs(rel)
    large = max_exact + (jnp.log(jnp.maximum(n, max_exact).astype(F32) / max_exact)
                         / math.log(MAX_DISTANCE / max_exact) * (nb - max_exact)).astype(jnp.int32)
    large = jnp.minimum(large, nb - 1)
    return jnp.where(rel > 0, nb, 0) + jnp.where(n < max_exact, n, large)


def _swa_bias_init(bucket_ref, rb_ref, bias_ref):
    w = WINDOW
    ri = lax.broadcasted_iota(jnp.int32, (w, 3 * w), 0)
    ci = lax.broadcasted_iota(jnp.int32, (w, 3 * w), 1)
    bucket = bucket_ref[...]
    in_band = jnp.abs(ci - w - ri) <= w
    for hd in range(SWA_HEADS):
        acc = jnp.zeros((w, 3 * w), F32)
        for b in range(NUM_BUCKETS):
            acc = jnp.where(bucket == b, rb_ref[b, hd], acc)
        bias_ref[hd] = jnp.where(in_band, acc * LOG2E, -jnp.inf)


def _swa_block(t, qblk, first_row, seq, q_ref, kext, vext, bias_ref, sinks):
    w = WINDOW
    group = SWA_HEADS // SWA_KV_HEADS
    heads = range(SWA_HEADS)
    rows = slice(t * w, (t + 1) * w)
    band = slice(t * w, (t + 3) * w)
    s = [_dot_nt(q_ref[rows, hd * HEAD_DIM:(hd + 1) * HEAD_DIM].astype(BF16), kext[hd // group][band])
         * (HEAD_DIM ** -0.5 * LOG2E) + bias_ref[hd] for hd in heads]
    if t == 0 or t == qblk - 1:
        key_pos = first_row + (t - 1) * w + lax.broadcasted_iota(jnp.int32, (w, 3 * w), 1)
        if t == 0:
            s = [jnp.where(key_pos >= 0, x, -jnp.inf) for x in s]
        if t == qblk - 1:
            s = [jnp.where(key_pos < seq, x, -jnp.inf) for x in s]
    mx = [jnp.maximum(jnp.max(x, axis=1, keepdims=True), sk) for x, sk in zip(s, sinks)]
    p = [jnp.exp2(x - m) for x, m in zip(s, mx)]
    den = [jnp.sum(x, axis=1, keepdims=True) + jnp.exp2(sk - m) for x, sk, m in zip(p, sinks, mx)]
    o = [_dot(x.astype(BF16), vext[hd // group][band]) / dn for hd, x, dn in zip(heads, p, den)]
    return jnp.concatenate([x.astype(BF16) for x in o], axis=1)


def _mix_mem_kernel(of_ref, ob_ref, z_ref, qs_ref, kp_ref, kc_ref, kn_ref, vp_ref, vc_ref, vn_ref, bucket_ref, rb_ref,
                    sink_ref, dng_ref, wout_ref, x_ref, gx_ref, wq_ref, k_ref, v_ref, wo_ref, gf_ref,
                    h2_ref, f_ref, bias_ref, *, seq):
    n = pl.program_id(0)
    rows_per_step = h2_ref.shape[0]
    qblk = rows_per_step // WINDOW

    @pl.when(n == 0)
    def _():
        _swa_bias_init(bucket_ref, rb_ref, bias_ref)

    kext, vext = [], []
    for kvh in range(SWA_KV_HEADS):
        kcols = slice(kvh * HEAD_DIM, (kvh + 1) * HEAD_DIM)
        kext.append(jnp.concatenate([kp_ref[:, kcols], kc_ref[:, kcols], kn_ref[:, kcols]], axis=0).astype(BF16))
        vext.append(jnp.concatenate([vp_ref[:, kcols], vc_ref[:, kcols], vn_ref[:, kcols]], axis=0).astype(BF16))
    sinks = [sink_ref[hd] * LOG2E for hd in range(SWA_HEADS)]
    for t0 in range(0, qblk, 2):
        blocks = range(t0, min(t0 + 2, qblk))
        y_sw = [_swa_block(t, qblk, n * rows_per_step, seq, qs_ref, kext, vext, bias_ref, sinks) for t in blocks]
        _mix_mem_rows([slice(t * WINDOW, (t + 1) * WINDOW) for t in blocks], y_sw, HEAD_DIM ** -0.5, of_ref, ob_ref,
                      z_ref, dng_ref, wout_ref, x_ref, gx_ref, wq_ref, k_ref, v_ref, wo_ref, gf_ref, h2_ref, f_ref)


def _mix_mem_rows(batches, y_sw, scale, of_ref, ob_ref, z_ref, dng_ref, wout_ref, x_ref, gx_ref, wq_ref, k_ref, v_ref,
                  wo_ref, gf_ref, h2_ref, f_ref):
    dn = DN_HEADS * HEAD_DIM
    y_dn = []
    for rows in batches:
        parts = []
        for hd in range(DN_HEADS):
            cols = slice(hd * HEAD_DIM, (hd + 1) * HEAD_DIM)
            z = z_ref[rows, cols]
            y = _rms(of_ref[rows, cols] + ob_ref[rows, cols], dng_ref[...]) * (z * jax.nn.sigmoid(z))
            parts.append(y.astype(BF16))
        y_dn.append(jnp.concatenate(parts, axis=1))
    h1 = [x_ref[rows, :] + _dot(a, wout_ref[:dn, :]) for rows, a in zip(batches, y_dn)]
    h1 = [h + _dot(sw, wout_ref[dn:, :]) for h, sw in zip(h1, y_sw)]
    q = [_dot(_rms(h, gx_ref[...]).astype(BF16), wq_ref[...]) for h in h1]
    att = []
    for qb in q:
        heads = []
        for hd in range(MEM_HEADS):
            cols = slice(hd * HEAD_DIM, (hd + 1) * HEAD_DIM)
            s = _dot_nt(qb[:, cols].astype(BF16), k_ref[:, cols]) * scale
            p = jnp.exp(s - jnp.max(s, axis=1, keepdims=True))
            den = jnp.sum(p, axis=1, keepdims=True)
            heads.append((_dot(p.astype(BF16), v_ref[:, cols]) / den).astype(BF16))
        att.append(jnp.concatenate(heads, axis=1))
    h2 = [h + _dot(a, wo_ref[...]) for h, a in zip(h1, att)]
    for rows, h in zip(batches, h2):
        h2_ref[rows, :] = h
        f_ref[rows, :] = _rms(h, gf_ref[...]).astype(f_ref.dtype)


def _mix_mem(o_f, o_b, proj, bucket, rel_bias, sink, dn_g, w_out, x, gx, wq, k, v, wo, gf, *, tm):
    s, d = x.shape
    w = WINDOW
    dn = DN_HEADS * HEAD_DIM
    qw = SWA_HEADS * HEAD_DIM
    kvw = SWA_KV_HEADS * HEAD_DIM
    qblk = _steps(tm, w)
    nb = _steps(s, w)
    row = lambda width, col=0: pl.BlockSpec((tm, width), lambda i: (i, col))
    halo = lambda off, lo: pl.BlockSpec(
        (w, kvw), lambda i: (jnp.clip(i * qblk - 1 if lo else (i + 1) * qblk, 0, nb - 1), off // kvw))
    const = lambda a: pl.BlockSpec(a.shape, lambda i: (0, 0), pipeline_mode=pl.Buffered(1))
    smem = pl.BlockSpec(memory_space=pltpu.SMEM)
    dn_g, gx, gf = dn_g.reshape(1, HEAD_DIM), gx.reshape(1, d), gf.reshape(1, d)
    return pl.pallas_call(
        functools.partial(_mix_mem_kernel, seq=s),
        grid=(_steps(s, tm),),
        in_specs=[row(dn), row(dn), row(dn, Z_OFF // dn), row(qw, QSW_OFF // qw),
                  halo(KSW_OFF, True), row(kvw, KSW_OFF // kvw), halo(KSW_OFF, False),
                  halo(VSW_OFF, True), row(kvw, VSW_OFF // kvw), halo(VSW_OFF, False),
                  const(bucket), smem, smem, const(dn_g), const(w_out), row(d),
                  const(gx), const(wq), const(k), const(v), const(wo), const(gf)],
        out_specs=[row(d), row(d)],
        out_shape=[jax.ShapeDtypeStruct((s, d), F32), jax.ShapeDtypeStruct((s, d), BF16)],
        scratch_shapes=[pltpu.VMEM((SWA_HEADS, w, 3 * w), F32)],
        compiler_params=_params("arbitrary"),
        name="mix_mem",
    )(o_f, o_b, proj, proj, proj, proj, proj, proj, proj, proj, bucket, rel_bias, sink, dn_g, w_out, x,
      gx, wq, k, v, wo, gf)


def _glu_kernel(f_ref, wg_ref, wu_ref, wd_ref, o_ref, wd16_ref):
    f = f_ref[...]
    a = _dot(f, wg_ref[...].astype(BF16))
    o_ref[...] = (a * jax.nn.sigmoid(a) * _dot(f, wu_ref[...].astype(BF16))).astype(o_ref.dtype)
    wd16_ref[...] = wd_ref[...].astype(wd16_ref.dtype)


def _ffn_glu(f, wg, wu, wd, *, tm, tn):
    s, d = f.shape
    dff = wg.shape[1]
    ni, nj = _steps(s, tm), _steps(dff, tn)
    wd_rows = _steps(wd.shape[0], ni * nj)
    wd_spec = pl.BlockSpec((wd_rows, wd.shape[1]), lambda i, j: (i * nj + j, 0))
    return pl.pallas_call(
        _glu_kernel,
        grid=(ni, nj),
        in_specs=[pl.BlockSpec((tm, d), lambda i, j: (i, 0)),
                  pl.BlockSpec((d, tn), lambda i, j: (0, j)),
                  pl.BlockSpec((d, tn), lambda i, j: (0, j)),
                  wd_spec],
        out_specs=[pl.BlockSpec((tm, tn), lambda i, j: (i, j)), wd_spec],
        out_shape=[jax.ShapeDtypeStruct((s, dff), BF16), jax.ShapeDtypeStruct(wd.shape, BF16)],
        compiler_params=_params("parallel", "parallel"),
        name="ffn_glu",
    )(f, wg, wu, wd)


def _down_kernel(a_ref, w_ref, h_ref, g_ref, o_ref, acc_ref):
    k = pl.program_id(1)

    @pl.when(k == 0)
    def _():
        acc_ref[...] = h_ref[...]

    acc_ref[...] += _dot(a_ref[...], w_ref[...])

    @pl.when(k == pl.num_programs(1) - 1)
    def _():
        o_ref[...] = _rms(acc_ref[...], g_ref[...])


def _ffn_down(act, wd, h, g, *, tm, tk):
    s, dff = act.shape
    d = wd.shape[1]
    return pl.pallas_call(
        _down_kernel,
        grid=(_steps(s, tm), _steps(dff, tk)),
        in_specs=[pl.BlockSpec((tm, tk), lambda i, k: (i, k)),
                  pl.BlockSpec((tk, d), lambda i, k: (k, 0)),
                  pl.BlockSpec((tm, d), lambda i, k: (i, 0)),
                  pl.BlockSpec((1, d), lambda i, k: (0, 0))],
        out_specs=pl.BlockSpec((tm, d), lambda i, k: (i, 0)),
        out_shape=jax.ShapeDtypeStruct((s, d), F32),
        scratch_shapes=[pltpu.VMEM((tm, d), F32)],
        compiler_params=_params("parallel", "arbitrary"),
        name="ffn_down",
    )(act, wd, h, g.reshape(1, d))


def _gate_row(fwd, bwd):
    row = jnp.zeros((1, LANES), F32)
    row = row.at[0, GATE_G_F:GATE_G_F + DN_HEADS].set(fwd.astype(F32))
    return row.at[0, GATE_G_B:GATE_G_B + DN_HEADS].set(bwd.astype(F32))


def _pick(n, *cands):
    for c in cands:
        if n % c == 0:
            return c
    return n


def kernel(x, mem, norm_mix_g, w_in, conv_w, a_log_f, a_log_b, dt_bias_f, dt_bias_b, dn_norm_g, attn_sink, rel_bias, w_out, norm_x_g, norm_mem_g, w_q_mem, w_kv_mem, w_o_mem, norm_ffn_g, w_gate, w_up, w_down, norm_final_g):
    batch, s, d = x.shape
    assert batch == 1 and mem.shape[0] == 1 and w_in.shape[0] == 1, "single sequence, single layer"
    w = WINDOW
    rel = (jnp.arange(3 * w)[None, :] - w) - jnp.arange(w)[:, None]
    bucket = _t5_bucket(rel).astype(jnp.int32)
    mem_dim = MEM_HEADS * HEAD_DIM

    h = x.reshape(s, d)
    mem2 = mem.reshape(mem.shape[1], d)
    tm_big = _pick(s, 1024, 512, 256, 128)
    tm_mid = _pick(s, 512, 256, 128)
    w_r = _w_in_layout(w_in[0].T, cols=256)
    proj = _rms_matmul(h, norm_mix_g[0], w_r, tm=tm_big, tn=_pick(PROJ_WIDTH, 1920, 1152, 640, 128), name="in_proj",
                       w_transposed=True)

    q, k, v, gates, gates_t = _dn_prep(proj, conv_w[0], _gate_row(a_log_f[0], a_log_b[0]),
                                       _gate_row(dt_bias_f[0], dt_bias_b[0]), rows=_pick(s, 256, 128))
    o_f, o_b = _deltanet(q, k, v, gates, gates_t, rows=_pick(s, 512, 256, 128), heads=8)
    kv = _rms_matmul(mem2, norm_mem_g[0], w_kv_mem[0].astype(BF16), tm=mem2.shape[0], tn=2 * mem_dim,
                     name="mem_kv").astype(BF16)
    h, f = _mix_mem(o_f, o_b, proj, bucket, rel_bias.astype(F32), attn_sink[0].astype(F32), dn_norm_g[0],
                    w_out[0].astype(BF16), h, norm_x_g[0], w_q_mem[0].astype(BF16), kv[:, :mem_dim], kv[:, mem_dim:],
                    w_o_mem[0].astype(BF16), norm_ffn_g[0], tm=_pick(s, 256))

    act, w_down16 = _ffn_glu(f, w_gate[0], w_up[0], w_down[0], tm=tm_big, tn=512)
    out = _ffn_down(act, w_down16, h, norm_final_g, tm=tm_mid, tk=2816)
    return out.reshape(batch, s, d)
```

```python
import functools
import math

import jax
import jax.numpy as jnp
from jax import lax
from jax.experimental import pallas as pl
from jax.experimental.pallas import tpu as pltpu

F32 = jnp.float32
BF16 = jnp.bfloat16

RMS_EPS = 1e-6
L2_EPS = 1e-6
HEAD_DIM = 128
DN_HEADS = 8
DN_CHUNK = 64
DN_SUB = 16
CONV_WIDTH = 5
CONV_HALO = 8
SWA_HEADS = 8
SWA_KV_HEADS = 2
WINDOW = 128
NUM_BUCKETS = 32
MAX_DISTANCE = 128
MEM_HEADS = 4
LOG2E = 1.4426950408889634
LANES = 128
ROW_BATCH = 128

DN_QKV = 3 * DN_HEADS * HEAD_DIM
Z_OFF = DN_QKV
QSW_OFF = Z_OFF + DN_HEADS * HEAD_DIM
KSW_OFF = QSW_OFF + SWA_HEADS * HEAD_DIM
VSW_OFF = KSW_OFF + SWA_KV_HEADS * HEAD_DIM
GATE_OFF = VSW_OFF + SWA_KV_HEADS * HEAD_DIM
PROJ_WIDTH = GATE_OFF + LANES
GATE_BETA_F, GATE_BETA_B, GATE_G_F, GATE_G_B = 0, DN_HEADS, 2 * DN_HEADS, 3 * DN_HEADS
GATE_ROWS = 4 * DN_HEADS

VMEM_LIMIT_V7X = 56 * 1024 * 1024


def _params(*sem):
    return pltpu.CompilerParams(dimension_semantics=sem, vmem_limit_bytes=VMEM_LIMIT_V7X)


def _steps(dim, tile):
    assert dim % tile == 0, f"tile {tile} does not divide {dim}"
    return dim // tile


def _dot(a, b):
    return jnp.dot(a, b, preferred_element_type=F32)


def _dot_nt(a, b):
    return lax.dot_general(a, b, (((1,), (1,)), ((), ())), preferred_element_type=F32)


def _dot_tn(a, b):
    return lax.dot_general(a, b, (((0,), (0,)), ((), ())), preferred_element_type=F32)


def _rms(x, g):
    return x * lax.rsqrt(jnp.mean(x * x, axis=-1, keepdims=True) + RMS_EPS) * g


def _w_in_kernel(w_ref, o_ref):
    gate_lo = QSW_OFF
    o_ref[:gate_lo, :] = w_ref[:gate_lo, :].astype(o_ref.dtype)
    o_ref[gate_lo:GATE_OFF, :] = w_ref[gate_lo + GATE_ROWS:, :].astype(o_ref.dtype)
    o_ref[GATE_OFF:GATE_OFF + GATE_ROWS, :] = w_ref[gate_lo:gate_lo + GATE_ROWS, :].astype(o_ref.dtype)
    o_ref[GATE_OFF + GATE_ROWS:, :] = jnp.zeros((LANES - GATE_ROWS, o_ref.shape[1]), o_ref.dtype)


def _w_in_layout(w_t, *, cols):
    n, k = w_t.shape
    assert n == PROJ_WIDTH - (LANES - GATE_ROWS)
    return pl.pallas_call(
        _w_in_kernel,
        grid=(_steps(k, cols),),
        in_specs=[pl.BlockSpec((n, cols), lambda i: (0, i))],
        out_specs=pl.BlockSpec((PROJ_WIDTH, cols), lambda i: (0, i)),
        out_shape=jax.ShapeDtypeStruct((PROJ_WIDTH, k), BF16),
        compiler_params=_params("parallel"),
        name="w_in_layout",
    )(w_t)


def _rms_matmul_kernel(x_ref, g_ref, w_ref, o_ref, n_ref, *, w_transposed):
    dot = _dot_nt if w_transposed else _dot

    @pl.when(pl.program_id(1) == 0)
    def _():
        half = max(x_ref.shape[0] // 2, ROW_BATCH)
        for r in range(0, x_ref.shape[0], half):
            rows = slice(r, r + half)
            n = _rms(x_ref[rows, :], g_ref[...]).astype(n_ref.dtype)
            n_ref[rows, :] = n
            o_ref[rows, :] = dot(n, w_ref[...]).astype(o_ref.dtype)

    @pl.when(pl.program_id(1) != 0)
    def _():
        o_ref[...] = dot(n_ref[...], w_ref[...]).astype(o_ref.dtype)


def _rms_matmul(x, g, w, *, tm, tn, name, w_transposed=False):
    m, k = x.shape
    n = w.shape[0] if w_transposed else w.shape[1]
    w_spec = pl.BlockSpec((tn, k), lambda i, j: (j, 0)) if w_transposed else pl.BlockSpec((k, tn), lambda i, j: (0, j))
    return pl.pallas_call(
        functools.partial(_rms_matmul_kernel, w_transposed=w_transposed),
        grid=(_steps(m, tm), _steps(n, tn)),
        in_specs=[pl.BlockSpec((tm, k), lambda i, j: (i, 0)),
                  pl.BlockSpec((1, k), lambda i, j: (0, 0)),
                  w_spec],
        out_specs=pl.BlockSpec((tm, tn), lambda i, j: (i, j)),
        out_shape=jax.ShapeDtypeStruct((m, n), F32),
        scratch_shapes=[pltpu.VMEM((tm, k), BF16)],
        compiler_params=_params("parallel", "arbitrary"),
        name=name,
    )(x, g.reshape(1, k), w)


def _split3(x):
    hi = x.astype(BF16)
    r = x - hi.astype(F32)
    mid = r.astype(BF16)
    lo = (r - mid.astype(F32)).astype(BF16)
    return hi, mid, lo


def _prep_kernel(main_ref, prev_ref, next_ref, gate_ref, cw_ref, alog_ref, dt_ref,
                 q_ref, k_ref, v_ref, go_ref, gt_ref):
    i = pl.program_id(0)
    rows = main_ref.shape[0]
    halo = CONV_HALO
    pad = (CONV_WIDTH - 1) // 2

    first = i == 0
    last = i == pl.num_programs(0) - 1
    for s in range(3 * DN_HEADS):
        cols = slice(s * HEAD_DIM, (s + 1) * HEAD_DIM)
        xe = jnp.concatenate([jnp.where(first, 0.0, prev_ref[:, cols]), main_ref[:, cols],
                              jnp.where(last, 0.0, next_ref[:, cols])], axis=0)
        acc = cw_ref[pad:pad + 1, cols] * xe[halo:halo + rows]
        for j in range(CONV_WIDTH):
            if j != pad:
                shifted = pltpu.roll(xe, shift=(pad - j) % xe.shape[0], axis=0)[halo:halo + rows]
                acc = acc + cw_ref[j:j + 1, cols] * shifted
        y = acc * jax.nn.sigmoid(acc)
        if s < 2 * DN_HEADS:
            inv = lax.rsqrt(jnp.sum(y * y, axis=-1, keepdims=True) + L2_EPS)
            y = y * (inv * (HEAD_DIM ** -0.5) if s < DN_HEADS else inv)
        if s < DN_HEADS:
            q_ref[:, cols] = y
        elif s < 2 * DN_HEADS:
            k_ref[:, slice((s - DN_HEADS) * HEAD_DIM, (s - DN_HEADS + 1) * HEAD_DIM)] = y
        else:
            v_ref[:, slice((s - 2 * DN_HEADS) * HEAD_DIM, (s - 2 * DN_HEADS + 1) * HEAD_DIM)] = y

    t = gate_ref[...]
    beta = jax.nn.sigmoid(t)
    a = t + dt_ref[...]
    softplus = jnp.maximum(a, 0.0) + jnp.log1p(jnp.exp(-jnp.abs(a)))
    g = -jnp.exp(alog_ref[...]) * softplus

    ri = lax.broadcasted_iota(jnp.int32, (rows, rows), 0)
    ci = lax.broadcasted_iota(jnp.int32, (rows, rows), 1)
    shift = DN_CHUNK.bit_length() - 1
    same_chunk = (ri >> shift) == (ci >> shift)
    lower = jnp.where(same_chunk & (ci <= ri), 1.0, 0.0).astype(BF16)
    upper = jnp.where(same_chunk & (ci >= ri), 1.0, 0.0).astype(BF16)
    parts = _split3(g)
    gc_f = _dot(lower, parts[0]) + _dot(lower, parts[1]) + _dot(lower, parts[2])
    gc_b = _dot(upper, parts[0]) + _dot(upper, parts[1]) + _dot(upper, parts[2])

    col = lax.broadcasted_iota(jnp.int32, t.shape, 1)
    out = jnp.where(col < GATE_G_F, beta, jnp.where(col < GATE_G_B, gc_f, gc_b))
    go_ref[...] = out
    gt_ref[...] = out.T[0:GATE_ROWS, :]


def _dn_prep(proj, conv_w, alog_row, dt_row, *, rows):
    s = proj.shape[0]
    nblk = _steps(s, rows)
    hb = rows // CONV_HALO
    last_halo = s // CONV_HALO - 1
    head_cols = DN_HEADS * HEAD_DIM
    return pl.pallas_call(
        _prep_kernel,
        grid=(nblk,),
        in_specs=[pl.BlockSpec((rows, DN_QKV), lambda i: (i, 0)),
                  pl.BlockSpec((CONV_HALO, DN_QKV), lambda i: (jnp.maximum(i * hb - 1, 0), 0)),
                  pl.BlockSpec((CONV_HALO, DN_QKV), lambda i: (jnp.minimum((i + 1) * hb, last_halo), 0)),
                  pl.BlockSpec((rows, LANES), lambda i: (i, GATE_OFF // LANES)),
                  pl.BlockSpec((CONV_WIDTH, DN_QKV), lambda i: (0, 0)),
                  pl.BlockSpec((1, LANES), lambda i: (0, 0)),
                  pl.BlockSpec((1, LANES), lambda i: (0, 0))],
        out_specs=[pl.BlockSpec((rows, head_cols), lambda i: (i, 0)),
                   pl.BlockSpec((rows, head_cols), lambda i: (i, 0)),
                   pl.BlockSpec((rows, head_cols), lambda i: (i, 0)),
                   pl.BlockSpec((rows, LANES), lambda i: (i, 0)),
                   pl.BlockSpec((GATE_ROWS, rows), lambda i: (0, i))],
        out_shape=[jax.ShapeDtypeStruct((s, head_cols), F32),
                   jax.ShapeDtypeStruct((s, head_cols), F32),
                   jax.ShapeDtypeStruct((s, head_cols), F32),
                   jax.ShapeDtypeStruct((s, LANES), F32),
                   jax.ShapeDtypeStruct((GATE_ROWS, s), F32)],
        compiler_params=_params("parallel"),
        name="dn_prep",
    )(proj, proj, proj, proj, conv_w, alog_row, dt_row)


def _unit_tri_inverse(ms, eye, same_sub, _mm):
    c = eye.shape[0]
    mds = [jnp.where(same_sub, m, 0.0) for m in ms]
    es = [m - md for m, md in zip(ms, mds)]
    pw = [_mm(md, md) for md in mds]
    xs = [eye - md for md in mds]
    for _ in range(DN_SUB.bit_length() - 3):
        both = [_mm(jnp.concatenate([a, x], axis=0), a) for a, x in zip(pw, xs)]
        pw = [b[:c] for b in both]
        xs = [x + b[c:] for x, b in zip(xs, both)]
    xs = [x + _mm(x, a) for x, a in zip(xs, pw)]
    ns = [_mm(x, e) for x, e in zip(xs, es)]
    n2 = [_mm(n, n) for n in ns]
    ys = [_mm(n, (eye - n) + b) for n, b in zip(ns, n2)]
    return [x - _mm(y, x) for x, y in zip(xs, ys)]


def _deltanet_kernel(qf, kf, vf, gf, gtf, qb, kb, vb, gb, gtb, of_ref, ob_ref, state_ref):
    c_len = DN_CHUNK
    nchunk = qf.shape[0] // c_len
    heads = qf.shape[1] // HEAD_DIM
    head0 = pl.program_id(0) * heads

    @pl.when(pl.program_id(1) == 0)
    def _():
        state_ref[...] = jnp.zeros_like(state_ref)

    p_len = 2 * c_len
    ri = lax.broadcasted_iota(jnp.int32, (c_len, p_len), 0)
    cl = lax.broadcasted_iota(jnp.int32, (c_len, p_len), 1)
    ci = cl & (c_len - 1)
    first = cl < c_len
    eye = jnp.where(ri == ci, 1.0, 0.0)
    sub_shift = DN_SUB.bit_length() - 1
    same_sub = (ri >> sub_shift) == (ci >> sub_shift)
    lane = lax.broadcasted_iota(jnp.int32, (p_len, LANES), 1)
    upper_rows = lax.broadcasted_iota(jnp.int32, (p_len, 1), 0) >= c_len
    first_tall = lax.broadcasted_iota(jnp.int32, (c_len + HEAD_DIM, p_len), 1) < c_len
    masks = {False: (ci <= ri, ci < ri), True: (ci >= ri, ci > ri)}
    refs = {False: (qf, kf, vf, gf, gtf, of_ref), True: (qb, kb, vb, gb, gtb, ob_ref)}

    def pair_mm(x, p):
        blockdiag = jnp.concatenate([jnp.where(first, p, 0.0), jnp.where(first, 0.0, p)], axis=0)
        return _dot(x.astype(BF16), blockdiag.astype(BF16))

    chains = [(rev, hd) for rev in (False, True) for hd in range(heads)]

    g_rows = {}
    for rev, hd in chains:
        g_col = head0 + hd + (GATE_G_B if rev else GATE_G_F)
        g_rows[rev, hd] = refs[rev][4][pl.ds(g_col, 1), :]

    def load(rev, hd, p):
        q_ref, k_ref, v_ref, g_ref = refs[rev][:4]
        rows = slice(p * p_len, (p + 1) * p_len)
        cols = slice(hd * HEAD_DIM, (hd + 1) * HEAD_DIM)
        gates = g_ref[rows, :]
        beta_col = head0 + hd + (GATE_BETA_B if rev else GATE_BETA_F)
        g_col = head0 + hd + (GATE_G_B if rev else GATE_G_F)
        beta = jnp.sum(jnp.where(lane == beta_col, gates, 0.0), axis=1, keepdims=True)
        gcol = jnp.sum(jnp.where(lane == g_col, gates, 0.0), axis=1, keepdims=True)
        grow = g_rows[rev, hd][:, rows]
        if rev:
            glast = (grow[:, 0:1], grow[:, c_len:c_len + 1])
        else:
            glast = (grow[:, c_len - 1:c_len], grow[:, p_len - 1:p_len])
        return q_ref[rows, cols], k_ref[rows, cols], v_ref[rows, cols], beta, gcol, grow, glast

    lo, hi = slice(0, c_len), slice(c_len, p_len)

    zeros = jnp.zeros((c_len, HEAD_DIM), BF16)

    def phase_a(inst):
        data = [load(*i) for i in inst]
        kbeta = [k * beta for (_, k, _, beta, _, _, _) in data]
        prod = []
        for kb_, (q, k, _, _, _, _, _) in zip(kbeta, data):
            lhs = jnp.concatenate([jnp.concatenate([kb_[lo], kb_[hi]], axis=1),
                                   jnp.concatenate([q[lo], q[hi]], axis=1)], axis=0).astype(BF16)
            kb16 = k.astype(BF16)
            rhs = jnp.concatenate([jnp.concatenate([kb16[lo], zeros], axis=1),
                                   jnp.concatenate([zeros, kb16[hi]], axis=1)], axis=0)
            prod.append(_dot_nt(lhs, rhs))
        decay = [jnp.exp(jnp.where(masks[rev][0], jnp.where(first, gcol[lo], gcol[hi]) - grow, -jnp.inf))
                 for (rev, _, _), (_, _, _, _, gcol, grow, _) in zip(inst, data)]
        ms = [jnp.where(masks[rev][1], p[lo] * dec, 0.0) for (rev, _, _), p, dec in zip(inst, prod, decay)]
        a_mat = [p[hi] * dec for p, dec in zip(prod, decay)]
        ts = _unit_tri_inverse(ms, eye, same_sub, pair_mm)
        egs = [jnp.exp(gcol) for (_, _, _, _, gcol, _, _) in data]
        uw = []
        for t, kb_, eg, (_, _, v, beta, _, _, _) in zip(ts, kbeta, egs, data):
            lhs = jnp.concatenate([jnp.where(first, t, 0.0), jnp.where(first, 0.0, t)], axis=0).astype(BF16)
            rhs = jnp.concatenate([(v * beta).astype(BF16), (kb_ * eg).astype(BF16)], axis=1)
            uw.append(_dot(lhs, rhs))
        qdec = [q * eg for eg, (q, _, _, _, _, _, _) in zip(egs, data)]
        kdec_t = [(k * jnp.exp(jnp.where(upper_rows, glast[1], glast[0]) - gcol)).T
                  for (_, k, _, _, gcol, _, glast) in data]
        tall = [jnp.concatenate([a_, kt], axis=0) for a_, kt in zip(a_mat, kdec_t)]
        pre = {}
        for (rev, hd, p), x, qd, tl, (_, _, _, _, _, _, glast) in zip(inst, uw, qdec, tall, data):
            for half, rs in enumerate((lo, hi)):
                wq = jnp.concatenate([x[rs, HEAD_DIM:], qd[rs]], axis=0).astype(BF16)
                keep = first_tall if half == 0 else ~first_tall
                pre[rev, hd, 2 * p + half] = (x[rs, :HEAD_DIM], wq, jnp.where(keep, tl, 0.0).astype(BF16),
                                              jnp.exp(glast[half]))
        return pre

    def phase_b(states, pre, t):
        chunk = {False: t, True: nchunk - 1 - t}
        cur = [pre[rev, hd, chunk[rev]] for rev, hd in chains]
        ws = [_dot(wq, s_.astype(BF16)) for (_, wq, _, _), s_ in zip(cur, states)]
        v_new = [(u - x[lo]).astype(BF16) for (u, _, _, _), x in zip(cur, ws)]
        upd = [_dot(tl, jnp.concatenate([vn, vn], axis=0)) for (_, _, tl, _), vn in zip(cur, v_new)]
        for (rev, hd), x, y in zip(chains, ws, upd):
            c = chunk[rev]
            refs[rev][5][c * c_len:(c + 1) * c_len, hd * HEAD_DIM:(hd + 1) * HEAD_DIM] = x[hi] + y[lo]
        return [s_ * g_ + x[c_len:] for (_, _, _, g_), s_, x in zip(cur, states, upd)]

    npair = nchunk // 2
    states = [state_ref[n] for n in range(len(chains))]
    for s in range(npair):
        pre = phase_a([(rev, hd, npair - 1 - s if rev else s) for rev, hd in chains])
        for t in (2 * s, 2 * s + 1):
            states = phase_b(states, pre, t)
    for n, s_ in enumerate(states):
        state_ref[n] = s_


def _deltanet(q, k, v, gates, gates_t, *, rows, heads):
    s = q.shape[0]
    nb = _steps(s, rows)
    assert rows % (2 * DN_CHUNK) == 0
    fwd = lambda h, b: (b, h)
    bwd = lambda h, b: (nb - 1 - b, h)
    head = lambda im: pl.BlockSpec((rows, heads * HEAD_DIM), im)
    gate = lambda im: pl.BlockSpec((rows, LANES), lambda h, b: (im(h, b)[0], 0))
    gate_t = lambda im: pl.BlockSpec((GATE_ROWS, rows), lambda h, b: (0, im(h, b)[0]))
    return pl.pallas_call(
        _deltanet_kernel,
        grid=(_steps(DN_HEADS, heads), nb),
        in_specs=[head(fwd), head(fwd), head(fwd), gate(fwd), gate_t(fwd),
                  head(bwd), head(bwd), head(bwd), gate(bwd), gate_t(bwd)],
        out_specs=[head(fwd), head(bwd)],
        out_shape=[jax.ShapeDtypeStruct(q.shape, F32), jax.ShapeDtypeStruct(q.shape, F32)],
        scratch_shapes=[pltpu.VMEM((2 * heads, HEAD_DIM, HEAD_DIM), F32)],
        compiler_params=_params("parallel", "arbitrary"),
        name="deltanet",
    )(q, k, v, gates, gates_t, q, k, v, gates, gates_t)


def _t5_bucket(rel):
    nb = NUM_BUCKETS // 2
    max_exact = nb // 2
    n = jnp.abs(rel)
    large = max_exact + (jnp.log(jnp.maximum(n, max_exact).astype(F32) / max_exact)
                         / math.log(MAX_DISTANCE / max_exact) * (nb - max_exact)).astype(jnp.int32)
    large = jnp.minimum(large, nb - 1)
    return jnp.where(rel > 0, nb, 0) + jnp.where(n < max_exact, n, large)


def _swa_bias_init(bucket_ref, rb_ref, bias_ref):
    w = WINDOW
    ri = lax.broadcasted_iota(jnp.int32, (w, 3 * w), 0)
    ci = lax.broadcasted_iota(jnp.int32, (w, 3 * w), 1)
    bucket = bucket_ref[...]
    in_band = jnp.abs(ci - w - ri) <= w
    for hd in range(SWA_HEADS):
        acc = jnp.zeros((w, 3 * w), F32)
        for b in range(NUM_BUCKETS):
            acc = jnp.where(bucket == b, rb_ref[b, hd], acc)
        bias_ref[hd] = jnp.where(in_band, acc * LOG2E, -jnp.inf)


def _swa_block(t, qblk, first_row, seq, q_ref, kext, vext, bias_ref, sinks):
    w = WINDOW
    group = SWA_HEADS // SWA_KV_HEADS
    heads = range(SWA_HEADS)
    rows = slice(t * w, (t + 1) * w)
    band = slice(t * w, (t + 3) * w)
    s = [_dot_nt(q_ref[rows, hd * HEAD_DIM:(hd + 1) * HEAD_DIM].astype(BF16), kext[hd // group][band])
         * (HEAD_DIM ** -0.5 * LOG2E) + bias_ref[hd] for hd in heads]
    if t == 0 or t == qblk - 1:
        key_pos = first_row + (t - 1) * w + lax.broadcasted_iota(jnp.int32, (w, 3 * w), 1)
        if t == 0:
            s = [jnp.where(key_pos >= 0, x, -jnp.inf) for x in s]
        if t == qblk - 1:
            s = [jnp.where(key_pos < seq, x, -jnp.inf) for x in s]
    mx = [jnp.maximum(jnp.max(x, axis=1, keepdims=True), sk) for x, sk in zip(s, sinks)]
    p = [jnp.exp2(x - m) for x, m in zip(s, mx)]
    den = [jnp.sum(x, axis=1, keepdims=True) + jnp.exp2(sk - m) for x, sk, m in zip(p, sinks, mx)]
    o = [_dot(x.astype(BF16), vext[hd // group][band]) / dn for hd, x, dn in zip(heads, p, den)]
    return jnp.concatenate([x.astype(BF16) for x in o], axis=1)


def _mix_mem_kernel(of_ref, ob_ref, z_ref, qs_ref, kp_ref, kc_ref, kn_ref, vp_ref, vc_ref, vn_ref, bucket_ref, rb_ref,
                    sink_ref, dng_ref, wout_ref, x_ref, gx_ref, wq_ref, k_ref, v_ref, wo_ref, gf_ref,
                    h2_ref, f_ref, bias_ref, *, seq):
    n = pl.program_id(0)
    rows_per_step = h2_ref.shape[0]
    qblk = rows_per_step // WINDOW

    @pl.when(n == 0)
    def _():
        _swa_bias_init(bucket_ref, rb_ref, bias_ref)

    kext, vext = [], []
    for kvh in range(SWA_KV_HEADS):
        kcols = slice(kvh * HEAD_DIM, (kvh + 1) * HEAD_DIM)
        kext.append(jnp.concatenate([kp_ref[:, kcols], kc_ref[:, kcols], kn_ref[:, kcols]], axis=0).astype(BF16))
        vext.append(jnp.concatenate([vp_ref[:, kcols], vc_ref[:, kcols], vn_ref[:, kcols]], axis=0).astype(BF16))
    sinks = [sink_ref[hd] * LOG2E for hd in range(SWA_HEADS)]
    for t0 in range(0, qblk, 2):
        blocks = range(t0, min(t0 + 2, qblk))
        y_sw = [_swa_block(t, qblk, n * rows_per_step, seq, qs_ref, kext, vext, bias_ref, sinks) for t in blocks]
        _mix_mem_rows([slice(t * WINDOW, (t + 1) * WINDOW) for t in blocks], y_sw, HEAD_DIM ** -0.5, of_ref, ob_ref,
                      z_ref, dng_ref, wout_ref, x_ref, gx_ref, wq_ref, k_ref, v_ref, wo_ref, gf_ref, h2_ref, f_ref)


def _mix_mem_rows(batches, y_sw, scale, of_ref, ob_ref, z_ref, dng_ref, wout_ref, x_ref, gx_ref, wq_ref, k_ref, v_ref,
                  wo_ref, gf_ref, h2_ref, f_ref):
    dn = DN_HEADS * HEAD_DIM
    y_dn = []
    for rows in batches:
        parts = []
        for hd in range(DN_HEADS):
            cols = slice(hd * HEAD_DIM, (hd + 1) * HEAD_DIM)
            z = z_ref[rows, cols]
            y = _rms(of_ref[rows, cols] + ob_ref[rows, cols], dng_ref[...]) * (z * jax.nn.sigmoid(z))
            parts.append(y.astype(BF16))
        y_dn.append(jnp.concatenate(parts, axis=1))
    h1 = [x_ref[rows, :] + _dot(a, wout_ref[:dn, :]) for rows, a in zip(batches, y_dn)]
    h1 = [h + _dot(sw, wout_ref[dn:, :]) for h, sw in zip(h1, y_sw)]
    q = [_dot(_rms(h, gx_ref[...]).astype(BF16), wq_ref[...]) for h in h1]
    att = []
    for qb in q:
        heads = []
        for hd in range(MEM_HEADS):
            cols = slice(hd * HEAD_DIM, (hd + 1) * HEAD_DIM)
            s = _dot_nt(qb[:, cols].astype(BF16), k_ref[:, cols]) * scale
            p = jnp.exp(s - jnp.max(s, axis=1, keepdims=True))
            den = jnp.sum(p, axis=1, keepdims=True)
            heads.append((_dot(p.astype(BF16), v_ref[:, cols]) / den).astype(BF16))
        att.append(jnp.concatenate(heads, axis=1))
    h2 = [h + _dot(a, wo_ref[...]) for h, a in zip(h1, att)]
    for rows, h in zip(batches, h2):
        h2_ref[rows, :] = h
        f_ref[rows, :] = _rms(h, gf_ref[...]).astype(f_ref.dtype)


def _mix_mem(o_f, o_b, proj, bucket, rel_bias, sink, dn_g, w_out, x, gx, wq, k, v, wo, gf, *, tm):
    s, d = x.shape
    w = WINDOW
    dn = DN_HEADS * HEAD_DIM
    qw = SWA_HEADS * HEAD_DIM
    kvw = SWA_KV_HEADS * HEAD_DIM
    qblk = _steps(tm, w)
    nb = _steps(s, w)
    row = lambda width, col=0: pl.BlockSpec((tm, width), lambda i: (i, col))
    halo = lambda off, lo: pl.BlockSpec(
        (w, kvw), lambda i: (jnp.clip(i * qblk - 1 if lo else (i + 1) * qblk, 0, nb - 1), off // kvw))
    const = lambda a: pl.BlockSpec(a.shape, lambda i: (0, 0), pipeline_mode=pl.Buffered(1))
    smem = pl.BlockSpec(memory_space=pltpu.SMEM)
    dn_g, gx, gf = dn_g.reshape(1, HEAD_DIM), gx.reshape(1, d), gf.reshape(1, d)
    return pl.pallas_call(
        functools.partial(_mix_mem_kernel, seq=s),
        grid=(_steps(s, tm),),
        in_specs=[row(dn), row(dn), row(dn, Z_OFF // dn), row(qw, QSW_OFF // qw),
                  halo(KSW_OFF, True), row(kvw, KSW_OFF // kvw), halo(KSW_OFF, False),
                  halo(VSW_OFF, True), row(kvw, VSW_OFF // kvw), halo(VSW_OFF, False),
                  const(bucket), smem, smem, const(dn_g), const(w_out), row(d),
                  const(gx), const(wq), const(k), const(v), const(wo), const(gf)],
        out_specs=[row(d), row(d)],
        out_shape=[jax.ShapeDtypeStruct((s, d), F32), jax.ShapeDtypeStruct((s, d), BF16)],
        scratch_shapes=[pltpu.VMEM((SWA_HEADS, w, 3 * w), F32)],
        compiler_params=_params("arbitrary"),
        name="mix_mem",
    )(o_f, o_b, proj, proj, proj, proj, proj, proj, proj, proj, bucket, rel_bias, sink, dn_g, w_out, x,
      gx, wq, k, v, wo, gf)


def _glu_kernel(f_ref, wg_ref, wu_ref, wd_ref, o_ref, wd16_ref):
    f = f_ref[...]
    a = _dot(f, wg_ref[...].astype(BF16))
    o_ref[...] = (a * jax.nn.sigmoid(a) * _dot(f, wu_ref[...].astype(BF16))).astype(o_ref.dtype)
    wd16_ref[...] = wd_ref[...].astype(wd16_ref.dtype)


def _ffn_glu(f, wg, wu, wd, *, tm, tn):
    s, d = f.shape
    dff = wg.shape[1]
    ni, nj = _steps(s, tm), _steps(dff, tn)
    wd_rows = _steps(wd.shape[0], ni * nj)
    wd_spec = pl.BlockSpec((wd_rows, wd.shape[1]), lambda i, j: (i * nj + j, 0))
    return pl.pallas_call(
        _glu_kernel,
        grid=(ni, nj),
        in_specs=[pl.BlockSpec((tm, d), lambda i, j: (i, 0)),
                  pl.BlockSpec((d, tn), lambda i, j: (0, j)),
                  pl.BlockSpec((d, tn), lambda i, j: (0, j)),
                  wd_spec],
        out_specs=[pl.BlockSpec((tm, tn), lambda i, j: (i, j)), wd_spec],
        out_shape=[jax.ShapeDtypeStruct((s, dff), BF16), jax.ShapeDtypeStruct(wd.shape, BF16)],
        compiler_params=_params("parallel", "parallel"),
        name="ffn_glu",
    )(f, wg, wu, wd)


def _down_kernel(a_ref, w_ref, h_ref, g_ref, o_ref, acc_ref):
    k = pl.program_id(1)
    last = pl.num_programs(1) - 1

    @pl.when(k == 0)
    def _():
        acc_ref[...] = h_ref[...] + _dot(a_ref[...], w_ref[...])

    @pl.when((k > 0) & (k < last))
    def _():
        acc_ref[...] += _dot(a_ref[...], w_ref[...])

    @pl.when(k == last)
    def _():
        half = o_ref.shape[0] // 2
        for r in (0, half):
            rows = slice(r, r + half)
            o_ref[rows, :] = _rms(acc_ref[rows, :] + _dot(a_ref[rows, :], w_ref[...]), g_ref[...])


def _ffn_down(act, wd, h, g, *, tm, tk):
    s, dff = act.shape
    d = wd.shape[1]
    assert _steps(dff, tk) >= 2
    return pl.pallas_call(
        _down_kernel,
        grid=(_steps(s, tm), _steps(dff, tk)),
        in_specs=[pl.BlockSpec((tm, tk), lambda i, k: (i, k)),
                  pl.BlockSpec((tk, d), lambda i, k: (k, 0)),
                  pl.BlockSpec((tm, d), lambda i, k: (i, 0)),
                  pl.BlockSpec((1, d), lambda i, k: (0, 0))],
        out_specs=pl.BlockSpec((tm, d), lambda i, k: (i, 0)),
        out_shape=jax.ShapeDtypeStruct((s, d), F32),
        scratch_shapes=[pltpu.VMEM((tm, d), F32)],
        compiler_params=_params("parallel", "arbitrary"),
        name="ffn_down",
    )(act, wd, h, g.reshape(1, d))


def _gate_row(fwd, bwd):
    row = jnp.zeros((1, LANES), F32)
    row = row.at[0, GATE_G_F:GATE_G_F + DN_HEADS].set(fwd.astype(F32))
    return row.at[0, GATE_G_B:GATE_G_B + DN_HEADS].set(bwd.astype(F32))


def _pick(n, *cands):
    for c in cands:
        if n % c == 0:
            return c
    return n


def kernel(x, mem, norm_mix_g, w_in, conv_w, a_log_f, a_log_b, dt_bias_f, dt_bias_b, dn_norm_g, attn_sink, rel_bias, w_out, norm_x_g, norm_mem_g, w_q_mem, w_kv_mem, w_o_mem, norm_ffn_g, w_gate, w_up, w_down, norm_final_g):
    batch, s, d = x.shape
    assert batch == 1 and mem.shape[0] == 1 and w_in.shape[0] == 1, "single sequence, single layer"
    w = WINDOW
    rel = (jnp.arange(3 * w)[None, :] - w) - jnp.arange(w)[:, None]
    bucket = _t5_bucket(rel).astype(jnp.int32)
    mem_dim = MEM_HEADS * HEAD_DIM

    h = x.reshape(s, d)
    mem2 = mem.reshape(mem.shape[1], d)
    tm_big = _pick(s, 1024, 512, 256, 128)
    tm_mid = _pick(s, 512, 256, 128)
    w_r = _w_in_layout(w_in[0].T, cols=256)
    proj = _rms_matmul(h, norm_mix_g[0], w_r, tm=tm_big, tn=_pick(PROJ_WIDTH, 1920, 1152, 640, 128), name="in_proj",
                       w_transposed=True)

    q, k, v, gates, gates_t = _dn_prep(proj, conv_w[0], _gate_row(a_log_f[0], a_log_b[0]),
                                       _gate_row(dt_bias_f[0], dt_bias_b[0]), rows=_pick(s, 256, 128))
    o_f, o_b = _deltanet(q, k, v, gates, gates_t, rows=_pick(s, 512, 256, 128), heads=8)
    kv = _rms_matmul(mem2, norm_mem_g[0], w_kv_mem[0].astype(BF16), tm=mem2.shape[0], tn=2 * mem_dim,
                     name="mem_kv").astype(BF16)
    h, f = _mix_mem(o_f, o_b, proj, bucket, rel_bias.astype(F32), attn_sink[0].astype(F32), dn_norm_g[0],
                    w_out[0].astype(BF16), h, norm_x_g[0], w_q_mem[0].astype(BF16), kv[:, :mem_dim], kv[:, mem_dim:],
                    w_o_mem[0].astype(BF16), norm_ffn_g[0], tm=_pick(s, 256))

    act, w_down16 = _ffn_glu(f, w_gate[0], w_up[0], w_down[0], tm=tm_big, tn=512)
    out = _ffn_down(act, w_down16, h, norm_final_g, tm=tm_mid, tk=2816)
    return out.reshape(batch, s, d)
```

```python
import functools
import math

import jax
import jax.numpy as jnp
from jax import lax
from jax.experimental import pallas as pl
from jax.experimental.pallas import tpu as pltpu

F32 = jnp.float32
BF16 = jnp.bfloat16

RMS_EPS = 1e-6
L2_EPS = 1e-6
HEAD_DIM = 128
DN_HEADS = 8
DN_CHUNK = 64
DN_SUB = 16
CONV_WIDTH = 5
CONV_HALO = 8
SWA_HEADS = 8
SWA_KV_HEADS = 2
WINDOW = 128
NUM_BUCKETS = 32
MAX_DISTANCE = 128
MEM_HEADS = 4
LOG2E = 1.4426950408889634
LANES = 128
ROW_BATCH = 128

DN_QKV = 3 * DN_HEADS * HEAD_DIM
Z_OFF = DN_QKV
QSW_OFF = Z_OFF + DN_HEADS * HEAD_DIM
KSW_OFF = QSW_OFF + SWA_HEADS * HEAD_DIM
VSW_OFF = KSW_OFF + SWA_KV_HEADS * HEAD_DIM
GATE_OFF = VSW_OFF + SWA_KV_HEADS * HEAD_DIM
PROJ_WIDTH = GATE_OFF + LANES
GATE_BETA_F, GATE_BETA_B, GATE_G_F, GATE_G_B = 0, DN_HEADS, 2 * DN_HEADS, 3 * DN_HEADS
GATE_ROWS = 4 * DN_HEADS

VMEM_LIMIT_V7X = 56 * 1024 * 1024


def _params(*sem):
    return pltpu.CompilerParams(dimension_semantics=sem, vmem_limit_bytes=VMEM_LIMIT_V7X)


def _steps(dim, tile):
    assert dim % tile == 0, f"tile {tile} does not divide {dim}"
    return dim // tile


def _dot(a, b):
    return jnp.dot(a, b, preferred_element_type=F32)


def _dot_nt(a, b):
    return lax.dot_general(a, b, (((1,), (1,)), ((), ())), preferred_element_type=F32)


def _dot_tn(a, b):
    return lax.dot_general(a, b, (((0,), (0,)), ((), ())), preferred_element_type=F32)


def _rms(x, g):
    return x * lax.rsqrt(jnp.mean(x * x, axis=-1, keepdims=True) + RMS_EPS) * g


def _w_in_kernel(w_ref, o_ref):
    gate_lo = QSW_OFF
    o_ref[:gate_lo, :] = w_ref[:gate_lo, :].astype(o_ref.dtype)
    o_ref[gate_lo:GATE_OFF, :] = w_ref[gate_lo + GATE_ROWS:, :].astype(o_ref.dtype)
    o_ref[GATE_OFF:GATE_OFF + GATE_ROWS, :] = w_ref[gate_lo:gate_lo + GATE_ROWS, :].astype(o_ref.dtype)
    o_ref[GATE_OFF + GATE_ROWS:, :] = jnp.zeros((LANES - GATE_ROWS, o_ref.shape[1]), o_ref.dtype)


def _w_in_layout(w_t, *, cols):
    n, k = w_t.shape
    assert n == PROJ_WIDTH - (LANES - GATE_ROWS)
    return pl.pallas_call(
        _w_in_kernel,
        grid=(_steps(k, cols),),
        in_specs=[pl.BlockSpec((n, cols), lambda i: (0, i))],
        out_specs=pl.BlockSpec((PROJ_WIDTH, cols), lambda i: (0, i)),
        out_shape=jax.ShapeDtypeStruct((PROJ_WIDTH, k), BF16),
        compiler_params=_params("parallel"),
        name="w_in_layout",
    )(w_t)


def _rms_matmul_kernel(x_ref, g_ref, w_ref, o_ref, n_ref, *, w_transposed):
    dot = _dot_nt if w_transposed else _dot

    @pl.when(pl.program_id(1) == 0)
    def _():
        half = max(x_ref.shape[0] // 2, ROW_BATCH)
        for r in range(0, x_ref.shape[0], half):
            rows = slice(r, r + half)
            n = _rms(x_ref[rows, :], g_ref[...]).astype(n_ref.dtype)
            n_ref[rows, :] = n
            o_ref[rows, :] = dot(n, w_ref[...]).astype(o_ref.dtype)

    @pl.when(pl.program_id(1) != 0)
    def _():
        o_ref[...] = dot(n_ref[...], w_ref[...]).astype(o_ref.dtype)


def _rms_matmul(x, g, w, *, tm, tn, name, w_transposed=False):
    m, k = x.shape
    n = w.shape[0] if w_transposed else w.shape[1]
    w_spec = pl.BlockSpec((tn, k), lambda i, j: (j, 0)) if w_transposed else pl.BlockSpec((k, tn), lambda i, j: (0, j))
    return pl.pallas_call(
        functools.partial(_rms_matmul_kernel, w_transposed=w_transposed),
        grid=(_steps(m, tm), _steps(n, tn)),
        in_specs=[pl.BlockSpec((tm, k), lambda i, j: (i, 0)),
                  pl.BlockSpec((1, k), lambda i, j: (0, 0)),
                  w_spec],
        out_specs=pl.BlockSpec((tm, tn), lambda i, j: (i, j)),
        out_shape=jax.ShapeDtypeStruct((m, n), F32),
        scratch_shapes=[pltpu.VMEM((tm, k), BF16)],
        compiler_params=_params("parallel", "arbitrary"),
        name=name,
    )(x, g.reshape(1, k), w)


def _split3(x):
    hi = x.astype(BF16)
    r = x - hi.astype(F32)
    mid = r.astype(BF16)
    lo = (r - mid.astype(F32)).astype(BF16)
    return hi, mid, lo


def _prep_kernel(main_ref, prev_ref, next_ref, gate_ref, cw_ref, alog_ref, dt_ref,
                 q_ref, k_ref, v_ref, go_ref, gt_ref):
    i = pl.program_id(0)
    rows = main_ref.shape[0]
    halo = CONV_HALO
    pad = (CONV_WIDTH - 1) // 2

    first = i == 0
    last = i == pl.num_programs(0) - 1
    for s in range(3 * DN_HEADS):
        cols = slice(s * HEAD_DIM, (s + 1) * HEAD_DIM)
        xe = jnp.concatenate([jnp.where(first, 0.0, prev_ref[:, cols]), main_ref[:, cols],
                              jnp.where(last, 0.0, next_ref[:, cols])], axis=0)
        acc = cw_ref[pad:pad + 1, cols] * xe[halo:halo + rows]
        for j in range(CONV_WIDTH):
            if j != pad:
                shifted = pltpu.roll(xe, shift=(pad - j) % xe.shape[0], axis=0)[halo:halo + rows]
                acc = acc + cw_ref[j:j + 1, cols] * shifted
        y = acc * jax.nn.sigmoid(acc)
        if s < 2 * DN_HEADS:
            inv = lax.rsqrt(jnp.sum(y * y, axis=-1, keepdims=True) + L2_EPS)
            y = y * (inv * (HEAD_DIM ** -0.5) if s < DN_HEADS else inv)
        if s < DN_HEADS:
            q_ref[:, cols] = y
        elif s < 2 * DN_HEADS:
            k_ref[:, slice((s - DN_HEADS) * HEAD_DIM, (s - DN_HEADS + 1) * HEAD_DIM)] = y
        else:
            v_ref[:, slice((s - 2 * DN_HEADS) * HEAD_DIM, (s - 2 * DN_HEADS + 1) * HEAD_DIM)] = y

    t = gate_ref[...]
    beta = jax.nn.sigmoid(t)
    a = t + dt_ref[...]
    softplus = jnp.maximum(a, 0.0) + jnp.log1p(jnp.exp(-jnp.abs(a)))
    g = -jnp.exp(alog_ref[...]) * softplus

    ri = lax.broadcasted_iota(jnp.int32, (rows, rows), 0)
    ci = lax.broadcasted_iota(jnp.int32, (rows, rows), 1)
    shift = DN_CHUNK.bit_length() - 1
    same_chunk = (ri >> shift) == (ci >> shift)
    lower = jnp.where(same_chunk & (ci <= ri), 1.0, 0.0).astype(BF16)
    upper = jnp.where(same_chunk & (ci >= ri), 1.0, 0.0).astype(BF16)
    parts = _split3(g)
    gc_f = _dot(lower, parts[0]) + _dot(lower, parts[1]) + _dot(lower, parts[2])
    gc_b = _dot(upper, parts[0]) + _dot(upper, parts[1]) + _dot(upper, parts[2])

    col = lax.broadcasted_iota(jnp.int32, t.shape, 1)
    out = jnp.where(col < GATE_G_F, beta, jnp.where(col < GATE_G_B, gc_f, gc_b))
    go_ref[...] = out
    gt_ref[...] = out.T[0:GATE_ROWS, :]


def _dn_prep(proj, conv_w, alog_row, dt_row, *, rows):
    s = proj.shape[0]
    nblk = _steps(s, rows)
    hb = rows // CONV_HALO
    last_halo = s // CONV_HALO - 1
    head_cols = DN_HEADS * HEAD_DIM
    return pl.pallas_call(
        _prep_kernel,
        grid=(nblk,),
        in_specs=[pl.BlockSpec((rows, DN_QKV), lambda i: (i, 0)),
                  pl.BlockSpec((CONV_HALO, DN_QKV), lambda i: (jnp.maximum(i * hb - 1, 0), 0)),
                  pl.BlockSpec((CONV_HALO, DN_QKV), lambda i: (jnp.minimum((i + 1) * hb, last_halo), 0)),
                  pl.BlockSpec((rows, LANES), lambda i: (i, GATE_OFF // LANES)),
                  pl.BlockSpec((CONV_WIDTH, DN_QKV), lambda i: (0, 0)),
                  pl.BlockSpec((1, LANES), lambda i: (0, 0)),
                  pl.BlockSpec((1, LANES), lambda i: (0, 0))],
        out_specs=[pl.BlockSpec((rows, head_cols), lambda i: (i, 0)),
                   pl.BlockSpec((rows, head_cols), lambda i: (i, 0)),
                   pl.BlockSpec((rows, head_cols), lambda i: (i, 0)),
                   pl.BlockSpec((rows, LANES), lambda i: (i, 0)),
                   pl.BlockSpec((GATE_ROWS, rows), lambda i: (0, i))],
        out_shape=[jax.ShapeDtypeStruct((s, head_cols), F32),
                   jax.ShapeDtypeStruct((s, head_cols), F32),
                   jax.ShapeDtypeStruct((s, head_cols), F32),
                   jax.ShapeDtypeStruct((s, LANES), F32),
                   jax.ShapeDtypeStruct((GATE_ROWS, s), F32)],
        compiler_params=_params("parallel"),
        name="dn_prep",
    )(proj, proj, proj, proj, conv_w, alog_row, dt_row)


def _unit_tri_inverse(ms, eye, same_sub, _mm):
    c = eye.shape[0]
    mds = [jnp.where(same_sub, m, 0.0) for m in ms]
    es = [m - md for m, md in zip(ms, mds)]
    pw = [_mm(md, md) for md in mds]
    xs = [eye - md for md in mds]
    for _ in range(DN_SUB.bit_length() - 3):
        both = [_mm(jnp.concatenate([a, x], axis=0), a) for a, x in zip(pw, xs)]
        pw = [b[:c] for b in both]
        xs = [x + b[c:] for x, b in zip(xs, both)]
    xs = [x + _mm(x, a) for x, a in zip(xs, pw)]
    ns = [_mm(x, e) for x, e in zip(xs, es)]
    n2 = [_mm(n, n) for n in ns]
    ys = [_mm(n, (eye - n) + b) for n, b in zip(ns, n2)]
    return [x - _mm(y, x) for x, y in zip(xs, ys)]


def _deltanet_kernel(qf, kf, vf, gf, gtf, qb, kb, vb, gb, gtb, of_ref, ob_ref, state_ref):
    c_len = DN_CHUNK
    nchunk = qf.shape[0] // c_len
    heads = qf.shape[1] // HEAD_DIM
    head0 = pl.program_id(0) * heads

    @pl.when(pl.program_id(1) == 0)
    def _():
        state_ref[...] = jnp.zeros_like(state_ref)

    p_len = 2 * c_len
    ri = lax.broadcasted_iota(jnp.int32, (c_len, p_len), 0)
    cl = lax.broadcasted_iota(jnp.int32, (c_len, p_len), 1)
    ci = cl & (c_len - 1)
    first = cl < c_len
    eye = jnp.where(ri == ci, 1.0, 0.0)
    sub_shift = DN_SUB.bit_length() - 1
    same_sub = (ri >> sub_shift) == (ci >> sub_shift)
    lane = lax.broadcasted_iota(jnp.int32, (p_len, LANES), 1)
    upper_rows = lax.broadcasted_iota(jnp.int32, (p_len, 1), 0) >= c_len
    first_tall = lax.broadcasted_iota(jnp.int32, (c_len + HEAD_DIM, p_len), 1) < c_len
    masks = {False: (ci <= ri, ci < ri), True: (ci >= ri, ci > ri)}
    refs = {False: (qf, kf, vf, gf, gtf, of_ref), True: (qb, kb, vb, gb, gtb, ob_ref)}

    def pair_mm(x, p):
        blockdiag = jnp.concatenate([jnp.where(first, p, 0.0), jnp.where(first, 0.0, p)], axis=0)
        return _dot(x.astype(BF16), blockdiag.astype(BF16))

    chains = [(rev, hd) for rev in (False, True) for hd in range(heads)]

    g_rows = {}
    for rev, hd in chains:
        g_col = head0 + hd + (GATE_G_B if rev else GATE_G_F)
        g_rows[rev, hd] = refs[rev][4][pl.ds(g_col, 1), :]

    def load(rev, hd, p):
        q_ref, k_ref, v_ref, g_ref = refs[rev][:4]
        rows = slice(p * p_len, (p + 1) * p_len)
        cols = slice(hd * HEAD_DIM, (hd + 1) * HEAD_DIM)
        gates = g_ref[rows, :]
        beta_col = head0 + hd + (GATE_BETA_B if rev else GATE_BETA_F)
        g_col = head0 + hd + (GATE_G_B if rev else GATE_G_F)
        beta = jnp.sum(jnp.where(lane == beta_col, gates, 0.0), axis=1, keepdims=True)
        gcol = jnp.sum(jnp.where(lane == g_col, gates, 0.0), axis=1, keepdims=True)
        grow = g_rows[rev, hd][:, rows]
        if rev:
            glast = (grow[:, 0:1], grow[:, c_len:c_len + 1])
        else:
            glast = (grow[:, c_len - 1:c_len], grow[:, p_len - 1:p_len])
        return q_ref[rows, cols], k_ref[rows, cols], v_ref[rows, cols], beta, gcol, grow, glast

    lo, hi = slice(0, c_len), slice(c_len, p_len)

    zeros = jnp.zeros((c_len, HEAD_DIM), BF16)

    def phase_a(inst):
        data = [load(*i) for i in inst]
        kbeta = [k * beta for (_, k, _, beta, _, _, _) in data]
        prod = []
        for kb_, (q, k, _, _, _, _, _) in zip(kbeta, data):
            lhs = jnp.concatenate([jnp.concatenate([kb_[lo], kb_[hi]], axis=1),
                                   jnp.concatenate([q[lo], q[hi]], axis=1)], axis=0).astype(BF16)
            kb16 = k.astype(BF16)
            rhs = jnp.concatenate([jnp.concatenate([kb16[lo], zeros], axis=1),
                                   jnp.concatenate([zeros, kb16[hi]], axis=1)], axis=0)
            prod.append(_dot_nt(lhs, rhs))
        decay = [jnp.exp(jnp.where(masks[rev][0], jnp.where(first, gcol[lo], gcol[hi]) - grow, -jnp.inf))
                 for (rev, _, _), (_, _, _, _, gcol, grow, _) in zip(inst, data)]
        ms = [jnp.where(masks[rev][1], p[lo] * dec, 0.0) for (rev, _, _), p, dec in zip(inst, prod, decay)]
        a_mat = [p[hi] * dec for p, dec in zip(prod, decay)]
        ts = _unit_tri_inverse(ms, eye, same_sub, pair_mm)
        egs = [jnp.exp(gcol) for (_, _, _, _, gcol, _, _) in data]
        uw = []
        for t, kb_, eg, (_, _, v, beta, _, _, _) in zip(ts, kbeta, egs, data):
            lhs = jnp.concatenate([jnp.where(first, t, 0.0), jnp.where(first, 0.0, t)], axis=0).astype(BF16)
            rhs = jnp.concatenate([(v * beta).astype(BF16), (kb_ * eg).astype(BF16)], axis=1)
            uw.append(_dot(lhs, rhs))
        qdec = [q * eg for eg, (q, _, _, _, _, _, _) in zip(egs, data)]
        kdec_t = [(k * jnp.exp(jnp.where(upper_rows, glast[1], glast[0]) - gcol)).T
                  for (_, k, _, _, gcol, _, glast) in data]
        tall = [jnp.concatenate([a_, kt], axis=0) for a_, kt in zip(a_mat, kdec_t)]
        pre = {}
        for (rev, hd, p), x, qd, tl, (_, _, _, _, _, _, glast) in zip(inst, uw, qdec, tall, data):
            for half, rs in enumerate((lo, hi)):
                wq = jnp.concatenate([x[rs, HEAD_DIM:], qd[rs]], axis=0).astype(BF16)
                keep = first_tall if half == 0 else ~first_tall
                pre[rev, hd, 2 * p + half] = (x[rs, :HEAD_DIM], wq, jnp.where(keep, tl, 0.0).astype(BF16),
                                              jnp.exp(glast[half]))
        return pre

    def phase_b(states, pre, t):
        chunk = {False: t, True: nchunk - 1 - t}
        cur = [pre[rev, hd, chunk[rev]] for rev, hd in chains]
        ws = [_dot(wq, s_.astype(BF16)) for (_, wq, _, _), s_ in zip(cur, states)]
        v_new = [(u - x[lo]).astype(BF16) for (u, _, _, _), x in zip(cur, ws)]
        upd = [_dot(tl, jnp.concatenate([vn, vn], axis=0)) for (_, _, tl, _), vn in zip(cur, v_new)]
        for (rev, hd), x, y in zip(chains, ws, upd):
            c = chunk[rev]
            refs[rev][5][c * c_len:(c + 1) * c_len, hd * HEAD_DIM:(hd + 1) * HEAD_DIM] = x[hi] + y[lo]
        return [s_ * g_ + x[c_len:] for (_, _, _, g_), s_, x in zip(cur, states, upd)]

    npair = nchunk // 2
    states = [state_ref[n] for n in range(len(chains))]
    for s in range(npair):
        pre = phase_a([(rev, hd, npair - 1 - s if rev else s) for rev, hd in chains])
        for t in (2 * s, 2 * s + 1):
            states = phase_b(states, pre, t)
    for n, s_ in enumerate(states):
        state_ref[n] = s_


def _deltanet(q, k, v, gates, gates_t, *, rows, heads):
    s = q.shape[0]
    nb = _steps(s, rows)
    assert rows % (2 * DN_CHUNK) == 0
    fwd = lambda h, b: (b, h)
    bwd = lambda h, b: (nb - 1 - b, h)
    head = lambda im: pl.BlockSpec((rows, heads * HEAD_DIM), im)
    gate = lambda im: pl.BlockSpec((rows, LANES), lambda h, b: (im(h, b)[0], 0))
    gate_t = lambda im: pl.BlockSpec((GATE_ROWS, rows), lambda h, b: (0, im(h, b)[0]))
    return pl.pallas_call(
        _deltanet_kernel,
        grid=(_steps(DN_HEADS, heads), nb),
        in_specs=[head(fwd), head(fwd), head(fwd), gate(fwd), gate_t(fwd),
                  head(bwd), head(bwd), head(bwd), gate(bwd), gate_t(bwd)],
        out_specs=[head(fwd), head(bwd)],
        out_shape=[jax.ShapeDtypeStruct(q.shape, F32), jax.ShapeDtypeStruct(q.shape, F32)],
        scratch_shapes=[pltpu.VMEM((2 * heads, HEAD_DIM, HEAD_DIM), F32)],
        compiler_params=_params("parallel", "arbitrary"),
        name="deltanet",
    )(q, k, v, gates, gates_t, q, k, v, gates, gates_t)


def _t5_bucket(rel):
    nb = NUM_BUCKETS // 2
    max_exact = nb // 2
    n = jnp.abs(rel)
    large = max_exact + (jnp.log(jnp.maximum(n, max_exact).astype(F32) / max_exact)
                         / math.log(MAX_DISTANCE / max_exact) * (nb - max_exact)).astype(jnp.int32)
    large = jnp.minimum(large, nb - 1)
    return jnp.where(rel > 0, nb, 0) + jnp.where(n < max_exact, n, large)


def _swa_bias_init(bucket_ref, rb_ref, bias_ref):
    w = WINDOW
    ri = lax.broadcasted_iota(jnp.int32, (w, 3 * w), 0)
    ci = lax.broadcasted_iota(jnp.int32, (w, 3 * w), 1)
    bucket = bucket_ref[...]
    in_band = jnp.abs(ci - w - ri) <= w
    for hd in range(SWA_HEADS):
        acc = jnp.zeros((w, 3 * w), F32)
        for b in range(NUM_BUCKETS):
            acc = jnp.where(bucket == b, rb_ref[b, hd], acc)
        bias_ref[hd] = jnp.where(in_band, acc * LOG2E, -jnp.inf)


def _swa_block(t, qblk, first_row, seq, q_ref, kext, vext, bias_ref, sinks):
    w = WINDOW
    group = SWA_HEADS // SWA_KV_HEADS
    heads = range(SWA_HEADS)
    rows = slice(t * w, (t + 1) * w)
    band = slice(t * w, (t + 3) * w)
    s = [_dot_nt(q_ref[rows, hd * HEAD_DIM:(hd + 1) * HEAD_DIM].astype(BF16), kext[hd // group][band])
         * (HEAD_DIM ** -0.5 * LOG2E) + bias_ref[hd] for hd in heads]
    if t == 0 or t == qblk - 1:
        key_pos = first_row + (t - 1) * w + lax.broadcasted_iota(jnp.int32, (w, 3 * w), 1)
        if t == 0:
            s = [jnp.where(key_pos >= 0, x, -jnp.inf) for x in s]
        if t == qblk - 1:
            s = [jnp.where(key_pos < seq, x, -jnp.inf) for x in s]
    mx = [jnp.maximum(jnp.max(x, axis=1, keepdims=True), sk) for x, sk in zip(s, sinks)]
    p = [jnp.exp2(x - m) for x, m in zip(s, mx)]
    den = [jnp.sum(x, axis=1, keepdims=True) + jnp.exp2(sk - m) for x, sk, m in zip(p, sinks, mx)]
    o = [_dot(x.astype(BF16), vext[hd // group][band]) / dn for hd, x, dn in zip(heads, p, den)]
    return jnp.concatenate([x.astype(BF16) for x in o], axis=1)


def _mix_mem_kernel(of_ref, ob_ref, z_ref, qs_ref, kp_ref, kc_ref, kn_ref, vp_ref, vc_ref, vn_ref, bucket_ref, rb_ref,
                    sink_ref, dng_ref, wout_ref, x_ref, gx_ref, wq_ref, k_ref, v_ref, wo_ref, gf_ref,
                    h2_ref, f_ref, bias_ref, *, seq):
    n = pl.program_id(0)
    rows_per_step = h2_ref.shape[0]
    qblk = rows_per_step // WINDOW

    @pl.when(n == 0)
    def _():
        _swa_bias_init(bucket_ref, rb_ref, bias_ref)

    kext, vext = [], []
    for kvh in range(SWA_KV_HEADS):
        kcols = slice(kvh * HEAD_DIM, (kvh + 1) * HEAD_DIM)
        kext.append(jnp.concatenate([kp_ref[:, kcols], kc_ref[:, kcols], kn_ref[:, kcols]], axis=0).astype(BF16))
        vext.append(jnp.concatenate([vp_ref[:, kcols], vc_ref[:, kcols], vn_ref[:, kcols]], axis=0).astype(BF16))
    sinks = [sink_ref[hd] * LOG2E for hd in range(SWA_HEADS)]
    for t0 in range(0, qblk, 2):
        blocks = range(t0, min(t0 + 2, qblk))
        y_sw = [_swa_block(t, qblk, n * rows_per_step, seq, qs_ref, kext, vext, bias_ref, sinks) for t in blocks]
        _mix_mem_rows([slice(t * WINDOW, (t + 1) * WINDOW) for t in blocks], y_sw, HEAD_DIM ** -0.5, of_ref, ob_ref,
                      z_ref, dng_ref, wout_ref, x_ref, gx_ref, wq_ref, k_ref, v_ref, wo_ref, gf_ref, h2_ref, f_ref)


def _mix_mem_rows(batches, y_sw, scale, of_ref, ob_ref, z_ref, dng_ref, wout_ref, x_ref, gx_ref, wq_ref, k_ref, v_ref,
                  wo_ref, gf_ref, h2_ref, f_ref):
    dn = DN_HEADS * HEAD_DIM
    y_dn = []
    for rows in batches:
        parts = []
        for hd in range(DN_HEADS):
            cols = slice(hd * HEAD_DIM, (hd + 1) * HEAD_DIM)
            z = z_ref[rows, cols]
            y = _rms(of_ref[rows, cols] + ob_ref[rows, cols], dng_ref[...]) * (z * jax.nn.sigmoid(z))
            parts.append(y.astype(BF16))
        y_dn.append(jnp.concatenate(parts, axis=1))
    h1 = [x_ref[rows, :] + _dot(a, wout_ref[:dn, :]) for rows, a in zip(batches, y_dn)]
    h1 = [h + _dot(sw, wout_ref[dn:, :]) for h, sw in zip(h1, y_sw)]
    q = [_dot(_rms(h, gx_ref[...]).astype(BF16), wq_ref[...]) for h in h1]
    att = []
    for qb in q:
        heads = []
        for hd in range(MEM_HEADS):
            cols = slice(hd * HEAD_DIM, (hd + 1) * HEAD_DIM)
            s = _dot_nt(qb[:, cols].astype(BF16), k_ref[:, cols]) * scale
            p = jnp.exp(s - jnp.max(s, axis=1, keepdims=True))
            den = jnp.sum(p, axis=1, keepdims=True)
            heads.append((_dot(p.astype(BF16), v_ref[:, cols]) / den).astype(BF16))
        att.append(jnp.concatenate(heads, axis=1))
    h2 = [h + _dot(a, wo_ref[...]) for h, a in zip(h1, att)]
    for rows, h in zip(batches, h2):
        h2_ref[rows, :] = h
        f_ref[rows, :] = _rms(h, gf_ref[...]).astype(f_ref.dtype)


def _mix_mem(o_f, o_b, proj, bucket, rel_bias, sink, dn_g, w_out, x, gx, wq, k, v, wo, gf, *, tm):
    s, d = x.shape
    w = WINDOW
    dn = DN_HEADS * HEAD_DIM
    qw = SWA_HEADS * HEAD_DIM
    kvw = SWA_KV_HEADS * HEAD_DIM
    qblk = _steps(tm, w)
    nb = _steps(s, w)
    row = lambda width, col=0: pl.BlockSpec((tm, width), lambda i: (i, col))
    halo = lambda off, lo: pl.BlockSpec(
        (w, kvw), lambda i: (jnp.clip(i * qblk - 1 if lo else (i + 1) * qblk, 0, nb - 1), off // kvw))
    const = lambda a: pl.BlockSpec(a.shape, lambda i: (0, 0), pipeline_mode=pl.Buffered(1))
    smem = pl.BlockSpec(memory_space=pltpu.SMEM)
    dn_g, gx, gf = dn_g.reshape(1, HEAD_DIM), gx.reshape(1, d), gf.reshape(1, d)
    return pl.pallas_call(
        functools.partial(_mix_mem_kernel, seq=s),
        grid=(_steps(s, tm),),
        in_specs=[row(dn), row(dn), row(dn, Z_OFF // dn), row(qw, QSW_OFF // qw),
                  halo(KSW_OFF, True), row(kvw, KSW_OFF // kvw), halo(KSW_OFF, False),
                  halo(VSW_OFF, True), row(kvw, VSW_OFF // kvw), halo(VSW_OFF, False),
                  const(bucket), smem, smem, const(dn_g), const(w_out), row(d),
                  const(gx), const(wq), const(k), const(v), const(wo), const(gf)],
        out_specs=[row(d), row(d)],
        out_shape=[jax.ShapeDtypeStruct((s, d), F32), jax.ShapeDtypeStruct((s, d), BF16)],
        scratch_shapes=[pltpu.VMEM((SWA_HEADS, w, 3 * w), F32)],
        compiler_params=_params("arbitrary"),
        name="mix_mem",
    )(o_f, o_b, proj, proj, proj, proj, proj, proj, proj, proj, bucket, rel_bias, sink, dn_g, w_out, x,
      gx, wq, k, v, wo, gf)


def _glu_kernel(f_ref, wg_ref, wu_ref, wd_ref, o_ref, wd16_ref):
    f = f_ref[...]
    a = _dot(f, wg_ref[...].astype(BF16))
    o_ref[...] = (a * jax.nn.sigmoid(a) * _dot(f, wu_ref[...].astype(BF16))).astype(o_ref.dtype)
    wd16_ref[...] = wd_ref[...].astype(wd16_ref.dtype)


def _ffn_glu(f, wg, wu, wd, *, tm, tn):
    s, d = f.shape
    dff = wg.shape[1]
    ni, nj = _steps(s, tm), _steps(dff, tn)
    wd_rows = _steps(wd.shape[0], ni * nj)
    wd_spec = pl.BlockSpec((wd_rows, wd.shape[1]), lambda i, j: (i * nj + j, 0))
    return pl.pallas_call(
        _glu_kernel,
        grid=(ni, nj),
        in_specs=[pl.BlockSpec((tm, d), lambda i, j: (i, 0)),
                  pl.BlockSpec((d, tn), lambda i, j: (0, j)),
                  pl.BlockSpec((d, tn), lambda i, j: (0, j)),
                  wd_spec],
        out_specs=[pl.BlockSpec((tm, tn), lambda i, j: (i, j)), wd_spec],
        out_shape=[jax.ShapeDtypeStruct((s, dff), BF16), jax.ShapeDtypeStruct(wd.shape, BF16)],
        compiler_params=_params("parallel", "parallel"),
        name="ffn_glu",
    )(f, wg, wu, wd)


def _down_kernel(a_ref, w_ref, h_ref, g_ref, o_ref):
    half = o_ref.shape[0] // 2
    for r in (0, half):
        rows = slice(r, r + half)
        o_ref[rows, :] = _rms(h_ref[rows, :] + _dot(a_ref[rows, :], w_ref[...]), g_ref[...])


def _ffn_down(act, wd, h, g, *, tm):
    s, dff = act.shape
    d = wd.shape[1]
    return pl.pallas_call(
        _down_kernel,
        grid=(_steps(s, tm),),
        in_specs=[pl.BlockSpec((tm, dff), lambda i: (i, 0)),
                  pl.BlockSpec((dff, d), lambda i: (0, 0), pipeline_mode=pl.Buffered(1)),
                  pl.BlockSpec((tm, d), lambda i: (i, 0)),
                  pl.BlockSpec((1, d), lambda i: (0, 0))],
        out_specs=pl.BlockSpec((tm, d), lambda i: (i, 0)),
        out_shape=jax.ShapeDtypeStruct((s, d), F32),
        compiler_params=_params("parallel"),
        name="ffn_down",
    )(act, wd, h, g.reshape(1, d))


def _gate_row(fwd, bwd):
    row = jnp.zeros((1, LANES), F32)
    row = row.at[0, GATE_G_F:GATE_G_F + DN_HEADS].set(fwd.astype(F32))
    return row.at[0, GATE_G_B:GATE_G_B + DN_HEADS].set(bwd.astype(F32))


def _pick(n, *cands):
    for c in cands:
        if n % c == 0:
            return c
    return n


def kernel(x, mem, norm_mix_g, w_in, conv_w, a_log_f, a_log_b, dt_bias_f, dt_bias_b, dn_norm_g, attn_sink, rel_bias, w_out, norm_x_g, norm_mem_g, w_q_mem, w_kv_mem, w_o_mem, norm_ffn_g, w_gate, w_up, w_down, norm_final_g):
    batch, s, d = x.shape
    assert batch == 1 and mem.shape[0] == 1 and w_in.shape[0] == 1, "single sequence, single layer"
    w = WINDOW
    rel = (jnp.arange(3 * w)[None, :] - w) - jnp.arange(w)[:, None]
    bucket = _t5_bucket(rel).astype(jnp.int32)
    mem_dim = MEM_HEADS * HEAD_DIM

    h = x.reshape(s, d)
    mem2 = mem.reshape(mem.shape[1], d)
    tm_big = _pick(s, 1024, 512, 256, 128)
    tm_mid = _pick(s, 512, 256, 128)
    w_r = _w_in_layout(w_in[0].T, cols=256)
    proj = _rms_matmul(h, norm_mix_g[0], w_r, tm=tm_big, tn=_pick(PROJ_WIDTH, 1920, 1152, 640, 128), name="in_proj",
                       w_transposed=True)

    q, k, v, gates, gates_t = _dn_prep(proj, conv_w[0], _gate_row(a_log_f[0], a_log_b[0]),
                                       _gate_row(dt_bias_f[0], dt_bias_b[0]), rows=_pick(s, 256, 128))
    o_f, o_b = _deltanet(q, k, v, gates, gates_t, rows=_pick(s, 512, 256, 128), heads=8)
    kv = _rms_matmul(mem2, norm_mem_g[0], w_kv_mem[0].astype(BF16), tm=mem2.shape[0], tn=2 * mem_dim,
                     name="mem_kv").astype(BF16)
    h, f = _mix_mem(o_f, o_b, proj, bucket, rel_bias.astype(F32), attn_sink[0].astype(F32), dn_norm_g[0],
                    w_out[0].astype(BF16), h, norm_x_g[0], w_q_mem[0].astype(BF16), kv[:, :mem_dim], kv[:, mem_dim:],
                    w_o_mem[0].astype(BF16), norm_ffn_g[0], tm=_pick(s, 256))

    act, w_down16 = _ffn_glu(f, w_gate[0], w_up[0], w_down[0], tm=tm_big, tn=512)
    out = _ffn_down(act, w_down16, h, norm_final_g, tm=tm_mid)
    return out.reshape(batch, s, d)
```

```python
import functools
import math

import jax
import jax.numpy as jnp
from jax import lax
from jax.experimental import pallas as pl
from jax.experimental.pallas import tpu as pltpu

F32 = jnp.float32
BF16 = jnp.bfloat16

RMS_EPS = 1e-6
L2_EPS = 1e-6
HEAD_DIM = 128
DN_HEADS = 8
DN_CHUNK = 64
DN_SUB = 16
CONV_WIDTH = 5
CONV_HALO = 8
SWA_HEADS = 8
SWA_KV_HEADS = 2
WINDOW = 128
NUM_BUCKETS = 32
MAX_DISTANCE = 128
MEM_HEADS = 4
LOG2E = 1.4426950408889634
LANES = 128
ROW_BATCH = 128

DN_QKV = 3 * DN_HEADS * HEAD_DIM
Z_OFF = DN_QKV
QSW_OFF = Z_OFF + DN_HEADS * HEAD_DIM
KSW_OFF = QSW_OFF + SWA_HEADS * HEAD_DIM
VSW_OFF = KSW_OFF + SWA_KV_HEADS * HEAD_DIM
GATE_OFF = VSW_OFF + SWA_KV_HEADS * HEAD_DIM
PROJ_WIDTH = GATE_OFF + LANES
GATE_BETA_F, GATE_BETA_B, GATE_G_F, GATE_G_B = 0, DN_HEADS, 2 * DN_HEADS, 3 * DN_HEADS
GATE_ROWS = 4 * DN_HEADS

VMEM_LIMIT_V7X = 56 * 1024 * 1024


def _params(*sem):
    return pltpu.CompilerParams(dimension_semantics=sem, vmem_limit_bytes=VMEM_LIMIT_V7X)


def _steps(dim, tile):
    assert dim % tile == 0, f"tile {tile} does not divide {dim}"
    return dim // tile


def _dot(a, b):
    return jnp.dot(a, b, preferred_element_type=F32)


def _dot_nt(a, b):
    return lax.dot_general(a, b, (((1,), (1,)), ((), ())), preferred_element_type=F32)


def _dot_tn(a, b):
    return lax.dot_general(a, b, (((0,), (0,)), ((), ())), preferred_element_type=F32)


def _rms(x, g):
    return x * lax.rsqrt(jnp.mean(x * x, axis=-1, keepdims=True) + RMS_EPS) * g


def _cast_specs(weights, nsteps, step_of):
    specs = [pl.BlockSpec((_steps(a.shape[0], nsteps), a.shape[1]), lambda *idx: (step_of(*idx), 0)) for a in weights]
    return specs, [jax.ShapeDtypeStruct(a.shape, BF16) for a in weights]


def _cast_slices(srcs, dsts):
    for src, dst in zip(srcs, dsts):
        dst[...] = src[...].astype(dst.dtype)


def _w_in_kernel(w_ref, o_ref):
    gate_lo = QSW_OFF
    o_ref[:gate_lo, :] = w_ref[:gate_lo, :].astype(o_ref.dtype)
    o_ref[gate_lo:GATE_OFF, :] = w_ref[gate_lo + GATE_ROWS:, :].astype(o_ref.dtype)
    o_ref[GATE_OFF:GATE_OFF + GATE_ROWS, :] = w_ref[gate_lo:gate_lo + GATE_ROWS, :].astype(o_ref.dtype)
    o_ref[GATE_OFF + GATE_ROWS:, :] = jnp.zeros((LANES - GATE_ROWS, o_ref.shape[1]), o_ref.dtype)


def _w_in_layout(w_t, *, cols):
    n, k = w_t.shape
    assert n == PROJ_WIDTH - (LANES - GATE_ROWS)
    return pl.pallas_call(
        _w_in_kernel,
        grid=(_steps(k, cols),),
        in_specs=[pl.BlockSpec((n, cols), lambda i: (0, i))],
        out_specs=pl.BlockSpec((PROJ_WIDTH, cols), lambda i: (0, i)),
        out_shape=jax.ShapeDtypeStruct((PROJ_WIDTH, k), BF16),
        compiler_params=_params("parallel"),
        name="w_in_layout",
    )(w_t)


def _rms_matmul_kernel(x_ref, g_ref, w_ref, o_ref, n_ref, *, w_transposed):
    dot = _dot_nt if w_transposed else _dot

    @pl.when(pl.program_id(1) == 0)
    def _():
        half = max(x_ref.shape[0] // 2, ROW_BATCH)
        for r in range(0, x_ref.shape[0], half):
            rows = slice(r, r + half)
            n = _rms(x_ref[rows, :], g_ref[...]).astype(n_ref.dtype)
            n_ref[rows, :] = n
            o_ref[rows, :] = dot(n, w_ref[...]).astype(o_ref.dtype)

    @pl.when(pl.program_id(1) != 0)
    def _():
        o_ref[...] = dot(n_ref[...], w_ref[...]).astype(o_ref.dtype)


def _rms_matmul(x, g, w, *, tm, tn, name, w_transposed=False):
    m, k = x.shape
    n = w.shape[0] if w_transposed else w.shape[1]
    w_spec = pl.BlockSpec((tn, k), lambda i, j: (j, 0)) if w_transposed else pl.BlockSpec((k, tn), lambda i, j: (0, j))
    return pl.pallas_call(
        functools.partial(_rms_matmul_kernel, w_transposed=w_transposed),
        grid=(_steps(m, tm), _steps(n, tn)),
        in_specs=[pl.BlockSpec((tm, k), lambda i, j: (i, 0)),
                  pl.BlockSpec((1, k), lambda i, j: (0, 0)),
                  w_spec],
        out_specs=pl.BlockSpec((tm, tn), lambda i, j: (i, j)),
        out_shape=jax.ShapeDtypeStruct((m, n), F32),
        scratch_shapes=[pltpu.VMEM((tm, k), BF16)],
        compiler_params=_params("parallel", "arbitrary"),
        name=name,
    )(x, g.reshape(1, k), w)


def _split3(x):
    hi = x.astype(BF16)
    r = x - hi.astype(F32)
    mid = r.astype(BF16)
    lo = (r - mid.astype(F32)).astype(BF16)
    return hi, mid, lo


def _prep_kernel(main_ref, prev_ref, next_ref, gate_ref, cw_ref, alog_ref, dt_ref,
                 q_ref, k_ref, v_ref, go_ref, gt_ref):
    i = pl.program_id(0)
    rows = main_ref.shape[0]
    halo = CONV_HALO
    pad = (CONV_WIDTH - 1) // 2

    first = i == 0
    last = i == pl.num_programs(0) - 1
    for s in range(3 * DN_HEADS):
        cols = slice(s * HEAD_DIM, (s + 1) * HEAD_DIM)
        xe = jnp.concatenate([jnp.where(first, 0.0, prev_ref[:, cols]), main_ref[:, cols],
                              jnp.where(last, 0.0, next_ref[:, cols])], axis=0)
        acc = cw_ref[pad:pad + 1, cols] * xe[halo:halo + rows]
        for j in range(CONV_WIDTH):
            if j != pad:
                shifted = pltpu.roll(xe, shift=(pad - j) % xe.shape[0], axis=0)[halo:halo + rows]
                acc = acc + cw_ref[j:j + 1, cols] * shifted
        y = acc * jax.nn.sigmoid(acc)
        if s < 2 * DN_HEADS:
            inv = lax.rsqrt(jnp.sum(y * y, axis=-1, keepdims=True) + L2_EPS)
            y = y * (inv * (HEAD_DIM ** -0.5) if s < DN_HEADS else inv)
        if s < DN_HEADS:
            q_ref[:, cols] = y
        elif s < 2 * DN_HEADS:
            k_ref[:, slice((s - DN_HEADS) * HEAD_DIM, (s - DN_HEADS + 1) * HEAD_DIM)] = y
        else:
            v_ref[:, slice((s - 2 * DN_HEADS) * HEAD_DIM, (s - 2 * DN_HEADS + 1) * HEAD_DIM)] = y

    t = gate_ref[...]
    beta = jax.nn.sigmoid(t)
    a = t + dt_ref[...]
    softplus = jnp.maximum(a, 0.0) + jnp.log1p(jnp.exp(-jnp.abs(a)))
    g = -jnp.exp(alog_ref[...]) * softplus

    ri = lax.broadcasted_iota(jnp.int32, (rows, rows), 0)
    ci = lax.broadcasted_iota(jnp.int32, (rows, rows), 1)
    shift = DN_CHUNK.bit_length() - 1
    same_chunk = (ri >> shift) == (ci >> shift)
    lower = jnp.where(same_chunk & (ci <= ri), 1.0, 0.0).astype(BF16)
    upper = jnp.where(same_chunk & (ci >= ri), 1.0, 0.0).astype(BF16)
    parts = _split3(g)
    gc_f = _dot(lower, parts[0]) + _dot(lower, parts[1]) + _dot(lower, parts[2])
    gc_b = _dot(upper, parts[0]) + _dot(upper, parts[1]) + _dot(upper, parts[2])

    col = lax.broadcasted_iota(jnp.int32, t.shape, 1)
    out = jnp.where(col < GATE_G_F, beta, jnp.where(col < GATE_G_B, gc_f, gc_b))
    go_ref[...] = out
    gt_ref[...] = out.T[0:GATE_ROWS, :]


def _dn_prep(proj, conv_w, alog_row, dt_row, *, rows):
    s = proj.shape[0]
    nblk = _steps(s, rows)
    hb = rows // CONV_HALO
    last_halo = s // CONV_HALO - 1
    head_cols = DN_HEADS * HEAD_DIM
    return pl.pallas_call(
        _prep_kernel,
        grid=(nblk,),
        in_specs=[pl.BlockSpec((rows, DN_QKV), lambda i: (i, 0)),
                  pl.BlockSpec((CONV_HALO, DN_QKV), lambda i: (jnp.maximum(i * hb - 1, 0), 0)),
                  pl.BlockSpec((CONV_HALO, DN_QKV), lambda i: (jnp.minimum((i + 1) * hb, last_halo), 0)),
                  pl.BlockSpec((rows, LANES), lambda i: (i, GATE_OFF // LANES)),
                  pl.BlockSpec((CONV_WIDTH, DN_QKV), lambda i: (0, 0)),
                  pl.BlockSpec((1, LANES), lambda i: (0, 0)),
                  pl.BlockSpec((1, LANES), lambda i: (0, 0))],
        out_specs=[pl.BlockSpec((rows, head_cols), lambda i: (i, 0)),
                   pl.BlockSpec((rows, head_cols), lambda i: (i, 0)),
                   pl.BlockSpec((rows, head_cols), lambda i: (i, 0)),
                   pl.BlockSpec((rows, LANES), lambda i: (i, 0)),
                   pl.BlockSpec((GATE_ROWS, rows), lambda i: (0, i))],
        out_shape=[jax.ShapeDtypeStruct((s, head_cols), F32),
                   jax.ShapeDtypeStruct((s, head_cols), F32),
                   jax.ShapeDtypeStruct((s, head_cols), F32),
                   jax.ShapeDtypeStruct((s, LANES), F32),
                   jax.ShapeDtypeStruct((GATE_ROWS, s), F32)],
        compiler_params=_params("parallel"),
        name="dn_prep",
    )(proj, proj, proj, proj, conv_w, alog_row, dt_row)


def _unit_tri_inverse(ms, eye, same_sub, _mm):
    c = eye.shape[0]
    mds = [jnp.where(same_sub, m, 0.0) for m in ms]
    es = [m - md for m, md in zip(ms, mds)]
    pw = [_mm(md, md) for md in mds]
    xs = [eye - md for md in mds]
    for _ in range(DN_SUB.bit_length() - 3):
        both = [_mm(jnp.concatenate([a, x], axis=0), a) for a, x in zip(pw, xs)]
        pw = [b[:c] for b in both]
        xs = [x + b[c:] for x, b in zip(xs, both)]
    xs = [x + _mm(x, a) for x, a in zip(xs, pw)]
    ns = [_mm(x, e) for x, e in zip(xs, es)]
    n2 = [_mm(n, n) for n in ns]
    ys = [_mm(n, (eye - n) + b) for n, b in zip(ns, n2)]
    return [x - _mm(y, x) for x, y in zip(xs, ys)]


def _deltanet_kernel(qf, kf, vf, gf, gtf, qb, kb, vb, gb, gtb, *rest):
    n_cast = (len(rest) - 3) // 2
    cast_in, (of_ref, ob_ref), cast_out = rest[:n_cast], rest[n_cast:n_cast + 2], rest[n_cast + 2:2 * n_cast + 2]
    state_ref = rest[-1]
    _cast_slices(cast_in, cast_out)
    c_len = DN_CHUNK
    nchunk = qf.shape[0] // c_len
    heads = qf.shape[1] // HEAD_DIM
    head0 = pl.program_id(0) * heads

    @pl.when(pl.program_id(1) == 0)
    def _():
        state_ref[...] = jnp.zeros_like(state_ref)

    p_len = 2 * c_len
    ri = lax.broadcasted_iota(jnp.int32, (c_len, p_len), 0)
    cl = lax.broadcasted_iota(jnp.int32, (c_len, p_len), 1)
    ci = cl & (c_len - 1)
    first = cl < c_len
    eye = jnp.where(ri == ci, 1.0, 0.0)
    sub_shift = DN_SUB.bit_length() - 1
    same_sub = (ri >> sub_shift) == (ci >> sub_shift)
    lane = lax.broadcasted_iota(jnp.int32, (p_len, LANES), 1)
    upper_rows = lax.broadcasted_iota(jnp.int32, (p_len, 1), 0) >= c_len
    first_tall = lax.broadcasted_iota(jnp.int32, (c_len + HEAD_DIM, p_len), 1) < c_len
    masks = {False: (ci <= ri, ci < ri), True: (ci >= ri, ci > ri)}
    refs = {False: (qf, kf, vf, gf, gtf, of_ref), True: (qb, kb, vb, gb, gtb, ob_ref)}

    def pair_mm(x, p):
        blockdiag = jnp.concatenate([jnp.where(first, p, 0.0), jnp.where(first, 0.0, p)], axis=0)
        return _dot(x.astype(BF16), blockdiag.astype(BF16))

    chains = [(rev, hd) for rev in (False, True) for hd in range(heads)]

    g_rows = {}
    for rev, hd in chains:
        g_col = head0 + hd + (GATE_G_B if rev else GATE_G_F)
        g_rows[rev, hd] = refs[rev][4][pl.ds(g_col, 1), :]

    def load(rev, hd, p):
        q_ref, k_ref, v_ref, g_ref = refs[rev][:4]
        rows = slice(p * p_len, (p + 1) * p_len)
        cols = slice(hd * HEAD_DIM, (hd + 1) * HEAD_DIM)
        gates = g_ref[rows, :]
        beta_col = head0 + hd + (GATE_BETA_B if rev else GATE_BETA_F)
        g_col = head0 + hd + (GATE_G_B if rev else GATE_G_F)
        beta = jnp.sum(jnp.where(lane == beta_col, gates, 0.0), axis=1, keepdims=True)
        gcol = jnp.sum(jnp.where(lane == g_col, gates, 0.0), axis=1, keepdims=True)
        grow = g_rows[rev, hd][:, rows]
        if rev:
            glast = (grow[:, 0:1], grow[:, c_len:c_len + 1])
        else:
            glast = (grow[:, c_len - 1:c_len], grow[:, p_len - 1:p_len])
        return q_ref[rows, cols], k_ref[rows, cols], v_ref[rows, cols], beta, gcol, grow, glast

    lo, hi = slice(0, c_len), slice(c_len, p_len)

    zeros = jnp.zeros((c_len, HEAD_DIM), BF16)

    def phase_a(inst):
        data = [load(*i) for i in inst]
        kbeta = [k * beta for (_, k, _, beta, _, _, _) in data]
        prod = []
        for kb_, (q, k, _, _, _, _, _) in zip(kbeta, data):
            lhs = jnp.concatenate([jnp.concatenate([kb_[lo], kb_[hi]], axis=1),
                                   jnp.concatenate([q[lo], q[hi]], axis=1)], axis=0).astype(BF16)
            kb16 = k.astype(BF16)
            rhs = jnp.concatenate([jnp.concatenate([kb16[lo], zeros], axis=1),
                                   jnp.concatenate([zeros, kb16[hi]], axis=1)], axis=0)
            prod.append(_dot_nt(lhs, rhs))
        decay = [jnp.exp(jnp.where(masks[rev][0], jnp.where(first, gcol[lo], gcol[hi]) - grow, -jnp.inf))
                 for (rev, _, _), (_, _, _, _, gcol, grow, _) in zip(inst, data)]
        ms = [jnp.where(masks[rev][1], p[lo] * dec, 0.0) for (rev, _, _), p, dec in zip(inst, prod, decay)]
        a_mat = [p[hi] * dec for p, dec in zip(prod, decay)]
        ts = _unit_tri_inverse(ms, eye, same_sub, pair_mm)
        egs = [jnp.exp(gcol) for (_, _, _, _, gcol, _, _) in data]
        uw = []
        for t, kb_, eg, (_, _, v, beta, _, _, _) in zip(ts, kbeta, egs, data):
            lhs = jnp.concatenate([jnp.where(first, t, 0.0), jnp.where(first, 0.0, t)], axis=0).astype(BF16)
            rhs = jnp.concatenate([(v * beta).astype(BF16), (kb_ * eg).astype(BF16)], axis=1)
            uw.append(_dot(lhs, rhs))
        qdec = [q * eg for eg, (q, _, _, _, _, _, _) in zip(egs, data)]
        kdec_t = [(k * jnp.exp(jnp.where(upper_rows, glast[1], glast[0]) - gcol)).T
                  for (_, k, _, _, gcol, _, glast) in data]
        tall = [jnp.concatenate([a_, kt], axis=0) for a_, kt in zip(a_mat, kdec_t)]
        pre = {}
        for (rev, hd, p), x, qd, tl, (_, _, _, _, _, _, glast) in zip(inst, uw, qdec, tall, data):
            for half, rs in enumerate((lo, hi)):
                wq = jnp.concatenate([x[rs, HEAD_DIM:], qd[rs]], axis=0).astype(BF16)
                keep = first_tall if half == 0 else ~first_tall
                pre[rev, hd, 2 * p + half] = (x[rs, :HEAD_DIM], wq, jnp.where(keep, tl, 0.0).astype(BF16),
                                              jnp.exp(glast[half]))
        return pre

    def phase_b(states, pre, t):
        chunk = {False: t, True: nchunk - 1 - t}
        cur = [pre[rev, hd, chunk[rev]] for rev, hd in chains]
        ws = [_dot(wq, s_.astype(BF16)) for (_, wq, _, _), s_ in zip(cur, states)]
        v_new = [(u - x[lo]).astype(BF16) for (u, _, _, _), x in zip(cur, ws)]
        upd = [_dot(tl, jnp.concatenate([vn, vn], axis=0)) for (_, _, tl, _), vn in zip(cur, v_new)]
        for (rev, hd), x, y in zip(chains, ws, upd):
            c = chunk[rev]
            refs[rev][5][c * c_len:(c + 1) * c_len, hd * HEAD_DIM:(hd + 1) * HEAD_DIM] = x[hi] + y[lo]
        return [s_ * g_ + x[c_len:] for (_, _, _, g_), s_, x in zip(cur, states, upd)]

    npair = nchunk // 2
    states = [state_ref[n] for n in range(len(chains))]
    for s in range(npair):
        pre = phase_a([(rev, hd, npair - 1 - s if rev else s) for rev, hd in chains])
        for t in (2 * s, 2 * s + 1):
            states = phase_b(states, pre, t)
    for n, s_ in enumerate(states):
        state_ref[n] = s_


def _deltanet(q, k, v, gates, gates_t, weights, *, rows, heads):
    s = q.shape[0]
    nb = _steps(s, rows)
    nh = _steps(DN_HEADS, heads)
    assert rows % (2 * DN_CHUNK) == 0
    fwd = lambda h, b: (b, h)
    bwd = lambda h, b: (nb - 1 - b, h)
    head = lambda im: pl.BlockSpec((rows, heads * HEAD_DIM), im)
    gate = lambda im: pl.BlockSpec((rows, LANES), lambda h, b: (im(h, b)[0], 0))
    gate_t = lambda im: pl.BlockSpec((GATE_ROWS, rows), lambda h, b: (0, im(h, b)[0]))
    cast_specs, cast_shapes = _cast_specs(weights, nh * nb, lambda h, b: h * nb + b)
    out = pl.pallas_call(
        _deltanet_kernel,
        grid=(nh, nb),
        in_specs=[head(fwd), head(fwd), head(fwd), gate(fwd), gate_t(fwd),
                  head(bwd), head(bwd), head(bwd), gate(bwd), gate_t(bwd)] + cast_specs,
        out_specs=[head(fwd), head(bwd)] + cast_specs,
        out_shape=[jax.ShapeDtypeStruct(q.shape, F32), jax.ShapeDtypeStruct(q.shape, F32)] + cast_shapes,
        scratch_shapes=[pltpu.VMEM((2 * heads, HEAD_DIM, HEAD_DIM), F32)],
        compiler_params=_params("parallel", "arbitrary"),
        name="deltanet",
    )(q, k, v, gates, gates_t, q, k, v, gates, gates_t, *weights)
    return out[0], out[1], out[2:]


def _t5_bucket(rel):
    nb = NUM_BUCKETS // 2
    max_exact = nb // 2
    n = jnp.abs(rel)
    large = max_exact + (jnp.log(jnp.maximum(n, max_exact).astype(F32) / max_exact)
                         / math.log(MAX_DISTANCE / max_exact) * (nb - max_exact)).astype(jnp.int32)
    large = jnp.minimum(large, nb - 1)
    return jnp.where(rel > 0, nb, 0) + jnp.where(n < max_exact, n, large)


def _swa_bias_init(bucket_ref, rb_ref, bias_ref):
    w = WINDOW
    ri = lax.broadcasted_iota(jnp.int32, (w, 3 * w), 0)
    ci = lax.broadcasted_iota(jnp.int32, (w, 3 * w), 1)
    bucket = bucket_ref[...]
    in_band = jnp.abs(ci - w - ri) <= w
    for hd in range(SWA_HEADS):
        acc = jnp.zeros((w, 3 * w), F32)
        for b in range(NUM_BUCKETS):
            acc = jnp.where(bucket == b, rb_ref[b, hd], acc)
        bias_ref[hd] = jnp.where(in_band, acc * LOG2E, -jnp.inf)


def _swa_block(t, qblk, first_row, seq, q_ref, kext, vext, bias_ref, sinks):
    w = WINDOW
    group = SWA_HEADS // SWA_KV_HEADS
    heads = range(SWA_HEADS)
    rows = slice(t * w, (t + 1) * w)
    band = slice(t * w, (t + 3) * w)
    s = [_dot_nt(q_ref[rows, hd * HEAD_DIM:(hd + 1) * HEAD_DIM].astype(BF16), kext[hd // group][band])
         * (HEAD_DIM ** -0.5 * LOG2E) + bias_ref[hd] for hd in heads]
    if t == 0 or t == qblk - 1:
        key_pos = first_row + (t - 1) * w + lax.broadcasted_iota(jnp.int32, (w, 3 * w), 1)
        if t == 0:
            s = [jnp.where(key_pos >= 0, x, -jnp.inf) for x in s]
        if t == qblk - 1:
            s = [jnp.where(key_pos < seq, x, -jnp.inf) for x in s]
    mx = [jnp.maximum(jnp.max(x, axis=1, keepdims=True), sk) for x, sk in zip(s, sinks)]
    p = [jnp.exp2(x - m) for x, m in zip(s, mx)]
    den = [jnp.sum(x, axis=1, keepdims=True) + jnp.exp2(sk - m) for x, sk, m in zip(p, sinks, mx)]
    o = [_dot(x.astype(BF16), vext[hd // group][band]) / dn for hd, x, dn in zip(heads, p, den)]
    return jnp.concatenate([x.astype(BF16) for x in o], axis=1)


def _mix_mem_kernel(of_ref, ob_ref, z_ref, qs_ref, kp_ref, kc_ref, kn_ref, vp_ref, vc_ref, vn_ref, bucket_ref, rb_ref,
                    sink_ref, dng_ref, wout_ref, x_ref, gx_ref, wq_ref, k_ref, v_ref, wo_ref, gf_ref, *rest, seq):
    n_cast = (len(rest) - 3) // 2
    cast_in, (h2_ref, f_ref), cast_out = rest[:n_cast], rest[n_cast:n_cast + 2], rest[n_cast + 2:2 * n_cast + 2]
    bias_ref = rest[-1]
    _cast_slices(cast_in, cast_out)
    n = pl.program_id(0)
    rows_per_step = h2_ref.shape[0]
    qblk = rows_per_step // WINDOW

    @pl.when(n == 0)
    def _():
        _swa_bias_init(bucket_ref, rb_ref, bias_ref)

    kext, vext = [], []
    for kvh in range(SWA_KV_HEADS):
        kcols = slice(kvh * HEAD_DIM, (kvh + 1) * HEAD_DIM)
        kext.append(jnp.concatenate([kp_ref[:, kcols], kc_ref[:, kcols], kn_ref[:, kcols]], axis=0).astype(BF16))
        vext.append(jnp.concatenate([vp_ref[:, kcols], vc_ref[:, kcols], vn_ref[:, kcols]], axis=0).astype(BF16))
    sinks = [sink_ref[hd] * LOG2E for hd in range(SWA_HEADS)]
    for t0 in range(0, qblk, 2):
        blocks = range(t0, min(t0 + 2, qblk))
        y_sw = [_swa_block(t, qblk, n * rows_per_step, seq, qs_ref, kext, vext, bias_ref, sinks) for t in blocks]
        _mix_mem_rows([slice(t * WINDOW, (t + 1) * WINDOW) for t in blocks], y_sw, HEAD_DIM ** -0.5, of_ref, ob_ref,
                      z_ref, dng_ref, wout_ref, x_ref, gx_ref, wq_ref, k_ref, v_ref, wo_ref, gf_ref, h2_ref, f_ref)


def _mix_mem_rows(batches, y_sw, scale, of_ref, ob_ref, z_ref, dng_ref, wout_ref, x_ref, gx_ref, wq_ref, k_ref, v_ref,
                  wo_ref, gf_ref, h2_ref, f_ref):
    dn = DN_HEADS * HEAD_DIM
    y_dn = []
    for rows in batches:
        parts = []
        for hd in range(DN_HEADS):
            cols = slice(hd * HEAD_DIM, (hd + 1) * HEAD_DIM)
            z = z_ref[rows, cols]
            y = _rms(of_ref[rows, cols] + ob_ref[rows, cols], dng_ref[...]) * (z * jax.nn.sigmoid(z))
            parts.append(y.astype(BF16))
        y_dn.append(jnp.concatenate(parts, axis=1))
    h1 = [x_ref[rows, :] + _dot(a, wout_ref[:dn, :]) for rows, a in zip(batches, y_dn)]
    h1 = [h + _dot(sw, wout_ref[dn:, :]) for h, sw in zip(h1, y_sw)]
    q = [_dot(_rms(h, gx_ref[...]).astype(BF16), wq_ref[...]) for h in h1]
    att = []
    for qb in q:
        heads = []
        for hd in range(MEM_HEADS):
            cols = slice(hd * HEAD_DIM, (hd + 1) * HEAD_DIM)
            s = _dot_nt(qb[:, cols].astype(BF16), k_ref[:, cols]) * scale
            p = jnp.exp(s - jnp.max(s, axis=1, keepdims=True))
            den = jnp.sum(p, axis=1, keepdims=True)
            heads.append((_dot(p.astype(BF16), v_ref[:, cols]) / den).astype(BF16))
        att.append(jnp.concatenate(heads, axis=1))
    h2 = [h + _dot(a, wo_ref[...]) for h, a in zip(h1, att)]
    for rows, h in zip(batches, h2):
        h2_ref[rows, :] = h
        f_ref[rows, :] = _rms(h, gf_ref[...]).astype(f_ref.dtype)


def _mix_mem(o_f, o_b, proj, bucket, rel_bias, sink, dn_g, w_out, x, gx, wq, k, v, wo, gf, weights, *, tm):
    s, d = x.shape
    w = WINDOW
    dn = DN_HEADS * HEAD_DIM
    qw = SWA_HEADS * HEAD_DIM
    kvw = SWA_KV_HEADS * HEAD_DIM
    qblk = _steps(tm, w)
    nb = _steps(s, w)
    row = lambda width, col=0: pl.BlockSpec((tm, width), lambda i: (i, col))
    halo = lambda off, lo: pl.BlockSpec(
        (w, kvw), lambda i: (jnp.clip(i * qblk - 1 if lo else (i + 1) * qblk, 0, nb - 1), off // kvw))
    const = lambda a: pl.BlockSpec(a.shape, lambda i: (0, 0), pipeline_mode=pl.Buffered(1))
    smem = pl.BlockSpec(memory_space=pltpu.SMEM)
    dn_g, gx, gf = dn_g.reshape(1, HEAD_DIM), gx.reshape(1, d), gf.reshape(1, d)
    nsteps = _steps(s, tm)
    cast_specs, cast_shapes = _cast_specs(weights, nsteps, lambda i: i)
    out = pl.pallas_call(
        functools.partial(_mix_mem_kernel, seq=s),
        grid=(nsteps,),
        in_specs=[row(dn), row(dn), row(dn, Z_OFF // dn), row(qw, QSW_OFF // qw),
                  halo(KSW_OFF, True), row(kvw, KSW_OFF // kvw), halo(KSW_OFF, False),
                  halo(VSW_OFF, True), row(kvw, VSW_OFF // kvw), halo(VSW_OFF, False),
                  const(bucket), smem, smem, const(dn_g), const(w_out), row(d),
                  const(gx), const(wq), const(k), const(v), const(wo), const(gf)] + cast_specs,
        out_specs=[row(d), row(d)] + cast_specs,
        out_shape=[jax.ShapeDtypeStruct((s, d), F32), jax.ShapeDtypeStruct((s, d), BF16)] + cast_shapes,
        scratch_shapes=[pltpu.VMEM((SWA_HEADS, w, 3 * w), F32)],
        compiler_params=_params("arbitrary"),
        name="mix_mem",
    )(o_f, o_b, proj, proj, proj, proj, proj, proj, proj, proj, bucket, rel_bias, sink, dn_g, w_out, x,
      gx, wq, k, v, wo, gf, *weights)
    return out[0], out[1], out[2:]


def _glu_kernel(f_ref, wg_ref, wu_ref, o_ref):
    f = f_ref[...]
    a = _dot(f, wg_ref[...])
    o_ref[...] = (a * jax.nn.sigmoid(a) * _dot(f, wu_ref[...])).astype(o_ref.dtype)


def _ffn_glu(f, wg, wu, *, tm, tn):
    s, d = f.shape
    dff = wg.shape[1]
    return pl.pallas_call(
        _glu_kernel,
        grid=(_steps(s, tm), _steps(dff, tn)),
        in_specs=[pl.BlockSpec((tm, d), lambda i, j: (i, 0)),
                  pl.BlockSpec((d, tn), lambda i, j: (0, j)),
                  pl.BlockSpec((d, tn), lambda i, j: (0, j))],
        out_specs=pl.BlockSpec((tm, tn), lambda i, j: (i, j)),
        out_shape=jax.ShapeDtypeStruct((s, dff), BF16),
        compiler_params=_params("parallel", "parallel"),
        name="ffn_glu",
    )(f, wg, wu)


def _down_kernel(a_ref, w_ref, h_ref, g_ref, o_ref):
    half = o_ref.shape[0] // 2
    for r in (0, half):
        rows = slice(r, r + half)
        o_ref[rows, :] = _rms(h_ref[rows, :] + _dot(a_ref[rows, :], w_ref[...]), g_ref[...])


def _ffn_down(act, wd, h, g, *, tm):
    s, dff = act.shape
    d = wd.shape[1]
    return pl.pallas_call(
        _down_kernel,
        grid=(_steps(s, tm),),
        in_specs=[pl.BlockSpec((tm, dff), lambda i: (i, 0)),
                  pl.BlockSpec((dff, d), lambda i: (0, 0), pipeline_mode=pl.Buffered(1)),
                  pl.BlockSpec((tm, d), lambda i: (i, 0)),
                  pl.BlockSpec((1, d), lambda i: (0, 0))],
        out_specs=pl.BlockSpec((tm, d), lambda i: (i, 0)),
        out_shape=jax.ShapeDtypeStruct((s, d), F32),
        compiler_params=_params("parallel"),
        name="ffn_down",
    )(act, wd, h, g.reshape(1, d))


def _gate_row(fwd, bwd):
    row = jnp.zeros((1, LANES), F32)
    row = row.at[0, GATE_G_F:GATE_G_F + DN_HEADS].set(fwd.astype(F32))
    return row.at[0, GATE_G_B:GATE_G_B + DN_HEADS].set(bwd.astype(F32))


def _pick(n, *cands):
    for c in cands:
        if n % c == 0:
            return c
    return n


def kernel(x, mem, norm_mix_g, w_in, conv_w, a_log_f, a_log_b, dt_bias_f, dt_bias_b, dn_norm_g, attn_sink, rel_bias, w_out, norm_x_g, norm_mem_g, w_q_mem, w_kv_mem, w_o_mem, norm_ffn_g, w_gate, w_up, w_down, norm_final_g):
    batch, s, d = x.shape
    assert batch == 1 and mem.shape[0] == 1 and w_in.shape[0] == 1, "single sequence, single layer"
    w = WINDOW
    rel = (jnp.arange(3 * w)[None, :] - w) - jnp.arange(w)[:, None]
    bucket = _t5_bucket(rel).astype(jnp.int32)
    mem_dim = MEM_HEADS * HEAD_DIM

    h = x.reshape(s, d)
    mem2 = mem.reshape(mem.shape[1], d)
    tm_big = _pick(s, 1024, 512, 256, 128)
    tm_mid = _pick(s, 512, 256, 128)
    w_r = _w_in_layout(w_in[0].T, cols=256)
    proj = _rms_matmul(h, norm_mix_g[0], w_r, tm=tm_big, tn=_pick(PROJ_WIDTH, 1920, 1152, 640, 128), name="in_proj",
                       w_transposed=True)

    q, k, v, gates, gates_t = _dn_prep(proj, conv_w[0], _gate_row(a_log_f[0], a_log_b[0]),
                                       _gate_row(dt_bias_f[0], dt_bias_b[0]), rows=_pick(s, 256, 128))
    o_f, o_b, (wd16, wout16, wq16, wo16, wkv16) = _deltanet(
        q, k, v, gates, gates_t, [w_down[0], w_out[0], w_q_mem[0], w_o_mem[0], w_kv_mem[0]],
        rows=_pick(s, 512, 256, 128), heads=8)
    kv = _rms_matmul(mem2, norm_mem_g[0], wkv16, tm=mem2.shape[0], tn=2 * mem_dim, name="mem_kv").astype(BF16)
    h, f, (wg16, wu16) = _mix_mem(o_f, o_b, proj, bucket, rel_bias.astype(F32), attn_sink[0].astype(F32), dn_norm_g[0],
                                  wout16, h, norm_x_g[0], wq16, kv[:, :mem_dim], kv[:, mem_dim:], wo16, norm_ffn_g[0],
                                  [w_gate[0], w_up[0]], tm=_pick(s, 256))

    act = _ffn_glu(f, wg16, wu16, tm=tm_big, tn=_pick(wg16.shape[1], 512, 128))
    out = _ffn_down(act, wd16, h, norm_final_g, tm=tm_mid)
    return out.reshape(batch, s, d)
```

```python
import functools
import math

import jax
import jax.numpy as jnp
from jax import lax
from jax.experimental import pallas as pl
from jax.experimental.pallas import tpu as pltpu

F32 = jnp.float32
BF16 = jnp.bfloat16

RMS_EPS = 1e-6
L2_EPS = 1e-6
HEAD_DIM = 128
DN_HEADS = 8
DN_CHUNK = 64
DN_SUB = 16
CONV_WIDTH = 5
CONV_HALO = 8
SWA_HEADS = 8
SWA_KV_HEADS = 2
WINDOW = 128
NUM_BUCKETS = 32
MAX_DISTANCE = 128
MEM_HEADS = 4
LOG2E = 1.4426950408889634
LANES = 128
ROW_BATCH = 128

DN_QKV = 3 * DN_HEADS * HEAD_DIM
Z_OFF = DN_QKV
QSW_OFF = Z_OFF + DN_HEADS * HEAD_DIM
KSW_OFF = QSW_OFF + SWA_HEADS * HEAD_DIM
VSW_OFF = KSW_OFF + SWA_KV_HEADS * HEAD_DIM
GATE_OFF = VSW_OFF + SWA_KV_HEADS * HEAD_DIM
PROJ_WIDTH = GATE_OFF + LANES
GATE_BETA_F, GATE_BETA_B, GATE_G_F, GATE_G_B = 0, DN_HEADS, 2 * DN_HEADS, 3 * DN_HEADS
GATE_ROWS = 4 * DN_HEADS

VMEM_LIMIT_V7X = 56 * 1024 * 1024


def _params(*sem):
    return pltpu.CompilerParams(dimension_semantics=sem, vmem_limit_bytes=VMEM_LIMIT_V7X)


def _steps(dim, tile):
    assert dim % tile == 0, f"tile {tile} does not divide {dim}"
    return dim // tile


def _dot(a, b):
    return jnp.dot(a, b, preferred_element_type=F32)


def _dot_nt(a, b):
    return lax.dot_general(a, b, (((1,), (1,)), ((), ())), preferred_element_type=F32)


def _dot_tn(a, b):
    return lax.dot_general(a, b, (((0,), (0,)), ((), ())), preferred_element_type=F32)


def _rms(x, g):
    return x * lax.rsqrt(jnp.mean(x * x, axis=-1, keepdims=True) + RMS_EPS) * g


def _cast_specs(weights, nsteps, step_of):
    specs = [pl.BlockSpec((_steps(a.shape[0], nsteps), a.shape[1]), lambda *idx: (step_of(*idx), 0)) for a in weights]
    return specs, [jax.ShapeDtypeStruct(a.shape, BF16) for a in weights]


def _cast_slices(srcs, dsts):
    for src, dst in zip(srcs, dsts):
        dst[...] = src[...].astype(dst.dtype)


def _w_in_kernel(w_ref, o_ref):
    gate_lo = QSW_OFF
    o_ref[:gate_lo, :] = w_ref[:gate_lo, :].astype(o_ref.dtype)
    o_ref[gate_lo:GATE_OFF, :] = w_ref[gate_lo + GATE_ROWS:, :].astype(o_ref.dtype)
    o_ref[GATE_OFF:GATE_OFF + GATE_ROWS, :] = w_ref[gate_lo:gate_lo + GATE_ROWS, :].astype(o_ref.dtype)
    o_ref[GATE_OFF + GATE_ROWS:, :] = jnp.zeros((LANES - GATE_ROWS, o_ref.shape[1]), o_ref.dtype)


def _w_in_layout(w_t, *, cols):
    n, k = w_t.shape
    assert n == PROJ_WIDTH - (LANES - GATE_ROWS)
    return pl.pallas_call(
        _w_in_kernel,
        grid=(_steps(k, cols),),
        in_specs=[pl.BlockSpec((n, cols), lambda i: (0, i))],
        out_specs=pl.BlockSpec((PROJ_WIDTH, cols), lambda i: (0, i)),
        out_shape=jax.ShapeDtypeStruct((PROJ_WIDTH, k), BF16),
        compiler_params=_params("parallel"),
        name="w_in_layout",
    )(w_t)


def _rms_matmul_kernel(x_ref, g_ref, w_ref, o_ref, n_ref, *, w_transposed):
    j = pl.program_id(1)
    tn = o_ref.shape[1]
    if w_transposed:
        w = lambda: w_ref[pl.ds(pl.multiple_of(j * tn, tn), tn), :]
        dot = _dot_nt
    else:
        w = lambda: w_ref[...]
        dot = _dot

    @pl.when(j == 0)
    def _():
        half = max(x_ref.shape[0] // 2, ROW_BATCH)
        for r in range(0, x_ref.shape[0], half):
            rows = slice(r, r + half)
            n = _rms(x_ref[rows, :], g_ref[...]).astype(n_ref.dtype)
            n_ref[rows, :] = n
            o_ref[rows, :] = dot(n, w()).astype(o_ref.dtype)

    @pl.when(j != 0)
    def _():
        o_ref[...] = dot(n_ref[...], w()).astype(o_ref.dtype)


def _rms_matmul(x, g, w, *, tm, tn, name, w_transposed=False):
    m, k = x.shape
    n = w.shape[0] if w_transposed else w.shape[1]
    if w_transposed:
        w_spec = pl.BlockSpec((n, k), lambda i, j: (0, 0), pipeline_mode=pl.Buffered(1))
    else:
        w_spec = pl.BlockSpec((k, tn), lambda i, j: (0, j))
    return pl.pallas_call(
        functools.partial(_rms_matmul_kernel, w_transposed=w_transposed),
        grid=(_steps(m, tm), _steps(n, tn)),
        in_specs=[pl.BlockSpec((tm, k), lambda i, j: (i, 0)),
                  pl.BlockSpec((1, k), lambda i, j: (0, 0)),
                  w_spec],
        out_specs=pl.BlockSpec((tm, tn), lambda i, j: (i, j)),
        out_shape=jax.ShapeDtypeStruct((m, n), F32),
        scratch_shapes=[pltpu.VMEM((tm, k), BF16)],
        compiler_params=_params("parallel", "arbitrary"),
        name=name,
    )(x, g.reshape(1, k), w)


def _split3(x):
    hi = x.astype(BF16)
    r = x - hi.astype(F32)
    mid = r.astype(BF16)
    lo = (r - mid.astype(F32)).astype(BF16)
    return hi, mid, lo


def _prep_kernel(main_ref, prev_ref, next_ref, gate_ref, cw_ref, alog_ref, dt_ref,
                 q_ref, k_ref, v_ref, go_ref, gt_ref):
    i = pl.program_id(0)
    rows = main_ref.shape[0]
    halo = CONV_HALO
    pad = (CONV_WIDTH - 1) // 2

    first = i == 0
    last = i == pl.num_programs(0) - 1
    for s in range(3 * DN_HEADS):
        cols = slice(s * HEAD_DIM, (s + 1) * HEAD_DIM)
        xe = jnp.concatenate([jnp.where(first, 0.0, prev_ref[:, cols]), main_ref[:, cols],
                              jnp.where(last, 0.0, next_ref[:, cols])], axis=0)
        acc = cw_ref[pad:pad + 1, cols] * xe[halo:halo + rows]
        for j in range(CONV_WIDTH):
            if j != pad:
                shifted = pltpu.roll(xe, shift=(pad - j) % xe.shape[0], axis=0)[halo:halo + rows]
                acc = acc + cw_ref[j:j + 1, cols] * shifted
        y = acc * jax.nn.sigmoid(acc)
        if s < 2 * DN_HEADS:
            inv = lax.rsqrt(jnp.sum(y * y, axis=-1, keepdims=True) + L2_EPS)
            y = y * (inv * (HEAD_DIM ** -0.5) if s < DN_HEADS else inv)
        if s < DN_HEADS:
            q_ref[:, cols] = y
        elif s < 2 * DN_HEADS:
            k_ref[:, slice((s - DN_HEADS) * HEAD_DIM, (s - DN_HEADS + 1) * HEAD_DIM)] = y
        else:
            v_ref[:, slice((s - 2 * DN_HEADS) * HEAD_DIM, (s - 2 * DN_HEADS + 1) * HEAD_DIM)] = y

    t = gate_ref[...]
    beta = jax.nn.sigmoid(t)
    a = t + dt_ref[...]
    softplus = jnp.maximum(a, 0.0) + jnp.log1p(jnp.exp(-jnp.abs(a)))
    g = -jnp.exp(alog_ref[...]) * softplus

    ri = lax.broadcasted_iota(jnp.int32, (rows, rows), 0)
    ci = lax.broadcasted_iota(jnp.int32, (rows, rows), 1)
    shift = DN_CHUNK.bit_length() - 1
    same_chunk = (ri >> shift) == (ci >> shift)
    lower = jnp.where(same_chunk & (ci <= ri), 1.0, 0.0).astype(BF16)
    upper = jnp.where(same_chunk & (ci >= ri), 1.0, 0.0).astype(BF16)
    parts = _split3(g)
    gc_f = _dot(lower, parts[0]) + _dot(lower, parts[1]) + _dot(lower, parts[2])
    gc_b = _dot(upper, parts[0]) + _dot(upper, parts[1]) + _dot(upper, parts[2])

    col = lax.broadcasted_iota(jnp.int32, t.shape, 1)
    out = jnp.where(col < GATE_G_F, beta, jnp.where(col < GATE_G_B, gc_f, gc_b))
    go_ref[...] = out
    gt_ref[...] = out.T[0:GATE_ROWS, :]


def _dn_prep(proj, conv_w, alog_row, dt_row, *, rows):
    s = proj.shape[0]
    nblk = _steps(s, rows)
    hb = rows // CONV_HALO
    last_halo = s // CONV_HALO - 1
    head_cols = DN_HEADS * HEAD_DIM
    return pl.pallas_call(
        _prep_kernel,
        grid=(nblk,),
        in_specs=[pl.BlockSpec((rows, DN_QKV), lambda i: (i, 0)),
                  pl.BlockSpec((CONV_HALO, DN_QKV), lambda i: (jnp.maximum(i * hb - 1, 0), 0)),
                  pl.BlockSpec((CONV_HALO, DN_QKV), lambda i: (jnp.minimum((i + 1) * hb, last_halo), 0)),
                  pl.BlockSpec((rows, LANES), lambda i: (i, GATE_OFF // LANES)),
                  pl.BlockSpec((CONV_WIDTH, DN_QKV), lambda i: (0, 0)),
                  pl.BlockSpec((1, LANES), lambda i: (0, 0)),
                  pl.BlockSpec((1, LANES), lambda i: (0, 0))],
        out_specs=[pl.BlockSpec((rows, head_cols), lambda i: (i, 0)),
                   pl.BlockSpec((rows, head_cols), lambda i: (i, 0)),
                   pl.BlockSpec((rows, head_cols), lambda i: (i, 0)),
                   pl.BlockSpec((rows, LANES), lambda i: (i, 0)),
                   pl.BlockSpec((GATE_ROWS, rows), lambda i: (0, i))],
        out_shape=[jax.ShapeDtypeStruct((s, head_cols), F32),
                   jax.ShapeDtypeStruct((s, head_cols), F32),
                   jax.ShapeDtypeStruct((s, head_cols), F32),
                   jax.ShapeDtypeStruct((s, LANES), F32),
                   jax.ShapeDtypeStruct((GATE_ROWS, s), F32)],
        compiler_params=_params("parallel"),
        name="dn_prep",
    )(proj, proj, proj, proj, conv_w, alog_row, dt_row)


def _unit_tri_inverse(ms, eye, same_sub, _mm):
    c = eye.shape[0]
    mds = [jnp.where(same_sub, m, 0.0) for m in ms]
    es = [m - md for m, md in zip(ms, mds)]
    pw = [_mm(md, md) for md in mds]
    xs = [eye - md for md in mds]
    for _ in range(DN_SUB.bit_length() - 3):
        both = [_mm(jnp.concatenate([a, x], axis=0), a) for a, x in zip(pw, xs)]
        pw = [b[:c] for b in both]
        xs = [x + b[c:] for x, b in zip(xs, both)]
    xs = [x + _mm(x, a) for x, a in zip(xs, pw)]
    ns = [_mm(x, e) for x, e in zip(xs, es)]
    n2 = [_mm(n, n) for n in ns]
    ys = [_mm(n, (eye - n) + b) for n, b in zip(ns, n2)]
    return [x - _mm(y, x) for x, y in zip(xs, ys)]


def _deltanet_kernel(qf, kf, vf, gf, gtf, qb, kb, vb, gb, gtb, *rest):
    n_cast = (len(rest) - 3) // 2
    cast_in, (of_ref, ob_ref), cast_out = rest[:n_cast], rest[n_cast:n_cast + 2], rest[n_cast + 2:2 * n_cast + 2]
    state_ref = rest[-1]
    _cast_slices(cast_in, cast_out)
    c_len = DN_CHUNK
    nchunk = qf.shape[0] // c_len
    heads = qf.shape[1] // HEAD_DIM
    head0 = pl.program_id(0) * heads

    @pl.when(pl.program_id(1) == 0)
    def _():
        state_ref[...] = jnp.zeros_like(state_ref)

    p_len = 2 * c_len
    ri = lax.broadcasted_iota(jnp.int32, (c_len, p_len), 0)
    cl = lax.broadcasted_iota(jnp.int32, (c_len, p_len), 1)
    ci = cl & (c_len - 1)
    first = cl < c_len
    eye = jnp.where(ri == ci, 1.0, 0.0)
    sub_shift = DN_SUB.bit_length() - 1
    same_sub = (ri >> sub_shift) == (ci >> sub_shift)
    lane = lax.broadcasted_iota(jnp.int32, (p_len, LANES), 1)
    upper_rows = lax.broadcasted_iota(jnp.int32, (p_len, 1), 0) >= c_len
    first_tall = lax.broadcasted_iota(jnp.int32, (c_len + HEAD_DIM, p_len), 1) < c_len
    masks = {False: (ci <= ri, ci < ri), True: (ci >= ri, ci > ri)}
    refs = {False: (qf, kf, vf, gf, gtf, of_ref), True: (qb, kb, vb, gb, gtb, ob_ref)}

    def pair_mm(x, p):
        blockdiag = jnp.concatenate([jnp.where(first, p, 0.0), jnp.where(first, 0.0, p)], axis=0)
        return _dot(x.astype(BF16), blockdiag.astype(BF16))

    chains = [(rev, hd) for rev in (False, True) for hd in range(heads)]

    g_rows = {}
    for rev, hd in chains:
        g_col = head0 + hd + (GATE_G_B if rev else GATE_G_F)
        g_rows[rev, hd] = refs[rev][4][pl.ds(g_col, 1), :]

    def load(rev, hd, p):
        q_ref, k_ref, v_ref, g_ref = refs[rev][:4]
        rows = slice(p * p_len, (p + 1) * p_len)
        cols = slice(hd * HEAD_DIM, (hd + 1) * HEAD_DIM)
        gates = g_ref[rows, :]
        beta_col = head0 + hd + (GATE_BETA_B if rev else GATE_BETA_F)
        g_col = head0 + hd + (GATE_G_B if rev else GATE_G_F)
        beta = jnp.sum(jnp.where(lane == beta_col, gates, 0.0), axis=1, keepdims=True)
        gcol = jnp.sum(jnp.where(lane == g_col, gates, 0.0), axis=1, keepdims=True)
        grow = g_rows[rev, hd][:, rows]
        if rev:
            glast = (grow[:, 0:1], grow[:, c_len:c_len + 1])
        else:
            glast = (grow[:, c_len - 1:c_len], grow[:, p_len - 1:p_len])
        return q_ref[rows, cols], k_ref[rows, cols], v_ref[rows, cols], beta, gcol, grow, glast

    lo, hi = slice(0, c_len), slice(c_len, p_len)

    zeros = jnp.zeros((c_len, HEAD_DIM), BF16)

    def phase_a(inst):
        data = [load(*i) for i in inst]
        kbeta = [k * beta for (_, k, _, beta, _, _, _) in data]
        prod = []
        for kb_, (q, k, _, _, _, _, _) in zip(kbeta, data):
            lhs = jnp.concatenate([jnp.concatenate([kb_[lo], kb_[hi]], axis=1),
                                   jnp.concatenate([q[lo], q[hi]], axis=1)], axis=0).astype(BF16)
            kb16 = k.astype(BF16)
            rhs = jnp.concatenate([jnp.concatenate([kb16[lo], zeros], axis=1),
                                   jnp.concatenate([zeros, kb16[hi]], axis=1)], axis=0)
            prod.append(_dot_nt(lhs, rhs))
        decay = [jnp.exp(jnp.where(masks[rev][0], jnp.where(first, gcol[lo], gcol[hi]) - grow, -jnp.inf))
                 for (rev, _, _), (_, _, _, _, gcol, grow, _) in zip(inst, data)]
        ms = [jnp.where(masks[rev][1], p[lo] * dec, 0.0) for (rev, _, _), p, dec in zip(inst, prod, decay)]
        a_mat = [p[hi] * dec for p, dec in zip(prod, decay)]
        ts = _unit_tri_inverse(ms, eye, same_sub, pair_mm)
        egs = [jnp.exp(gcol) for (_, _, _, _, gcol, _, _) in data]
        uw = []
        for t, kb_, eg, (_, _, v, beta, _, _, _) in zip(ts, kbeta, egs, data):
            lhs = jnp.concatenate([jnp.where(first, t, 0.0), jnp.where(first, 0.0, t)], axis=0).astype(BF16)
            rhs = jnp.concatenate([(v * beta).astype(BF16), (kb_ * eg).astype(BF16)], axis=1)
            uw.append(_dot(lhs, rhs))
        qdec = [q * eg for eg, (q, _, _, _, _, _, _) in zip(egs, data)]
        kdec_t = [(k * jnp.exp(jnp.where(upper_rows, glast[1], glast[0]) - gcol)).T
                  for (_, k, _, _, gcol, _, glast) in data]
        tall = [jnp.concatenate([a_, kt], axis=0) for a_, kt in zip(a_mat, kdec_t)]
        pre = {}
        for (rev, hd, p), x, qd, tl, (_, _, _, _, _, _, glast) in zip(inst, uw, qdec, tall, data):
            for half, rs in enumerate((lo, hi)):
                wq = jnp.concatenate([x[rs, HEAD_DIM:], qd[rs]], axis=0).astype(BF16)
                keep = first_tall if half == 0 else ~first_tall
                pre[rev, hd, 2 * p + half] = (x[rs, :HEAD_DIM], wq, jnp.where(keep, tl, 0.0).astype(BF16),
                                              jnp.exp(glast[half]))
        return pre

    def phase_b(states, pre, t):
        chunk = {False: t, True: nchunk - 1 - t}
        cur = [pre[rev, hd, chunk[rev]] for rev, hd in chains]
        ws = [_dot(wq, s_.astype(BF16)) for (_, wq, _, _), s_ in zip(cur, states)]
        v_new = [(u - x[lo]).astype(BF16) for (u, _, _, _), x in zip(cur, ws)]
        upd = [_dot(tl, jnp.concatenate([vn, vn], axis=0)) for (_, _, tl, _), vn in zip(cur, v_new)]
        for (rev, hd), x, y in zip(chains, ws, upd):
            c = chunk[rev]
            refs[rev][5][c * c_len:(c + 1) * c_len, hd * HEAD_DIM:(hd + 1) * HEAD_DIM] = x[hi] + y[lo]
        return [s_ * g_ + x[c_len:] for (_, _, _, g_), s_, x in zip(cur, states, upd)]

    npair = nchunk // 2
    states = [state_ref[n] for n in range(len(chains))]
    for s in range(npair):
        pre = phase_a([(rev, hd, npair - 1 - s if rev else s) for rev, hd in chains])
        for t in (2 * s, 2 * s + 1):
            states = phase_b(states, pre, t)
    for n, s_ in enumerate(states):
        state_ref[n] = s_


def _deltanet(q, k, v, gates, gates_t, weights, *, rows, heads):
    s = q.shape[0]
    nb = _steps(s, rows)
    nh = _steps(DN_HEADS, heads)
    assert rows % (2 * DN_CHUNK) == 0
    fwd = lambda h, b: (b, h)
    bwd = lambda h, b: (nb - 1 - b, h)
    head = lambda im: pl.BlockSpec((rows, heads * HEAD_DIM), im)
    gate = lambda im: pl.BlockSpec((rows, LANES), lambda h, b: (im(h, b)[0], 0))
    gate_t = lambda im: pl.BlockSpec((GATE_ROWS, rows), lambda h, b: (0, im(h, b)[0]))
    cast_specs, cast_shapes = _cast_specs(weights, nh * nb, lambda h, b: h * nb + b)
    out = pl.pallas_call(
        _deltanet_kernel,
        grid=(nh, nb),
        in_specs=[head(fwd), head(fwd), head(fwd), gate(fwd), gate_t(fwd),
                  head(bwd), head(bwd), head(bwd), gate(bwd), gate_t(bwd)] + cast_specs,
        out_specs=[head(fwd), head(bwd)] + cast_specs,
        out_shape=[jax.ShapeDtypeStruct(q.shape, F32), jax.ShapeDtypeStruct(q.shape, F32)] + cast_shapes,
        scratch_shapes=[pltpu.VMEM((2 * heads, HEAD_DIM, HEAD_DIM), F32)],
        compiler_params=_params("parallel", "arbitrary"),
        name="deltanet",
    )(q, k, v, gates, gates_t, q, k, v, gates, gates_t, *weights)
    return out[0], out[1], out[2:]


def _t5_bucket(rel):
    nb = NUM_BUCKETS // 2
    max_exact = nb // 2
    n = jnp.abs(rel)
    large = max_exact + (jnp.log(jnp.maximum(n, max_exact).astype(F32) / max_exact)
                         / math.log(MAX_DISTANCE / max_exact) * (nb - max_exact)).astype(jnp.int32)
    large = jnp.minimum(large, nb - 1)
    return jnp.where(rel > 0, nb, 0) + jnp.where(n < max_exact, n, large)


def _swa_bias_init(bucket_ref, rb_ref, bias_ref):
    w = WINDOW
    ri = lax.broadcasted_iota(jnp.int32, (w, 3 * w), 0)
    ci = lax.broadcasted_iota(jnp.int32, (w, 3 * w), 1)
    bucket = bucket_ref[...]
    in_band = jnp.abs(ci - w - ri) <= w
    for hd in range(SWA_HEADS):
        acc = jnp.zeros((w, 3 * w), F32)
        for b in range(NUM_BUCKETS):
            acc = jnp.where(bucket == b, rb_ref[b, hd], acc)
        bias_ref[hd] = jnp.where(in_band, acc * LOG2E, -jnp.inf)


def _swa_block(t, qblk, first_row, seq, q_ref, kext, vext, bias_ref, sinks):
    w = WINDOW
    group = SWA_HEADS // SWA_KV_HEADS
    heads = range(SWA_HEADS)
    rows = slice(t * w, (t + 1) * w)
    band = slice(t * w, (t + 3) * w)
    s = [_dot_nt(q_ref[rows, hd * HEAD_DIM:(hd + 1) * HEAD_DIM].astype(BF16), kext[hd // group][band])
         * (HEAD_DIM ** -0.5 * LOG2E) + bias_ref[hd] for hd in heads]
    if t == 0 or t == qblk - 1:
        key_pos = first_row + (t - 1) * w + lax.broadcasted_iota(jnp.int32, (w, 3 * w), 1)
        if t == 0:
            s = [jnp.where(key_pos >= 0, x, -jnp.inf) for x in s]
        if t == qblk - 1:
            s = [jnp.where(key_pos < seq, x, -jnp.inf) for x in s]
    mx = [jnp.maximum(jnp.max(x, axis=1, keepdims=True), sk) for x, sk in zip(s, sinks)]
    p = [jnp.exp2(x - m) for x, m in zip(s, mx)]
    den = [jnp.sum(x, axis=1, keepdims=True) + jnp.exp2(sk - m) for x, sk, m in zip(p, sinks, mx)]
    o = [_dot(x.astype(BF16), vext[hd // group][band]) / dn for hd, x, dn in zip(heads, p, den)]
    return jnp.concatenate([x.astype(BF16) for x in o], axis=1)


def _mix_mem_kernel(of_ref, ob_ref, z_ref, qs_ref, kp_ref, kc_ref, kn_ref, vp_ref, vc_ref, vn_ref, bucket_ref, rb_ref,
                    sink_ref, dng_ref, wout_ref, x_ref, gx_ref, wq_ref, k_ref, v_ref, wo_ref, gf_ref, *rest, seq):
    n_cast = (len(rest) - 3) // 2
    cast_in, (h2_ref, f_ref), cast_out = rest[:n_cast], rest[n_cast:n_cast + 2], rest[n_cast + 2:2 * n_cast + 2]
    bias_ref = rest[-1]
    _cast_slices(cast_in, cast_out)
    n = pl.program_id(0)
    rows_per_step = h2_ref.shape[0]
    qblk = rows_per_step // WINDOW

    @pl.when(n == 0)
    def _():
        _swa_bias_init(bucket_ref, rb_ref, bias_ref)

    kext, vext = [], []
    for kvh in range(SWA_KV_HEADS):
        kcols = slice(kvh * HEAD_DIM, (kvh + 1) * HEAD_DIM)
        kext.append(jnp.concatenate([kp_ref[:, kcols], kc_ref[:, kcols], kn_ref[:, kcols]], axis=0).astype(BF16))
        vext.append(jnp.concatenate([vp_ref[:, kcols], vc_ref[:, kcols], vn_ref[:, kcols]], axis=0).astype(BF16))
    sinks = [sink_ref[hd] * LOG2E for hd in range(SWA_HEADS)]
    for t0 in range(0, qblk, 2):
        blocks = range(t0, min(t0 + 2, qblk))
        y_sw = [_swa_block(t, qblk, n * rows_per_step, seq, qs_ref, kext, vext, bias_ref, sinks) for t in blocks]
        _mix_mem_rows([slice(t * WINDOW, (t + 1) * WINDOW) for t in blocks], y_sw, HEAD_DIM ** -0.5, of_ref, ob_ref,
                      z_ref, dng_ref, wout_ref, x_ref, gx_ref, wq_ref, k_ref, v_ref, wo_ref, gf_ref, h2_ref, f_ref)


def _mix_mem_rows(batches, y_sw, scale, of_ref, ob_ref, z_ref, dng_ref, wout_ref, x_ref, gx_ref, wq_ref, k_ref, v_ref,
                  wo_ref, gf_ref, h2_ref, f_ref):
    dn = DN_HEADS * HEAD_DIM
    y_dn = []
    for rows in batches:
        parts = []
        for hd in range(DN_HEADS):
            cols = slice(hd * HEAD_DIM, (hd + 1) * HEAD_DIM)
            z = z_ref[rows, cols]
            y = _rms(of_ref[rows, cols] + ob_ref[rows, cols], dng_ref[...]) * (z * jax.nn.sigmoid(z))
            parts.append(y.astype(BF16))
        y_dn.append(jnp.concatenate(parts, axis=1))
    h1 = [x_ref[rows, :] + _dot(a, wout_ref[:dn, :]) for rows, a in zip(batches, y_dn)]
    h1 = [h + _dot(sw, wout_ref[dn:, :]) for h, sw in zip(h1, y_sw)]
    q = [_dot(_rms(h, gx_ref[...]).astype(BF16), wq_ref[...]) for h in h1]
    att = []
    for qb in q:
        heads = []
        for hd in range(MEM_HEADS):
            cols = slice(hd * HEAD_DIM, (hd + 1) * HEAD_DIM)
            s = _dot_nt(qb[:, cols].astype(BF16), k_ref[:, cols]) * scale
            p = jnp.exp(s - jnp.max(s, axis=1, keepdims=True))
            den = jnp.sum(p, axis=1, keepdims=True)
            heads.append((_dot(p.astype(BF16), v_ref[:, cols]) / den).astype(BF16))
        att.append(jnp.concatenate(heads, axis=1))
    h2 = [h + _dot(a, wo_ref[...]) for h, a in zip(h1, att)]
    for rows, h in zip(batches, h2):
        h2_ref[rows, :] = h
        f_ref[rows, :] = _rms(h, gf_ref[...]).astype(f_ref.dtype)


def _mix_mem(o_f, o_b, proj, bucket, rel_bias, sink, dn_g, w_out, x, gx, wq, k, v, wo, gf, weights, *, tm):
    s, d = x.shape
    w = WINDOW
    dn = DN_HEADS * HEAD_DIM
    qw = SWA_HEADS * HEAD_DIM
    kvw = SWA_KV_HEADS * HEAD_DIM
    qblk = _steps(tm, w)
    nb = _steps(s, w)
    row = lambda width, col=0: pl.BlockSpec((tm, width), lambda i: (i, col))
    halo = lambda off, lo: pl.BlockSpec(
        (w, kvw), lambda i: (jnp.clip(i * qblk - 1 if lo else (i + 1) * qblk, 0, nb - 1), off // kvw))
    const = lambda a: pl.BlockSpec(a.shape, lambda i: (0, 0), pipeline_mode=pl.Buffered(1))
    smem = pl.BlockSpec(memory_space=pltpu.SMEM)
    dn_g, gx, gf = dn_g.reshape(1, HEAD_DIM), gx.reshape(1, d), gf.reshape(1, d)
    nsteps = _steps(s, tm)
    cast_specs, cast_shapes = _cast_specs(weights, nsteps, lambda i: i)
    out = pl.pallas_call(
        functools.partial(_mix_mem_kernel, seq=s),
        grid=(nsteps,),
        in_specs=[row(dn), row(dn), row(dn, Z_OFF // dn), row(qw, QSW_OFF // qw),
                  halo(KSW_OFF, True), row(kvw, KSW_OFF // kvw), halo(KSW_OFF, False),
                  halo(VSW_OFF, True), row(kvw, VSW_OFF // kvw), halo(VSW_OFF, False),
                  const(bucket), smem, smem, const(dn_g), const(w_out), row(d),
                  const(gx), const(wq), const(k), const(v), const(wo), const(gf)] + cast_specs,
        out_specs=[row(d), row(d)] + cast_specs,
        out_shape=[jax.ShapeDtypeStruct((s, d), F32), jax.ShapeDtypeStruct((s, d), BF16)] + cast_shapes,
        scratch_shapes=[pltpu.VMEM((SWA_HEADS, w, 3 * w), F32)],
        compiler_params=_params("arbitrary"),
        name="mix_mem",
    )(o_f, o_b, proj, proj, proj, proj, proj, proj, proj, proj, bucket, rel_bias, sink, dn_g, w_out, x,
      gx, wq, k, v, wo, gf, *weights)
    return out[0], out[1], out[2:]


def _glu_kernel(f_ref, wg_ref, wu_ref, o_ref):
    f = f_ref[...]
    a = _dot(f, wg_ref[...])
    o_ref[...] = (a * jax.nn.sigmoid(a) * _dot(f, wu_ref[...])).astype(o_ref.dtype)


def _ffn_glu(f, wg, wu, *, tm, tn):
    s, d = f.shape
    dff = wg.shape[1]
    return pl.pallas_call(
        _glu_kernel,
        grid=(_steps(s, tm), _steps(dff, tn)),
        in_specs=[pl.BlockSpec((tm, d), lambda i, j: (i, 0)),
                  pl.BlockSpec((d, tn), lambda i, j: (0, j)),
                  pl.BlockSpec((d, tn), lambda i, j: (0, j))],
        out_specs=pl.BlockSpec((tm, tn), lambda i, j: (i, j)),
        out_shape=jax.ShapeDtypeStruct((s, dff), BF16),
        compiler_params=_params("parallel", "parallel"),
        name="ffn_glu",
    )(f, wg, wu)


def _down_kernel(a_ref, w_ref, h_ref, g_ref, o_ref):
    half = o_ref.shape[0] // 2
    for r in (0, half):
        rows = slice(r, r + half)
        o_ref[rows, :] = _rms(h_ref[rows, :] + _dot(a_ref[rows, :], w_ref[...]), g_ref[...])


def _ffn_down(act, wd, h, g, *, tm):
    s, dff = act.shape
    d = wd.shape[1]
    return pl.pallas_call(
        _down_kernel,
        grid=(_steps(s, tm),),
        in_specs=[pl.BlockSpec((tm, dff), lambda i: (i, 0)),
                  pl.BlockSpec((dff, d), lambda i: (0, 0), pipeline_mode=pl.Buffered(1)),
                  pl.BlockSpec((tm, d), lambda i: (i, 0)),
                  pl.BlockSpec((1, d), lambda i: (0, 0))],
        out_specs=pl.BlockSpec((tm, d), lambda i: (i, 0)),
        out_shape=jax.ShapeDtypeStruct((s, d), F32),
        compiler_params=_params("parallel"),
        name="ffn_down",
    )(act, wd, h, g.reshape(1, d))


def _gate_row(fwd, bwd):
    row = jnp.zeros((1, LANES), F32)
    row = row.at[0, GATE_G_F:GATE_G_F + DN_HEADS].set(fwd.astype(F32))
    return row.at[0, GATE_G_B:GATE_G_B + DN_HEADS].set(bwd.astype(F32))


def _pick(n, *cands):
    for c in cands:
        if n % c == 0:
            return c
    return n


def kernel(x, mem, norm_mix_g, w_in, conv_w, a_log_f, a_log_b, dt_bias_f, dt_bias_b, dn_norm_g, attn_sink, rel_bias, w_out, norm_x_g, norm_mem_g, w_q_mem, w_kv_mem, w_o_mem, norm_ffn_g, w_gate, w_up, w_down, norm_final_g):
    batch, s, d = x.shape
    assert batch == 1 and mem.shape[0] == 1 and w_in.shape[0] == 1, "single sequence, single layer"
    w = WINDOW
    rel = (jnp.arange(3 * w)[None, :] - w) - jnp.arange(w)[:, None]
    bucket = _t5_bucket(rel).astype(jnp.int32)
    mem_dim = MEM_HEADS * HEAD_DIM

    h = x.reshape(s, d)
    mem2 = mem.reshape(mem.shape[1], d)
    tm_big = _pick(s, 1024, 512, 256, 128)
    tm_mid = _pick(s, 512, 256, 128)
    w_r = _w_in_layout(w_in[0].T, cols=256)
    proj = _rms_matmul(h, norm_mix_g[0], w_r, tm=tm_mid, tn=_pick(PROJ_WIDTH, 1920, 1152, 640, 128), name="in_proj",
                       w_transposed=True)

    q, k, v, gates, gates_t = _dn_prep(proj, conv_w[0], _gate_row(a_log_f[0], a_log_b[0]),
                                       _gate_row(dt_bias_f[0], dt_bias_b[0]), rows=_pick(s, 256, 128))
    o_f, o_b, (wd16, wout16, wq16, wo16, wkv16) = _deltanet(
        q, k, v, gates, gates_t, [w_down[0], w_out[0], w_q_mem[0], w_o_mem[0], w_kv_mem[0]],
        rows=_pick(s, 512, 256, 128), heads=8)
    kv = _rms_matmul(mem2, norm_mem_g[0], wkv16, tm=mem2.shape[0], tn=2 * mem_dim, name="mem_kv").astype(BF16)
    h, f, (wg16, wu16) = _mix_mem(o_f, o_b, proj, bucket, rel_bias.astype(F32), attn_sink[0].astype(F32), dn_norm_g[0],
                                  wout16, h, norm_x_g[0], wq16, kv[:, :mem_dim], kv[:, mem_dim:], wo16, norm_ffn_g[0],
                                  [w_gate[0], w_up[0]], tm=_pick(s, 256))

    act = _ffn_glu(f, wg16, wu16, tm=tm_big, tn=_pick(wg16.shape[1], 512, 128))
    out = _ffn_down(act, wd16, h, norm_final_g, tm=tm_mid)
    return out.reshape(batch, s, d)
```

```python
import functools
import math

import jax
import jax.numpy as jnp
from jax import lax
from jax.experimental import pallas as pl
from jax.experimental.pallas import tpu as pltpu

F32 = jnp.float32
BF16 = jnp.bfloat16

RMS_EPS = 1e-6
L2_EPS = 1e-6
HEAD_DIM = 128
DN_HEADS = 8
DN_CHUNK = 64
DN_SUB = 16
CONV_WIDTH = 5
CONV_HALO = 8
SWA_HEADS = 8
SWA_KV_HEADS = 2
WINDOW = 128
NUM_BUCKETS = 32
MAX_DISTANCE = 128
MEM_HEADS = 4
LOG2E = 1.4426950408889634
LANES = 128
ROW_BATCH = 128

DN_QKV = 3 * DN_HEADS * HEAD_DIM
Z_OFF = DN_QKV
QSW_OFF = Z_OFF + DN_HEADS * HEAD_DIM
KSW_OFF = QSW_OFF + SWA_HEADS * HEAD_DIM
VSW_OFF = KSW_OFF + SWA_KV_HEADS * HEAD_DIM
GATE_OFF = VSW_OFF + SWA_KV_HEADS * HEAD_DIM
PROJ_WIDTH = GATE_OFF + LANES
GATE_BETA_F, GATE_BETA_B, GATE_G_F, GATE_G_B = 0, DN_HEADS, 2 * DN_HEADS, 3 * DN_HEADS
GATE_ROWS = 4 * DN_HEADS

VMEM_LIMIT_V7X = 56 * 1024 * 1024


def _params(*sem):
    return pltpu.CompilerParams(dimension_semantics=sem, vmem_limit_bytes=VMEM_LIMIT_V7X)


def _steps(dim, tile):
    assert dim % tile == 0, f"tile {tile} does not divide {dim}"
    return dim // tile


def _dot(a, b):
    return jnp.dot(a, b, preferred_element_type=F32)


def _dot_nt(a, b):
    return lax.dot_general(a, b, (((1,), (1,)), ((), ())), preferred_element_type=F32)


def _rms(x, g):
    return x * lax.rsqrt(jnp.mean(x * x, axis=-1, keepdims=True) + RMS_EPS) * g


def _cast_specs(weights, nsteps, step_of):
    specs = [pl.BlockSpec((_steps(a.shape[0], nsteps), a.shape[1]), lambda *idx: (step_of(*idx), 0)) for a in weights]
    return specs, [jax.ShapeDtypeStruct(a.shape, BF16) for a in weights]


def _cast_slices(srcs, dsts):
    for src, dst in zip(srcs, dsts):
        dst[...] = src[...].astype(dst.dtype)


def _w_in_kernel(w_ref, o_ref):
    gate_lo = QSW_OFF
    o_ref[:gate_lo, :] = w_ref[:gate_lo, :].astype(o_ref.dtype)
    o_ref[gate_lo:GATE_OFF, :] = w_ref[gate_lo + GATE_ROWS:, :].astype(o_ref.dtype)
    o_ref[GATE_OFF:GATE_OFF + GATE_ROWS, :] = w_ref[gate_lo:gate_lo + GATE_ROWS, :].astype(o_ref.dtype)
    o_ref[GATE_OFF + GATE_ROWS:, :] = jnp.zeros((LANES - GATE_ROWS, o_ref.shape[1]), o_ref.dtype)


def _w_in_layout(w_t, *, cols):
    n, k = w_t.shape
    assert n == PROJ_WIDTH - (LANES - GATE_ROWS)
    return pl.pallas_call(
        _w_in_kernel,
        grid=(_steps(k, cols),),
        in_specs=[pl.BlockSpec((n, cols), lambda i: (0, i))],
        out_specs=pl.BlockSpec((PROJ_WIDTH, cols), lambda i: (0, i)),
        out_shape=jax.ShapeDtypeStruct((PROJ_WIDTH, k), BF16),
        compiler_params=_params("parallel"),
        name="w_in_layout",
    )(w_t)


def _rms_matmul_kernel(x_ref, g_ref, w_ref, o_ref, n_ref, *, w_transposed):
    dot = _dot_nt if w_transposed else _dot

    @pl.when(pl.program_id(1) == 0)
    def _():
        half = max(x_ref.shape[0] // 2, ROW_BATCH)
        for r in range(0, x_ref.shape[0], half):
            rows = slice(r, r + half)
            n = _rms(x_ref[rows, :], g_ref[...]).astype(n_ref.dtype)
            n_ref[rows, :] = n
            o_ref[rows, :] = dot(n, w_ref[...]).astype(o_ref.dtype)

    @pl.when(pl.program_id(1) != 0)
    def _():
        o_ref[...] = dot(n_ref[...], w_ref[...]).astype(o_ref.dtype)


def _rms_matmul(x, g, w, *, tm, tn, name, w_transposed=False, out_dtype=F32):
    m, k = x.shape
    n = w.shape[0] if w_transposed else w.shape[1]
    w_spec = pl.BlockSpec((tn, k), lambda i, j: (j, 0)) if w_transposed else pl.BlockSpec((k, tn), lambda i, j: (0, j))
    return pl.pallas_call(
        functools.partial(_rms_matmul_kernel, w_transposed=w_transposed),
        grid=(_steps(m, tm), _steps(n, tn)),
        in_specs=[pl.BlockSpec((tm, k), lambda i, j: (i, 0)),
                  pl.BlockSpec((1, k), lambda i, j: (0, 0)),
                  w_spec],
        out_specs=pl.BlockSpec((tm, tn), lambda i, j: (i, j)),
        out_shape=jax.ShapeDtypeStruct((m, n), out_dtype),
        scratch_shapes=[pltpu.VMEM((tm, k), BF16)],
        compiler_params=_params("parallel", "arbitrary"),
        name=name,
    )(x, g.reshape(1, k), w)


def _split3(x):
    hi = x.astype(BF16)
    r = x - hi.astype(F32)
    mid = r.astype(BF16)
    lo = (r - mid.astype(F32)).astype(BF16)
    return hi, mid, lo


def _prep_kernel(main_ref, prev_ref, next_ref, gate_ref, cw_ref, gp_ref,
                 q_ref, k_ref, v_ref, go_ref, gt_ref):
    i = pl.program_id(0)
    rows = main_ref.shape[0]
    halo = CONV_HALO
    pad = (CONV_WIDTH - 1) // 2

    first = i == 0
    last = i == pl.num_programs(0) - 1
    for s in range(3 * DN_HEADS):
        cols = slice(s * HEAD_DIM, (s + 1) * HEAD_DIM)
        xe = jnp.concatenate([jnp.where(first, 0.0, prev_ref[:, cols]), main_ref[:, cols],
                              jnp.where(last, 0.0, next_ref[:, cols])], axis=0)
        acc = cw_ref[pad:pad + 1, cols] * xe[halo:halo + rows]
        for j in range(CONV_WIDTH):
            if j != pad:
                shifted = pltpu.roll(xe, shift=(pad - j) % xe.shape[0], axis=0)[halo:halo + rows]
                acc = acc + cw_ref[j:j + 1, cols] * shifted
        y = acc * jax.nn.sigmoid(acc)
        if s < 2 * DN_HEADS:
            inv = lax.rsqrt(jnp.sum(y * y, axis=-1, keepdims=True) + L2_EPS)
            y = y * (inv * (HEAD_DIM ** -0.5) if s < DN_HEADS else inv)
        if s < DN_HEADS:
            q_ref[:, cols] = y
        elif s < 2 * DN_HEADS:
            k_ref[:, slice((s - DN_HEADS) * HEAD_DIM, (s - DN_HEADS + 1) * HEAD_DIM)] = y
        else:
            v_ref[:, slice((s - 2 * DN_HEADS) * HEAD_DIM, (s - 2 * DN_HEADS + 1) * HEAD_DIM)] = y

    t = gate_ref[...]
    beta = jax.nn.sigmoid(t)
    a = t + gp_ref[1:2, :]
    softplus = jnp.maximum(a, 0.0) + jnp.log1p(jnp.exp(-jnp.abs(a)))
    g = -jnp.exp(gp_ref[0:1, :]) * softplus

    ri = lax.broadcasted_iota(jnp.int32, (rows, rows), 0)
    ci = lax.broadcasted_iota(jnp.int32, (rows, rows), 1)
    shift = DN_CHUNK.bit_length() - 1
    same_chunk = (ri >> shift) == (ci >> shift)
    lower = jnp.where(same_chunk & (ci <= ri), 1.0, 0.0).astype(BF16)
    upper = jnp.where(same_chunk & (ci >= ri), 1.0, 0.0).astype(BF16)
    parts = _split3(g)
    gc_f = _dot(lower, parts[0]) + _dot(lower, parts[1]) + _dot(lower, parts[2])
    gc_b = _dot(upper, parts[0]) + _dot(upper, parts[1]) + _dot(upper, parts[2])

    col = lax.broadcasted_iota(jnp.int32, t.shape, 1)
    out = jnp.where(col < GATE_G_F, beta, jnp.where(col < GATE_G_B, gc_f, gc_b))
    go_ref[...] = out
    gt_ref[...] = out.T[0:GATE_ROWS, :]


def _dn_prep(proj, conv_w, gate_params, *, rows):
    s = proj.shape[0]
    nblk = _steps(s, rows)
    hb = rows // CONV_HALO
    last_halo = s // CONV_HALO - 1
    head_cols = DN_HEADS * HEAD_DIM
    return pl.pallas_call(
        _prep_kernel,
        grid=(nblk,),
        in_specs=[pl.BlockSpec((rows, DN_QKV), lambda i: (i, 0)),
                  pl.BlockSpec((CONV_HALO, DN_QKV), lambda i: (jnp.maximum(i * hb - 1, 0), 0)),
                  pl.BlockSpec((CONV_HALO, DN_QKV), lambda i: (jnp.minimum((i + 1) * hb, last_halo), 0)),
                  pl.BlockSpec((rows, LANES), lambda i: (i, GATE_OFF // LANES)),
                  pl.BlockSpec((CONV_WIDTH, DN_QKV), lambda i: (0, 0)),
                  pl.BlockSpec((2, LANES), lambda i: (0, 0))],
        out_specs=[pl.BlockSpec((rows, head_cols), lambda i: (i, 0)),
                   pl.BlockSpec((rows, head_cols), lambda i: (i, 0)),
                   pl.BlockSpec((rows, head_cols), lambda i: (i, 0)),
                   pl.BlockSpec((rows, LANES), lambda i: (i, 0)),
                   pl.BlockSpec((GATE_ROWS, rows), lambda i: (0, i))],
        out_shape=[jax.ShapeDtypeStruct((s, head_cols), F32),
                   jax.ShapeDtypeStruct((s, head_cols), F32),
                   jax.ShapeDtypeStruct((s, head_cols), F32),
                   jax.ShapeDtypeStruct((s, LANES), F32),
                   jax.ShapeDtypeStruct((GATE_ROWS, s), F32)],
        compiler_params=_params("parallel"),
        name="dn_prep",
    )(proj, proj, proj, proj, conv_w, gate_params)


def _unit_tri_inverse(ms, eye, same_sub, _mm):
    c = eye.shape[0]
    assert c <= 4 * DN_SUB
    mds = [jnp.where(same_sub, m, 0.0) for m in ms]
    es = [m - md for m, md in zip(ms, mds)]
    pw = [_mm(md, md) for md in mds]
    xs = [eye - md for md in mds]
    for _ in range(DN_SUB.bit_length() - 3):
        both = [_mm(jnp.concatenate([a, x], axis=0), a) for a, x in zip(pw, xs)]
        pw = [b[:c] for b in both]
        xs = [x + b[c:] for x, b in zip(xs, both)]
    xs = [x + _mm(x, a) for x, a in zip(xs, pw)]
    ns = [_mm(x, e) for x, e in zip(xs, es)]
    n2 = [_mm(n, n) for n in ns]
    ys = [_mm(n, (eye - n) + b) for n, b in zip(ns, n2)]
    return [x - _mm(y, x) for x, y in zip(xs, ys)]


def _deltanet_kernel(qf, kf, vf, gf, gtf, qb, kb, vb, gb, gtb, *rest):
    n_cast = (len(rest) - 3) // 2
    cast_in, (of_ref, ob_ref), cast_out = rest[:n_cast], rest[n_cast:n_cast + 2], rest[n_cast + 2:2 * n_cast + 2]
    state_ref = rest[-1]
    _cast_slices(cast_in, cast_out)
    c_len = DN_CHUNK
    nchunk = qf.shape[0] // c_len
    heads = qf.shape[1] // HEAD_DIM
    head0 = pl.program_id(0) * heads

    @pl.when(pl.program_id(1) == 0)
    def _():
        state_ref[...] = jnp.zeros_like(state_ref)

    p_len = 2 * c_len
    ri = lax.broadcasted_iota(jnp.int32, (c_len, p_len), 0)
    cl = lax.broadcasted_iota(jnp.int32, (c_len, p_len), 1)
    ci = cl & (c_len - 1)
    first = cl < c_len
    eye = jnp.where(ri == ci, 1.0, 0.0)
    sub_shift = DN_SUB.bit_length() - 1
    same_sub = (ri >> sub_shift) == (ci >> sub_shift)
    lane = lax.broadcasted_iota(jnp.int32, (p_len, LANES), 1)
    upper_rows = lax.broadcasted_iota(jnp.int32, (p_len, 1), 0) >= c_len
    first_tall = lax.broadcasted_iota(jnp.int32, (c_len + HEAD_DIM, p_len), 1) < c_len
    masks = {False: (ci <= ri, ci < ri), True: (ci >= ri, ci > ri)}
    refs = {False: (qf, kf, vf, gf, gtf, of_ref), True: (qb, kb, vb, gb, gtb, ob_ref)}

    def pair_mm(x, p):
        blockdiag = jnp.concatenate([jnp.where(first, p, 0.0), jnp.where(first, 0.0, p)], axis=0)
        return _dot(x.astype(BF16), blockdiag.astype(BF16))

    chains = [(rev, hd) for rev in (False, True) for hd in range(heads)]

    g_rows = {}
    for rev, hd in chains:
        g_col = head0 + hd + (GATE_G_B if rev else GATE_G_F)
        g_rows[rev, hd] = refs[rev][4][pl.ds(g_col, 1), :]

    def load(rev, hd, p):
        q_ref, k_ref, v_ref, g_ref = refs[rev][:4]
        rows = slice(p * p_len, (p + 1) * p_len)
        cols = slice(hd * HEAD_DIM, (hd + 1) * HEAD_DIM)
        gates = g_ref[rows, :]
        beta_col = head0 + hd + (GATE_BETA_B if rev else GATE_BETA_F)
        g_col = head0 + hd + (GATE_G_B if rev else GATE_G_F)
        beta = jnp.sum(jnp.where(lane == beta_col, gates, 0.0), axis=1, keepdims=True)
        gcol = jnp.sum(jnp.where(lane == g_col, gates, 0.0), axis=1, keepdims=True)
        grow = g_rows[rev, hd][:, rows]
        if rev:
            glast = (grow[:, 0:1], grow[:, c_len:c_len + 1])
        else:
            glast = (grow[:, c_len - 1:c_len], grow[:, p_len - 1:p_len])
        return q_ref[rows, cols], k_ref[rows, cols], v_ref[rows, cols], beta, gcol, grow, glast

    lo, hi = slice(0, c_len), slice(c_len, p_len)

    zeros = jnp.zeros((c_len, HEAD_DIM), BF16)

    def phase_a(inst):
        data = [load(*i) for i in inst]
        kbeta = [k * beta for (_, k, _, beta, _, _, _) in data]
        prod = []
        for kb_, (q, k, _, _, _, _, _) in zip(kbeta, data):
            lhs = jnp.concatenate([jnp.concatenate([kb_[lo], kb_[hi]], axis=1),
                                   jnp.concatenate([q[lo], q[hi]], axis=1)], axis=0).astype(BF16)
            kb16 = k.astype(BF16)
            rhs = jnp.concatenate([jnp.concatenate([kb16[lo], zeros], axis=1),
                                   jnp.concatenate([zeros, kb16[hi]], axis=1)], axis=0)
            prod.append(_dot_nt(lhs, rhs))
        decay = [jnp.exp(jnp.where(masks[rev][0], jnp.where(first, gcol[lo], gcol[hi]) - grow, -jnp.inf))
                 for (rev, _, _), (_, _, _, _, gcol, grow, _) in zip(inst, data)]
        ms = [jnp.where(masks[rev][1], p[lo] * dec, 0.0) for (rev, _, _), p, dec in zip(inst, prod, decay)]
        a_mat = [p[hi] * dec for p, dec in zip(prod, decay)]
        ts = _unit_tri_inverse(ms, eye, same_sub, pair_mm)
        egs = [jnp.exp(gcol) for (_, _, _, _, gcol, _, _) in data]
        uw = []
        for t, kb_, eg, (_, _, v, beta, _, _, _) in zip(ts, kbeta, egs, data):
            lhs = jnp.concatenate([jnp.where(first, t, 0.0), jnp.where(first, 0.0, t)], axis=0).astype(BF16)
            rhs = jnp.concatenate([(v * beta).astype(BF16), (kb_ * eg).astype(BF16)], axis=1)
            uw.append(_dot(lhs, rhs))
        qdec = [q * eg for eg, (q, _, _, _, _, _, _) in zip(egs, data)]
        kdec_t = [(k * jnp.exp(jnp.where(upper_rows, glast[1], glast[0]) - gcol)).T
                  for (_, k, _, _, gcol, _, glast) in data]
        tall = [jnp.concatenate([a_, kt], axis=0) for a_, kt in zip(a_mat, kdec_t)]
        pre = {}
        for (rev, hd, p), x, qd, tl, (_, _, _, _, _, _, glast) in zip(inst, uw, qdec, tall, data):
            for half, rs in enumerate((lo, hi)):
                wq = jnp.concatenate([x[rs, HEAD_DIM:], qd[rs]], axis=0).astype(BF16)
                keep = first_tall if half == 0 else ~first_tall
                pre[rev, hd, 2 * p + half] = (x[rs, :HEAD_DIM], wq, jnp.where(keep, tl, 0.0).astype(BF16),
                                              jnp.exp(glast[half]))
        return pre

    def phase_b(states, pre, t):
        chunk = {False: t, True: nchunk - 1 - t}
        cur = [pre[rev, hd, chunk[rev]] for rev, hd in chains]
        ws = [_dot(wq, s_.astype(BF16)) for (_, wq, _, _), s_ in zip(cur, states)]
        v_new = [(u - x[lo]).astype(BF16) for (u, _, _, _), x in zip(cur, ws)]
        upd = [_dot(tl, jnp.concatenate([vn, vn], axis=0)) for (_, _, tl, _), vn in zip(cur, v_new)]
        for (rev, hd), x, y in zip(chains, ws, upd):
            c = chunk[rev]
            refs[rev][5][c * c_len:(c + 1) * c_len, hd * HEAD_DIM:(hd + 1) * HEAD_DIM] = x[hi] + y[lo]
        return [s_ * g_ + x[c_len:] for (_, _, _, g_), s_, x in zip(cur, states, upd)]

    npair = nchunk // 2
    states = [state_ref[n] for n in range(len(chains))]
    for s in range(npair):
        pre = phase_a([(rev, hd, npair - 1 - s if rev else s) for rev, hd in chains])
        for t in (2 * s, 2 * s + 1):
            states = phase_b(states, pre, t)
    for n, s_ in enumerate(states):
        state_ref[n] = s_


def _deltanet(q, k, v, gates, gates_t, weights, *, rows, heads):
    s = q.shape[0]
    nb = _steps(s, rows)
    nh = _steps(DN_HEADS, heads)
    assert rows % (2 * DN_CHUNK) == 0
    fwd = lambda h, b: (b, h)
    bwd = lambda h, b: (nb - 1 - b, h)
    head = lambda im: pl.BlockSpec((rows, heads * HEAD_DIM), im)
    gate = lambda im: pl.BlockSpec((rows, LANES), lambda h, b: (im(h, b)[0], 0))
    gate_t = lambda im: pl.BlockSpec((GATE_ROWS, rows), lambda h, b: (0, im(h, b)[0]))
    cast_specs, cast_shapes = _cast_specs(weights, nh * nb, lambda h, b: h * nb + b)
    out = pl.pallas_call(
        _deltanet_kernel,
        grid=(nh, nb),
        in_specs=[head(fwd), head(fwd), head(fwd), gate(fwd), gate_t(fwd),
                  head(bwd), head(bwd), head(bwd), gate(bwd), gate_t(bwd)] + cast_specs,
        out_specs=[head(fwd), head(bwd)] + cast_specs,
        out_shape=[jax.ShapeDtypeStruct(q.shape, F32), jax.ShapeDtypeStruct(q.shape, F32)] + cast_shapes,
        scratch_shapes=[pltpu.VMEM((2 * heads, HEAD_DIM, HEAD_DIM), F32)],
        compiler_params=_params("parallel", "arbitrary"),
        name="deltanet",
    )(q, k, v, gates, gates_t, q, k, v, gates, gates_t, *weights)
    return out[0], out[1], out[2:]


def _t5_bucket(rel):
    nb = NUM_BUCKETS // 2
    max_exact = nb // 2
    n = jnp.abs(rel)
    large = max_exact + (jnp.log(jnp.maximum(n, max_exact).astype(F32) / max_exact)
                         / math.log(MAX_DISTANCE / max_exact) * (nb - max_exact)).astype(jnp.int32)
    large = jnp.minimum(large, nb - 1)
    return jnp.where(rel > 0, nb, 0) + jnp.where(n < max_exact, n, large)


def _swa_bias_init(bucket_ref, rb_ref, bias_ref):
    w = WINDOW
    ri = lax.broadcasted_iota(jnp.int32, (w, 3 * w), 0)
    ci = lax.broadcasted_iota(jnp.int32, (w, 3 * w), 1)
    bucket = bucket_ref[...]
    in_band = jnp.abs(ci - w - ri) <= w
    for hd in range(SWA_HEADS):
        acc = jnp.zeros((w, 3 * w), F32)
        for b in range(NUM_BUCKETS):
            acc = jnp.where(bucket == b, rb_ref[b, hd], acc)
        bias_ref[hd] = jnp.where(in_band, acc * LOG2E, -jnp.inf)


def _swa_block(t, qblk, first_row, seq, q_ref, kext, vext, bias_ref, sinks):
    w = WINDOW
    group = SWA_HEADS // SWA_KV_HEADS
    heads = range(SWA_HEADS)
    rows = slice(t * w, (t + 1) * w)
    band = slice(t * w, (t + 3) * w)
    s = [_dot_nt(q_ref[rows, hd * HEAD_DIM:(hd + 1) * HEAD_DIM].astype(BF16), kext[hd // group][band])
         * (HEAD_DIM ** -0.5 * LOG2E) + bias_ref[hd] for hd in heads]
    if t == 0 or t == qblk - 1:
        key_pos = first_row + (t - 1) * w + lax.broadcasted_iota(jnp.int32, (w, 3 * w), 1)
        if t == 0:
            s = [jnp.where(key_pos >= 0, x, -jnp.inf) for x in s]
        if t == qblk - 1:
            s = [jnp.where(key_pos < seq, x, -jnp.inf) for x in s]
    mx = [jnp.maximum(jnp.max(x, axis=1, keepdims=True), sk) for x, sk in zip(s, sinks)]
    p = [jnp.exp2(x - m) for x, m in zip(s, mx)]
    den = [jnp.sum(x, axis=1, keepdims=True) + jnp.exp2(sk - m) for x, sk, m in zip(p, sinks, mx)]
    o = [_dot(x.astype(BF16), vext[hd // group][band]) / dn for hd, x, dn in zip(heads, p, den)]
    return jnp.concatenate([x.astype(BF16) for x in o], axis=1)


def _mix_mem_kernel(of_ref, ob_ref, z_ref, qs_ref, kp_ref, kc_ref, kn_ref, vp_ref, vc_ref, vn_ref, bucket_ref, rb_ref,
                    sink_ref, dng_ref, wout_ref, x_ref, gx_ref, wq_ref, k_ref, v_ref, wo_ref, gf_ref, *rest, seq):
    n_cast = (len(rest) - 3) // 2
    cast_in, (h2_ref, f_ref), cast_out = rest[:n_cast], rest[n_cast:n_cast + 2], rest[n_cast + 2:2 * n_cast + 2]
    bias_ref = rest[-1]
    _cast_slices(cast_in, cast_out)
    n = pl.program_id(0)
    rows_per_step = h2_ref.shape[0]
    qblk = rows_per_step // WINDOW

    @pl.when(n == 0)
    def _():
        _swa_bias_init(bucket_ref, rb_ref, bias_ref)

    kext, vext = [], []
    for kvh in range(SWA_KV_HEADS):
        kcols = slice(kvh * HEAD_DIM, (kvh + 1) * HEAD_DIM)
        kext.append(jnp.concatenate([kp_ref[:, kcols], kc_ref[:, kcols], kn_ref[:, kcols]], axis=0).astype(BF16))
        vext.append(jnp.concatenate([vp_ref[:, kcols], vc_ref[:, kcols], vn_ref[:, kcols]], axis=0).astype(BF16))
    sinks = [sink_ref[hd] * LOG2E for hd in range(SWA_HEADS)]
    for t0 in range(0, qblk, 2):
        blocks = range(t0, min(t0 + 2, qblk))
        y_sw = [_swa_block(t, qblk, n * rows_per_step, seq, qs_ref, kext, vext, bias_ref, sinks) for t in blocks]
        _mix_mem_rows([slice(t * WINDOW, (t + 1) * WINDOW) for t in blocks], y_sw, HEAD_DIM ** -0.5, of_ref, ob_ref,
                      z_ref, dng_ref, wout_ref, x_ref, gx_ref, wq_ref, k_ref, v_ref, wo_ref, gf_ref, h2_ref, f_ref)


def _mix_mem_rows(batches, y_sw, scale, of_ref, ob_ref, z_ref, dng_ref, wout_ref, x_ref, gx_ref, wq_ref, k_ref, v_ref,
                  wo_ref, gf_ref, h2_ref, f_ref):
    dn = DN_HEADS * HEAD_DIM
    y_dn = []
    for rows in batches:
        parts = []
        for hd in range(DN_HEADS):
            cols = slice(hd * HEAD_DIM, (hd + 1) * HEAD_DIM)
            z = z_ref[rows, cols]
            y = _rms(of_ref[rows, cols] + ob_ref[rows, cols], dng_ref[...]) * (z * jax.nn.sigmoid(z))
            parts.append(y.astype(BF16))
        y_dn.append(jnp.concatenate(parts, axis=1))
    h1 = [x_ref[rows, :] + _dot(a, wout_ref[:dn, :]) for rows, a in zip(batches, y_dn)]
    h1 = [h + _dot(sw, wout_ref[dn:, :]) for h, sw in zip(h1, y_sw)]
    q = [_dot(_rms(h, gx_ref[...]).astype(BF16), wq_ref[...]) for h in h1]
    att = []
    for qb in q:
        heads = []
        for hd in range(MEM_HEADS):
            cols = slice(hd * HEAD_DIM, (hd + 1) * HEAD_DIM)
            s = _dot_nt(qb[:, cols].astype(BF16), k_ref[:, cols]) * scale
            p = jnp.exp(s - jnp.max(s, axis=1, keepdims=True))
            den = jnp.sum(p, axis=1, keepdims=True)
            heads.append((_dot(p.astype(BF16), v_ref[:, cols]) / den).astype(BF16))
        att.append(jnp.concatenate(heads, axis=1))
    h2 = [h + _dot(a, wo_ref[...]) for h, a in zip(h1, att)]
    for rows, h in zip(batches, h2):
        h2_ref[rows, :] = h
        f_ref[rows, :] = _rms(h, gf_ref[...]).astype(f_ref.dtype)


def _mix_mem(o_f, o_b, proj, bucket, rel_bias, sink, dn_g, w_out, x, gx, wq, k, v, wo, gf, weights, *, tm):
    s, d = x.shape
    w = WINDOW
    dn = DN_HEADS * HEAD_DIM
    qw = SWA_HEADS * HEAD_DIM
    kvw = SWA_KV_HEADS * HEAD_DIM
    qblk = _steps(tm, w)
    nb = _steps(s, w)
    row = lambda width, col=0: pl.BlockSpec((tm, width), lambda i: (i, col))
    halo = lambda off, lo: pl.BlockSpec(
        (w, kvw), lambda i: (jnp.clip(i * qblk - 1 if lo else (i + 1) * qblk, 0, nb - 1), off // kvw))
    const = lambda a: pl.BlockSpec(a.shape, lambda i: (0, 0), pipeline_mode=pl.Buffered(1))
    smem = pl.BlockSpec(memory_space=pltpu.SMEM)
    dn_g, gx, gf = dn_g.reshape(1, HEAD_DIM), gx.reshape(1, d), gf.reshape(1, d)
    nsteps = _steps(s, tm)
    cast_specs, cast_shapes = _cast_specs(weights, nsteps, lambda i: i)
    out = pl.pallas_call(
        functools.partial(_mix_mem_kernel, seq=s),
        grid=(nsteps,),
        in_specs=[row(dn), row(dn), row(dn, Z_OFF // dn), row(qw, QSW_OFF // qw),
                  halo(KSW_OFF, True), row(kvw, KSW_OFF // kvw), halo(KSW_OFF, False),
                  halo(VSW_OFF, True), row(kvw, VSW_OFF // kvw), halo(VSW_OFF, False),
                  const(bucket), smem, smem, const(dn_g), const(w_out), row(d),
                  const(gx), const(wq), const(k), const(v), const(wo), const(gf)] + cast_specs,
        out_specs=[row(d), row(d)] + cast_specs,
        out_shape=[jax.ShapeDtypeStruct((s, d), F32), jax.ShapeDtypeStruct((s, d), BF16)] + cast_shapes,
        scratch_shapes=[pltpu.VMEM((SWA_HEADS, w, 3 * w), F32)],
        compiler_params=_params("arbitrary"),
        name="mix_mem",
    )(o_f, o_b, proj, proj, proj, proj, proj, proj, proj, proj, bucket, rel_bias, sink, dn_g, w_out, x,
      gx, wq, k, v, wo, gf, *weights)
    return out[0], out[1], out[2:]


def _glu_kernel(f_ref, wg_ref, wu_ref, o_ref):
    f = f_ref[...]
    a = _dot(f, wg_ref[...])
    o_ref[...] = (a * jax.nn.sigmoid(a) * _dot(f, wu_ref[...])).astype(o_ref.dtype)


def _ffn_glu(f, wg, wu, *, tm, tn):
    s, d = f.shape
    dff = wg.shape[1]
    return pl.pallas_call(
        _glu_kernel,
        grid=(_steps(s, tm), _steps(dff, tn)),
        in_specs=[pl.BlockSpec((tm, d), lambda i, j: (i, 0)),
                  pl.BlockSpec((d, tn), lambda i, j: (0, j)),
                  pl.BlockSpec((d, tn), lambda i, j: (0, j))],
        out_specs=pl.BlockSpec((tm, tn), lambda i, j: (i, j)),
        out_shape=jax.ShapeDtypeStruct((s, dff), BF16),
        compiler_params=_params("parallel", "parallel"),
        name="ffn_glu",
    )(f, wg, wu)


def _down_kernel(a_ref, w_ref, h_ref, g_ref, o_ref):
    half = o_ref.shape[0] // 2
    for r in (0, half):
        rows = slice(r, r + half)
        o_ref[rows, :] = _rms(h_ref[rows, :] + _dot(a_ref[rows, :], w_ref[...]), g_ref[...])


def _ffn_down(act, wd, h, g, *, tm):
    s, dff = act.shape
    d = wd.shape[1]
    return pl.pallas_call(
        _down_kernel,
        grid=(_steps(s, tm),),
        in_specs=[pl.BlockSpec((tm, dff), lambda i: (i, 0)),
                  pl.BlockSpec((dff, d), lambda i: (0, 0), pipeline_mode=pl.Buffered(1)),
                  pl.BlockSpec((tm, d), lambda i: (i, 0)),
                  pl.BlockSpec((1, d), lambda i: (0, 0))],
        out_specs=pl.BlockSpec((tm, d), lambda i: (i, 0)),
        out_shape=jax.ShapeDtypeStruct((s, d), F32),
        compiler_params=_params("parallel"),
        name="ffn_down",
    )(act, wd, h, g.reshape(1, d))


def _gate_params(a_log_f, a_log_b, dt_f, dt_b):
    fwd = jnp.stack([a_log_f, dt_f]).astype(F32)
    bwd = jnp.stack([a_log_b, dt_b]).astype(F32)
    return jnp.concatenate([jnp.zeros((2, GATE_G_F), F32), fwd, bwd, jnp.zeros((2, LANES - GATE_ROWS), F32)], axis=1)


def _pick(n, *cands):
    for c in cands:
        if n % c == 0:
            return c
    return n


def kernel(x, mem, norm_mix_g, w_in, conv_w, a_log_f, a_log_b, dt_bias_f, dt_bias_b, dn_norm_g, attn_sink, rel_bias, w_out, norm_x_g, norm_mem_g, w_q_mem, w_kv_mem, w_o_mem, norm_ffn_g, w_gate, w_up, w_down, norm_final_g):
    batch, s, d = x.shape
    assert batch == 1 and mem.shape[0] == 1 and w_in.shape[0] == 1, "single sequence, single layer"
    w = WINDOW
    rel = (jnp.arange(3 * w)[None, :] - w) - jnp.arange(w)[:, None]
    bucket = _t5_bucket(rel).astype(jnp.int32)
    mem_dim = MEM_HEADS * HEAD_DIM

    h = x.reshape(s, d)
    mem2 = mem.reshape(mem.shape[1], d)
    tm_big = _pick(s, 1024, 512, 256, 128)
    tm_mid = _pick(s, 512, 256, 128)
    w_r = _w_in_layout(w_in[0].T, cols=256)
    proj = _rms_matmul(h, norm_mix_g[0], w_r, tm=tm_big, tn=_pick(PROJ_WIDTH, 1920, 1152, 640, 128), name="in_proj",
                       w_transposed=True)

    q, k, v, gates, gates_t = _dn_prep(proj, conv_w[0], _gate_params(a_log_f[0], a_log_b[0], dt_bias_f[0], dt_bias_b[0]),
                                       rows=_pick(s, 256, 128))
    o_f, o_b, (wd16, wout16, wq16, wo16, wkv16) = _deltanet(
        q, k, v, gates, gates_t, [w_down[0], w_out[0], w_q_mem[0], w_o_mem[0], w_kv_mem[0]],
        rows=_pick(s, 512, 256, 128), heads=8)
    kv = _rms_matmul(mem2, norm_mem_g[0], wkv16, tm=mem2.shape[0], tn=2 * mem_dim, name="mem_kv", out_dtype=BF16)
    h, f, (wg16, wu16) = _mix_mem(o_f, o_b, proj, bucket, rel_bias.astype(F32), attn_sink[0].astype(F32), dn_norm_g[0],
                                  wout16, h, norm_x_g[0], wq16, kv[:, :mem_dim], kv[:, mem_dim:], wo16, norm_ffn_g[0],
                                  [w_gate[0], w_up[0]], tm=_pick(s, 256))

    act = _ffn_glu(f, wg16, wu16, tm=tm_big, tn=_pick(wg16.shape[1], 512, 128))
    out = _ffn_down(act, wd16, h, norm_final_g, tm=tm_mid)
    return out.reshape(batch, s, d)
```

```python
import functools
import math

import jax
import jax.numpy as jnp
from jax import lax
from jax.experimental import pallas as pl
from jax.experimental.pallas import tpu as pltpu

F32 = jnp.float32
BF16 = jnp.bfloat16

RMS_EPS = 1e-6
L2_EPS = 1e-6
HEAD_DIM = 128
DN_HEADS = 8
DN_CHUNK = 64
DN_SUB = 16
CONV_WIDTH = 5
CONV_HALO = 8
SWA_HEADS = 8
SWA_KV_HEADS = 2
WINDOW = 128
NUM_BUCKETS = 32
MAX_DISTANCE = 128
MEM_HEADS = 4
LOG2E = 1.4426950408889634
LANES = 128
ROW_BATCH = 128

DN_QKV = 3 * DN_HEADS * HEAD_DIM
Z_OFF = DN_QKV
QSW_OFF = Z_OFF + DN_HEADS * HEAD_DIM
KSW_OFF = QSW_OFF + SWA_HEADS * HEAD_DIM
VSW_OFF = KSW_OFF + SWA_KV_HEADS * HEAD_DIM
GATE_OFF = VSW_OFF + SWA_KV_HEADS * HEAD_DIM
PROJ_WIDTH = GATE_OFF + LANES
GATE_BETA_F, GATE_BETA_B, GATE_G_F, GATE_G_B = 0, DN_HEADS, 2 * DN_HEADS, 3 * DN_HEADS
GATE_ROWS = 4 * DN_HEADS

VMEM_LIMIT_V7X = 56 * 1024 * 1024


def _params(*sem):
    return pltpu.CompilerParams(dimension_semantics=sem, vmem_limit_bytes=VMEM_LIMIT_V7X)


def _steps(dim, tile):
    assert dim % tile == 0, f"tile {tile} does not divide {dim}"
    return dim // tile


def _dot(a, b):
    return jnp.dot(a, b, preferred_element_type=F32)


def _dot_nt(a, b):
    return lax.dot_general(a, b, (((1,), (1,)), ((), ())), preferred_element_type=F32)


def _rms(x, g):
    return x * lax.rsqrt(jnp.mean(x * x, axis=-1, keepdims=True) + RMS_EPS) * g


def _cast_specs(weights, nsteps, step_of):
    specs = [pl.BlockSpec((_steps(a.shape[0], nsteps), a.shape[1]), lambda *idx: (step_of(*idx), 0)) for a in weights]
    return specs, [jax.ShapeDtypeStruct(a.shape, BF16) for a in weights]


def _cast_slices(srcs, dsts):
    for src, dst in zip(srcs, dsts):
        dst[...] = src[...].astype(dst.dtype)


def _w_in_kernel(w_ref, o_ref):
    gate_lo = QSW_OFF
    o_ref[:gate_lo, :] = w_ref[:gate_lo, :].astype(o_ref.dtype)
    o_ref[gate_lo:GATE_OFF, :] = w_ref[gate_lo + GATE_ROWS:, :].astype(o_ref.dtype)
    o_ref[GATE_OFF:GATE_OFF + GATE_ROWS, :] = w_ref[gate_lo:gate_lo + GATE_ROWS, :].astype(o_ref.dtype)
    o_ref[GATE_OFF + GATE_ROWS:, :] = jnp.zeros((LANES - GATE_ROWS, o_ref.shape[1]), o_ref.dtype)


def _w_in_layout(w_t, *, cols):
    n, k = w_t.shape
    assert n == PROJ_WIDTH - (LANES - GATE_ROWS)
    return pl.pallas_call(
        _w_in_kernel,
        grid=(_steps(k, cols),),
        in_specs=[pl.BlockSpec((n, cols), lambda i: (0, i))],
        out_specs=pl.BlockSpec((PROJ_WIDTH, cols), lambda i: (0, i)),
        out_shape=jax.ShapeDtypeStruct((PROJ_WIDTH, k), BF16),
        compiler_params=_params("parallel"),
        name="w_in_layout",
    )(w_t)


def _rms_matmul_kernel(x_ref, g_ref, w_ref, o_ref, n_ref, *, w_transposed):
    dot = _dot_nt if w_transposed else _dot

    @pl.when(pl.program_id(1) == 0)
    def _():
        half = max(x_ref.shape[0] // 2, ROW_BATCH)
        for r in range(0, x_ref.shape[0], half):
            rows = slice(r, r + half)
            n = _rms(x_ref[rows, :], g_ref[...]).astype(n_ref.dtype)
            n_ref[rows, :] = n
            o_ref[rows, :] = dot(n, w_ref[...]).astype(o_ref.dtype)

    @pl.when(pl.program_id(1) != 0)
    def _():
        o_ref[...] = dot(n_ref[...], w_ref[...]).astype(o_ref.dtype)


def _rms_matmul(x, g, w, *, tm, tn, name, w_transposed=False, out_dtype=F32):
    m, k = x.shape
    n = w.shape[0] if w_transposed else w.shape[1]
    w_spec = pl.BlockSpec((tn, k), lambda i, j: (j, 0)) if w_transposed else pl.BlockSpec((k, tn), lambda i, j: (0, j))
    return pl.pallas_call(
        functools.partial(_rms_matmul_kernel, w_transposed=w_transposed),
        grid=(_steps(m, tm), _steps(n, tn)),
        in_specs=[pl.BlockSpec((tm, k), lambda i, j: (i, 0)),
                  pl.BlockSpec((1, k), lambda i, j: (0, 0)),
                  w_spec],
        out_specs=pl.BlockSpec((tm, tn), lambda i, j: (i, j)),
        out_shape=jax.ShapeDtypeStruct((m, n), out_dtype),
        scratch_shapes=[pltpu.VMEM((tm, k), BF16)],
        compiler_params=_params("parallel", "arbitrary"),
        name=name,
    )(x, g.reshape(1, k), w)


def _split3(x):
    hi = x.astype(BF16)
    r = x - hi.astype(F32)
    mid = r.astype(BF16)
    lo = (r - mid.astype(F32)).astype(BF16)
    return hi, mid, lo


def _prep_kernel(main_ref, prev_ref, next_ref, gate_ref, cw_ref, gp_ref,
                 q_ref, k_ref, v_ref, go_ref, gt_ref):
    i = pl.program_id(0)
    rows = main_ref.shape[0]
    halo = CONV_HALO
    pad = (CONV_WIDTH - 1) // 2

    first = i == 0
    last = i == pl.num_programs(0) - 1
    for s in range(3 * DN_HEADS):
        cols = slice(s * HEAD_DIM, (s + 1) * HEAD_DIM)
        xe = jnp.concatenate([jnp.where(first, 0.0, prev_ref[:, cols]), main_ref[:, cols],
                              jnp.where(last, 0.0, next_ref[:, cols])], axis=0)
        acc = cw_ref[pad:pad + 1, cols] * xe[halo:halo + rows]
        for j in range(CONV_WIDTH):
            if j != pad:
                shifted = pltpu.roll(xe, shift=(pad - j) % xe.shape[0], axis=0)[halo:halo + rows]
                acc = acc + cw_ref[j:j + 1, cols] * shifted
        y = acc * jax.nn.sigmoid(acc)
        if s < 2 * DN_HEADS:
            inv = lax.rsqrt(jnp.sum(y * y, axis=-1, keepdims=True) + L2_EPS)
            y = y * (inv * (HEAD_DIM ** -0.5) if s < DN_HEADS else inv)
        if s < DN_HEADS:
            q_ref[:, cols] = y
        elif s < 2 * DN_HEADS:
            k_ref[:, slice((s - DN_HEADS) * HEAD_DIM, (s - DN_HEADS + 1) * HEAD_DIM)] = y
        else:
            v_ref[:, slice((s - 2 * DN_HEADS) * HEAD_DIM, (s - 2 * DN_HEADS + 1) * HEAD_DIM)] = y

    t = gate_ref[...]
    beta = jax.nn.sigmoid(t)
    a = t + gp_ref[1:2, :]
    softplus = jnp.maximum(a, 0.0) + jnp.log1p(jnp.exp(-jnp.abs(a)))
    g = -jnp.exp(gp_ref[0:1, :]) * softplus

    ri = lax.broadcasted_iota(jnp.int32, (rows, rows), 0)
    ci = lax.broadcasted_iota(jnp.int32, (rows, rows), 1)
    shift = DN_CHUNK.bit_length() - 1
    same_chunk = (ri >> shift) == (ci >> shift)
    lower = jnp.where(same_chunk & (ci <= ri), 1.0, 0.0).astype(BF16)
    upper = jnp.where(same_chunk & (ci >= ri), 1.0, 0.0).astype(BF16)
    parts = _split3(g)
    gc_f = _dot(lower, parts[0]) + _dot(lower, parts[1]) + _dot(lower, parts[2])
    gc_b = _dot(upper, parts[0]) + _dot(upper, parts[1]) + _dot(upper, parts[2])

    col = lax.broadcasted_iota(jnp.int32, t.shape, 1)
    out = jnp.where(col < GATE_G_F, beta, jnp.where(col < GATE_G_B, gc_f, gc_b))
    go_ref[...] = out
    gt_ref[...] = out.T[0:GATE_ROWS, :]


def _dn_prep(proj, conv_w, gate_params, *, rows):
    s = proj.shape[0]
    nblk = _steps(s, rows)
    hb = rows // CONV_HALO
    last_halo = s // CONV_HALO - 1
    head_cols = DN_HEADS * HEAD_DIM
    return pl.pallas_call(
        _prep_kernel,
        grid=(nblk,),
        in_specs=[pl.BlockSpec((rows, DN_QKV), lambda i: (i, 0)),
                  pl.BlockSpec((CONV_HALO, DN_QKV), lambda i: (jnp.maximum(i * hb - 1, 0), 0)),
                  pl.BlockSpec((CONV_HALO, DN_QKV), lambda i: (jnp.minimum((i + 1) * hb, last_halo), 0)),
                  pl.BlockSpec((rows, LANES), lambda i: (i, GATE_OFF // LANES)),
                  pl.BlockSpec((CONV_WIDTH, DN_QKV), lambda i: (0, 0)),
                  pl.BlockSpec((2, LANES), lambda i: (0, 0))],
        out_specs=[pl.BlockSpec((rows, head_cols), lambda i: (i, 0)),
                   pl.BlockSpec((rows, head_cols), lambda i: (i, 0)),
                   pl.BlockSpec((rows, head_cols), lambda i: (i, 0)),
                   pl.BlockSpec((rows, LANES), lambda i: (i, 0)),
                   pl.BlockSpec((GATE_ROWS, rows), lambda i: (0, i))],
        out_shape=[jax.ShapeDtypeStruct((s, head_cols), F32),
                   jax.ShapeDtypeStruct((s, head_cols), F32),
                   jax.ShapeDtypeStruct((s, head_cols), F32),
                   jax.ShapeDtypeStruct((s, LANES), F32),
                   jax.ShapeDtypeStruct((GATE_ROWS, s), F32)],
        compiler_params=_params("parallel"),
        name="dn_prep",
    )(proj, proj, proj, proj, conv_w, gate_params)


def _unit_tri_inverse(ms, eye, same_sub, _mm):
    c = eye.shape[0]
    assert c <= 4 * DN_SUB
    mds = [jnp.where(same_sub, m, 0.0) for m in ms]
    es = [m - md for m, md in zip(ms, mds)]
    pw = [_mm(md, md) for md in mds]
    xs = [eye - md for md in mds]
    for _ in range(DN_SUB.bit_length() - 3):
        both = [_mm(jnp.concatenate([a, x], axis=0), a) for a, x in zip(pw, xs)]
        pw = [b[:c] for b in both]
        xs = [x + b[c:] for x, b in zip(xs, both)]
    xs = [x + _mm(x, a) for x, a in zip(xs, pw)]
    ns = [_mm(x, e) for x, e in zip(xs, es)]
    n2 = [_mm(n, n) for n in ns]
    ys = [_mm(n, (eye - n) + b) for n, b in zip(ns, n2)]
    return [x - _mm(y, x) for x, y in zip(xs, ys)]


def _deltanet_kernel(qf, kf, vf, gf, gtf, qb, kb, vb, gb, gtb, *rest):
    n_cast = (len(rest) - 3) // 2
    cast_in, (of_ref, ob_ref), cast_out = rest[:n_cast], rest[n_cast:n_cast + 2], rest[n_cast + 2:2 * n_cast + 2]
    state_ref = rest[-1]
    _cast_slices(cast_in, cast_out)
    c_len = DN_CHUNK
    nchunk = qf.shape[0] // c_len
    heads = qf.shape[1] // HEAD_DIM
    head0 = pl.program_id(0) * heads

    @pl.when(pl.program_id(1) == 0)
    def _():
        state_ref[...] = jnp.zeros_like(state_ref)

    p_len = 2 * c_len
    ri = lax.broadcasted_iota(jnp.int32, (c_len, p_len), 0)
    cl = lax.broadcasted_iota(jnp.int32, (c_len, p_len), 1)
    ci = cl & (c_len - 1)
    first = cl < c_len
    eye = jnp.where(ri == ci, 1.0, 0.0)
    sub_shift = DN_SUB.bit_length() - 1
    same_sub = (ri >> sub_shift) == (ci >> sub_shift)
    lane = lax.broadcasted_iota(jnp.int32, (p_len, LANES), 1)
    upper_rows = lax.broadcasted_iota(jnp.int32, (p_len, 1), 0) >= c_len
    first_tall = lax.broadcasted_iota(jnp.int32, (c_len + HEAD_DIM, p_len), 1) < c_len
    masks = {False: (ci <= ri, ci < ri), True: (ci >= ri, ci > ri)}
    refs = {False: (qf, kf, vf, gf, gtf, of_ref), True: (qb, kb, vb, gb, gtb, ob_ref)}

    def pair_mm(x, p):
        blockdiag = jnp.concatenate([jnp.where(first, p, 0.0), jnp.where(first, 0.0, p)], axis=0)
        return _dot(x.astype(BF16), blockdiag.astype(BF16))

    chains = [(rev, hd) for rev in (False, True) for hd in range(heads)]

    g_rows = {}
    for rev, hd in chains:
        g_col = head0 + hd + (GATE_G_B if rev else GATE_G_F)
        g_rows[rev, hd] = refs[rev][4][pl.ds(g_col, 1), :]

    def load(rev, hd, p):
        q_ref, k_ref, v_ref, g_ref = refs[rev][:4]
        rows = slice(p * p_len, (p + 1) * p_len)
        cols = slice(hd * HEAD_DIM, (hd + 1) * HEAD_DIM)
        gates = g_ref[rows, :]
        beta_col = head0 + hd + (GATE_BETA_B if rev else GATE_BETA_F)
        g_col = head0 + hd + (GATE_G_B if rev else GATE_G_F)
        beta = jnp.sum(jnp.where(lane == beta_col, gates, 0.0), axis=1, keepdims=True)
        gcol = jnp.sum(jnp.where(lane == g_col, gates, 0.0), axis=1, keepdims=True)
        grow = g_rows[rev, hd][:, rows]
        if rev:
            glast = (grow[:, 0:1], grow[:, c_len:c_len + 1])
        else:
            glast = (grow[:, c_len - 1:c_len], grow[:, p_len - 1:p_len])
        return q_ref[rows, cols], k_ref[rows, cols], v_ref[rows, cols], beta, gcol, grow, glast

    lo, hi = slice(0, c_len), slice(c_len, p_len)

    zeros = jnp.zeros((c_len, HEAD_DIM), BF16)

    def phase_a(inst):
        data = [load(*i) for i in inst]
        kbeta = [k * beta for (_, k, _, beta, _, _, _) in data]
        prod = []
        for kb_, (q, k, _, _, _, _, _) in zip(kbeta, data):
            lhs = jnp.concatenate([jnp.concatenate([kb_[lo], kb_[hi]], axis=1),
                                   jnp.concatenate([q[lo], q[hi]], axis=1)], axis=0).astype(BF16)
            kb16 = k.astype(BF16)
            rhs = jnp.concatenate([jnp.concatenate([kb16[lo], zeros], axis=1),
                                   jnp.concatenate([zeros, kb16[hi]], axis=1)], axis=0)
            prod.append(_dot_nt(lhs, rhs))
        decay = [jnp.exp(jnp.where(masks[rev][0], jnp.where(first, gcol[lo], gcol[hi]) - grow, -jnp.inf))
                 for (rev, _, _), (_, _, _, _, gcol, grow, _) in zip(inst, data)]
        ms = [jnp.where(masks[rev][1], p[lo] * dec, 0.0) for (rev, _, _), p, dec in zip(inst, prod, decay)]
        a_mat = [p[hi] * dec for p, dec in zip(prod, decay)]
        ts = _unit_tri_inverse(ms, eye, same_sub, pair_mm)
        egs = [jnp.exp(gcol) for (_, _, _, _, gcol, _, _) in data]
        uw = []
        for t, kb_, eg, (_, _, v, beta, _, _, _) in zip(ts, kbeta, egs, data):
            lhs = jnp.concatenate([jnp.where(first, t, 0.0), jnp.where(first, 0.0, t)], axis=0).astype(BF16)
            rhs = jnp.concatenate([(v * beta).astype(BF16), (kb_ * eg).astype(BF16)], axis=1)
            uw.append(_dot(lhs, rhs))
        qdec = [q * eg for eg, (q, _, _, _, _, _, _) in zip(egs, data)]
        kdec_t = [(k * jnp.exp(jnp.where(upper_rows, glast[1], glast[0]) - gcol)).T
                  for (_, k, _, _, gcol, _, glast) in data]
        tall = [jnp.concatenate([a_, kt], axis=0) for a_, kt in zip(a_mat, kdec_t)]
        pre = {}
        for (rev, hd, p), x, qd, tl, (_, _, _, _, _, _, glast) in zip(inst, uw, qdec, tall, data):
            for half, rs in enumerate((lo, hi)):
                wq = jnp.concatenate([x[rs, HEAD_DIM:], qd[rs]], axis=0).astype(BF16)
                keep = first_tall if half == 0 else ~first_tall
                pre[rev, hd, 2 * p + half] = (x[rs, :HEAD_DIM], wq, jnp.where(keep, tl, 0.0).astype(BF16),
                                              jnp.exp(glast[half]))
        return pre

    def phase_b(states, pre, t):
        chunk = {False: t, True: nchunk - 1 - t}
        cur = [pre[rev, hd, chunk[rev]] for rev, hd in chains]
        ws = [_dot(wq, s_.astype(BF16)) for (_, wq, _, _), s_ in zip(cur, states)]
        v_new = [(u - x[lo]).astype(BF16) for (u, _, _, _), x in zip(cur, ws)]
        upd = [_dot(tl, jnp.concatenate([vn, vn], axis=0)) for (_, _, tl, _), vn in zip(cur, v_new)]
        for (rev, hd), x, y in zip(chains, ws, upd):
            c = chunk[rev]
            refs[rev][5][c * c_len:(c + 1) * c_len, hd * HEAD_DIM:(hd + 1) * HEAD_DIM] = x[hi] + y[lo]
        return [s_ * g_ + x[c_len:] for (_, _, _, g_), s_, x in zip(cur, states, upd)]

    npair = nchunk // 2
    states = [state_ref[n] for n in range(len(chains))]
    for s in range(npair):
        pre = phase_a([(rev, hd, npair - 1 - s if rev else s) for rev, hd in chains])
        for t in (2 * s, 2 * s + 1):
            states = phase_b(states, pre, t)
    for n, s_ in enumerate(states):
        state_ref[n] = s_


def _deltanet(q, k, v, gates, gates_t, weights, *, rows, heads):
    s = q.shape[0]
    nb = _steps(s, rows)
    nh = _steps(DN_HEADS, heads)
    assert rows % (2 * DN_CHUNK) == 0
    fwd = lambda h, b: (b, h)
    bwd = lambda h, b: (nb - 1 - b, h)
    head = lambda im: pl.BlockSpec((rows, heads * HEAD_DIM), im)
    gate = lambda im: pl.BlockSpec((rows, LANES), lambda h, b: (im(h, b)[0], 0))
    gate_t = lambda im: pl.BlockSpec((GATE_ROWS, rows), lambda h, b: (0, im(h, b)[0]))
    cast_specs, cast_shapes = _cast_specs(weights, nh * nb, lambda h, b: h * nb + b)
    out = pl.pallas_call(
        _deltanet_kernel,
        grid=(nh, nb),
        in_specs=[head(fwd), head(fwd), head(fwd), gate(fwd), gate_t(fwd),
                  head(bwd), head(bwd), head(bwd), gate(bwd), gate_t(bwd)] + cast_specs,
        out_specs=[head(fwd), head(bwd)] + cast_specs,
        out_shape=[jax.ShapeDtypeStruct(q.shape, F32), jax.ShapeDtypeStruct(q.shape, F32)] + cast_shapes,
        scratch_shapes=[pltpu.VMEM((2 * heads, HEAD_DIM, HEAD_DIM), F32)],
        compiler_params=_params("parallel", "arbitrary"),
        name="deltanet",
    )(q, k, v, gates, gates_t, q, k, v, gates, gates_t, *weights)
    return out[0], out[1], out[2:]


def _t5_bucket(rel):
    nb = NUM_BUCKETS // 2
    max_exact = nb // 2
    n = jnp.abs(rel)
    large = max_exact + (jnp.log(jnp.maximum(n, max_exact).astype(F32) / max_exact)
                         / math.log(MAX_DISTANCE / max_exact) * (nb - max_exact)).astype(jnp.int32)
    large = jnp.minimum(large, nb - 1)
    return jnp.where(rel > 0, nb, 0) + jnp.where(n < max_exact, n, large)


def _swa_bias_init(bucket_ref, rb_ref, bias_ref):
    w = WINDOW
    ri = lax.broadcasted_iota(jnp.int32, (w, 3 * w), 0)
    ci = lax.broadcasted_iota(jnp.int32, (w, 3 * w), 1)
    bucket = bucket_ref[...]
    in_band = jnp.abs(ci - w - ri) <= w
    for hd in range(SWA_HEADS):
        acc = jnp.zeros((w, 3 * w), F32)
        for b in range(NUM_BUCKETS):
            acc = jnp.where(bucket == b, rb_ref[b, hd], acc)
        bias_ref[hd] = jnp.where(in_band, acc * LOG2E, -jnp.inf)


def _swa_block(t, qblk, first_row, seq, q_ref, kext, vext, bias_ref, sinks):
    w = WINDOW
    group = SWA_HEADS // SWA_KV_HEADS
    heads = range(SWA_HEADS)
    rows = slice(t * w, (t + 1) * w)
    band = slice(t * w, (t + 3) * w)
    s = [_dot_nt(q_ref[rows, hd * HEAD_DIM:(hd + 1) * HEAD_DIM].astype(BF16), kext[hd // group][band])
         * (HEAD_DIM ** -0.5 * LOG2E) + bias_ref[hd] for hd in heads]
    if t == 0 or t == qblk - 1:
        key_pos = first_row + (t - 1) * w + lax.broadcasted_iota(jnp.int32, (w, 3 * w), 1)
        if t == 0:
            s = [jnp.where(key_pos >= 0, x, -jnp.inf) for x in s]
        if t == qblk - 1:
            s = [jnp.where(key_pos < seq, x, -jnp.inf) for x in s]
    mx = [jnp.maximum(jnp.max(x, axis=1, keepdims=True), sk) for x, sk in zip(s, sinks)]
    p = [jnp.exp2(x - m) for x, m in zip(s, mx)]
    den = [jnp.sum(x, axis=1, keepdims=True) + jnp.exp2(sk - m) for x, sk, m in zip(p, sinks, mx)]
    o = [_dot(x.astype(BF16), vext[hd // group][band]) / dn for hd, x, dn in zip(heads, p, den)]
    return jnp.concatenate([x.astype(BF16) for x in o], axis=1)


def _mix_mem_kernel(of_ref, ob_ref, z_ref, qs_ref, kp_ref, kc_ref, kn_ref, vp_ref, vc_ref, vn_ref, bucket_ref, rb_ref,
                    sink_ref, dng_ref, wout_ref, x_ref, gx_ref, wq_ref, mem_ref, gm_ref, wkv_ref, wo_ref, gf_ref, *rest,
                    seq):
    n_cast = (len(rest) - 4) // 2
    cast_in, (h2_ref, f_ref), cast_out = rest[:n_cast], rest[n_cast:n_cast + 2], rest[n_cast + 2:2 * n_cast + 2]
    bias_ref, kv_ref = rest[-2:]
    _cast_slices(cast_in, cast_out)
    n = pl.program_id(0)
    rows_per_step = h2_ref.shape[0]
    qblk = rows_per_step // WINDOW
    mem_dim = MEM_HEADS * HEAD_DIM
    k_ref, v_ref = kv_ref.at[:, pl.ds(0, mem_dim)], kv_ref.at[:, pl.ds(mem_dim, mem_dim)]

    @pl.when(n == 0)
    def _():
        _swa_bias_init(bucket_ref, rb_ref, bias_ref)
        kv_ref[...] = _dot(_rms(mem_ref[...], gm_ref[...]).astype(BF16), wkv_ref[...]).astype(kv_ref.dtype)

    kext, vext = [], []
    for kvh in range(SWA_KV_HEADS):
        kcols = slice(kvh * HEAD_DIM, (kvh + 1) * HEAD_DIM)
        kext.append(jnp.concatenate([kp_ref[:, kcols], kc_ref[:, kcols], kn_ref[:, kcols]], axis=0).astype(BF16))
        vext.append(jnp.concatenate([vp_ref[:, kcols], vc_ref[:, kcols], vn_ref[:, kcols]], axis=0).astype(BF16))
    sinks = [sink_ref[hd] * LOG2E for hd in range(SWA_HEADS)]
    for t0 in range(0, qblk, 2):
        blocks = range(t0, min(t0 + 2, qblk))
        y_sw = [_swa_block(t, qblk, n * rows_per_step, seq, qs_ref, kext, vext, bias_ref, sinks) for t in blocks]
        _mix_mem_rows([slice(t * WINDOW, (t + 1) * WINDOW) for t in blocks], y_sw, HEAD_DIM ** -0.5, of_ref, ob_ref,
                      z_ref, dng_ref, wout_ref, x_ref, gx_ref, wq_ref, k_ref, v_ref, wo_ref, gf_ref, h2_ref, f_ref)


def _mix_mem_rows(batches, y_sw, scale, of_ref, ob_ref, z_ref, dng_ref, wout_ref, x_ref, gx_ref, wq_ref, k_ref, v_ref,
                  wo_ref, gf_ref, h2_ref, f_ref):
    dn = DN_HEADS * HEAD_DIM
    y_dn = []
    for rows in batches:
        parts = []
        for hd in range(DN_HEADS):
            cols = slice(hd * HEAD_DIM, (hd + 1) * HEAD_DIM)
            z = z_ref[rows, cols]
            y = _rms(of_ref[rows, cols] + ob_ref[rows, cols], dng_ref[...]) * (z * jax.nn.sigmoid(z))
            parts.append(y.astype(BF16))
        y_dn.append(jnp.concatenate(parts, axis=1))
    h1 = [x_ref[rows, :] + _dot(a, wout_ref[:dn, :]) for rows, a in zip(batches, y_dn)]
    h1 = [h + _dot(sw, wout_ref[dn:, :]) for h, sw in zip(h1, y_sw)]
    q = [_dot(_rms(h, gx_ref[...]).astype(BF16), wq_ref[...]) for h in h1]
    att = []
    for qb in q:
        heads = []
        for hd in range(MEM_HEADS):
            cols = slice(hd * HEAD_DIM, (hd + 1) * HEAD_DIM)
            s = _dot_nt(qb[:, cols].astype(BF16), k_ref[:, cols]) * scale
            p = jnp.exp(s - jnp.max(s, axis=1, keepdims=True))
            den = jnp.sum(p, axis=1, keepdims=True)
            heads.append((_dot(p.astype(BF16), v_ref[:, cols]) / den).astype(BF16))
        att.append(jnp.concatenate(heads, axis=1))
    h2 = [h + _dot(a, wo_ref[...]) for h, a in zip(h1, att)]
    for rows, h in zip(batches, h2):
        h2_ref[rows, :] = h
        f_ref[rows, :] = _rms(h, gf_ref[...]).astype(f_ref.dtype)


def _mix_mem(o_f, o_b, proj, bucket, rel_bias, sink, dn_g, w_out, x, gx, wq, mem, gm, wkv, wo, gf, weights, *, tm):
    s, d = x.shape
    gm = gm.reshape(1, d)
    w = WINDOW
    dn = DN_HEADS * HEAD_DIM
    qw = SWA_HEADS * HEAD_DIM
    kvw = SWA_KV_HEADS * HEAD_DIM
    qblk = _steps(tm, w)
    nb = _steps(s, w)
    row = lambda width, col=0: pl.BlockSpec((tm, width), lambda i: (i, col))
    halo = lambda off, lo: pl.BlockSpec(
        (w, kvw), lambda i: (jnp.clip(i * qblk - 1 if lo else (i + 1) * qblk, 0, nb - 1), off // kvw))
    const = lambda a: pl.BlockSpec(a.shape, lambda i: (0, 0), pipeline_mode=pl.Buffered(1))
    smem = pl.BlockSpec(memory_space=pltpu.SMEM)
    dn_g, gx, gf = dn_g.reshape(1, HEAD_DIM), gx.reshape(1, d), gf.reshape(1, d)
    nsteps = _steps(s, tm)
    cast_specs, cast_shapes = _cast_specs(weights, nsteps, lambda i: i)
    out = pl.pallas_call(
        functools.partial(_mix_mem_kernel, seq=s),
        grid=(nsteps,),
        in_specs=[row(dn), row(dn), row(dn, Z_OFF // dn), row(qw, QSW_OFF // qw),
                  halo(KSW_OFF, True), row(kvw, KSW_OFF // kvw), halo(KSW_OFF, False),
                  halo(VSW_OFF, True), row(kvw, VSW_OFF // kvw), halo(VSW_OFF, False),
                  const(bucket), smem, smem, const(dn_g), const(w_out), row(d),
                  const(gx), const(wq), const(mem), const(gm), const(wkv), const(wo), const(gf)] + cast_specs,
        out_specs=[row(d), row(d)] + cast_specs,
        out_shape=[jax.ShapeDtypeStruct((s, d), F32), jax.ShapeDtypeStruct((s, d), BF16)] + cast_shapes,
        scratch_shapes=[pltpu.VMEM((SWA_HEADS, w, 3 * w), F32), pltpu.VMEM((mem.shape[0], wkv.shape[1]), BF16)],
        compiler_params=_params("arbitrary"),
        name="mix_mem",
    )(o_f, o_b, proj, proj, proj, proj, proj, proj, proj, proj, bucket, rel_bias, sink, dn_g, w_out, x,
      gx, wq, mem, gm, wkv, wo, gf, *weights)
    return out[0], out[1], out[2:]


def _glu_kernel(f_ref, wg_ref, wu_ref, o_ref):
    f = f_ref[...]
    a = _dot(f, wg_ref[...])
    o_ref[...] = (a * jax.nn.sigmoid(a) * _dot(f, wu_ref[...])).astype(o_ref.dtype)


def _ffn_glu(f, wg, wu, *, tm, tn):
    s, d = f.shape
    dff = wg.shape[1]
    return pl.pallas_call(
        _glu_kernel,
        grid=(_steps(s, tm), _steps(dff, tn)),
        in_specs=[pl.BlockSpec((tm, d), lambda i, j: (i, 0)),
                  pl.BlockSpec((d, tn), lambda i, j: (0, j)),
                  pl.BlockSpec((d, tn), lambda i, j: (0, j))],
        out_specs=pl.BlockSpec((tm, tn), lambda i, j: (i, j)),
        out_shape=jax.ShapeDtypeStruct((s, dff), BF16),
        compiler_params=_params("parallel", "parallel"),
        name="ffn_glu",
    )(f, wg, wu)


def _down_kernel(a_ref, w_ref, h_ref, g_ref, o_ref):
    half = o_ref.shape[0] // 2
    for r in (0, half):
        rows = slice(r, r + half)
        o_ref[rows, :] = _rms(h_ref[rows, :] + _dot(a_ref[rows, :], w_ref[...]), g_ref[...])


def _ffn_down(act, wd, h, g, *, tm):
    s, dff = act.shape
    d = wd.shape[1]
    return pl.pallas_call(
        _down_kernel,
        grid=(_steps(s, tm),),
        in_specs=[pl.BlockSpec((tm, dff), lambda i: (i, 0)),
                  pl.BlockSpec((dff, d), lambda i: (0, 0), pipeline_mode=pl.Buffered(1)),
                  pl.BlockSpec((tm, d), lambda i: (i, 0)),
                  pl.BlockSpec((1, d), lambda i: (0, 0))],
        out_specs=pl.BlockSpec((tm, d), lambda i: (i, 0)),
        out_shape=jax.ShapeDtypeStruct((s, d), F32),
        compiler_params=_params("parallel"),
        name="ffn_down",
    )(act, wd, h, g.reshape(1, d))


def _gate_params(a_log_f, a_log_b, dt_f, dt_b):
    fwd = jnp.stack([a_log_f, dt_f]).astype(F32)
    bwd = jnp.stack([a_log_b, dt_b]).astype(F32)
    return jnp.concatenate([jnp.zeros((2, GATE_G_F), F32), fwd, bwd, jnp.zeros((2, LANES - GATE_ROWS), F32)], axis=1)


def _pick(n, *cands):
    for c in cands:
        if n % c == 0:
            return c
    return n


def kernel(x, mem, norm_mix_g, w_in, conv_w, a_log_f, a_log_b, dt_bias_f, dt_bias_b, dn_norm_g, attn_sink, rel_bias, w_out, norm_x_g, norm_mem_g, w_q_mem, w_kv_mem, w_o_mem, norm_ffn_g, w_gate, w_up, w_down, norm_final_g):
    batch, s, d = x.shape
    assert batch == 1 and mem.shape[0] == 1 and w_in.shape[0] == 1, "single sequence, single layer"
    w = WINDOW
    rel = (jnp.arange(3 * w)[None, :] - w) - jnp.arange(w)[:, None]
    bucket = _t5_bucket(rel).astype(jnp.int32)

    h = x.reshape(s, d)
    mem2 = mem.reshape(mem.shape[1], d)
    tm_big = _pick(s, 1024, 512, 256, 128)
    tm_mid = _pick(s, 512, 256, 128)
    w_r = _w_in_layout(w_in[0].T, cols=256)
    proj = _rms_matmul(h, norm_mix_g[0], w_r, tm=tm_big, tn=_pick(PROJ_WIDTH, 1920, 1152, 640, 128), name="in_proj",
                       w_transposed=True)

    q, k, v, gates, gates_t = _dn_prep(proj, conv_w[0], _gate_params(a_log_f[0], a_log_b[0], dt_bias_f[0], dt_bias_b[0]),
                                       rows=_pick(s, 256, 128))
    o_f, o_b, (wd16, wout16, wq16, wo16, wkv16) = _deltanet(
        q, k, v, gates, gates_t, [w_down[0], w_out[0], w_q_mem[0], w_o_mem[0], w_kv_mem[0]],
        rows=_pick(s, 512, 256, 128), heads=8)
    h, f, (wg16, wu16) = _mix_mem(o_f, o_b, proj, bucket, rel_bias.astype(F32), attn_sink[0].astype(F32), dn_norm_g[0],
                                  wout16, h, norm_x_g[0], wq16, mem2, norm_mem_g[0], wkv16, wo16, norm_ffn_g[0],
                                  [w_gate[0], w_up[0]], tm=_pick(s, 256))

    act = _ffn_glu(f, wg16, wu16, tm=tm_big, tn=_pick(wg16.shape[1], 512, 128))
    out = _ffn_down(act, wd16, h, norm_final_g, tm=tm_mid)
    return out.reshape(batch, s, d)
```
